```python
import math
import jax, jax.numpy as jnp
from jax import lax
import numpy as np

D_MODEL = 1024
BATCH = 8
SEQ = 4096
DEPTH = 4

CHUNK = 64
Q_BLOCK = 128
N_MIXERS = 2
SB_HEADS = 16
SB_HEAD_DIM = D_MODEL // SB_HEADS
DIFF_HEADS = 8
DIFF_HEAD_DIM = D_MODEL // (2 * DIFF_HEADS)
DIFF_V_DIM = 2 * DIFF_HEAD_DIM
ROPE_THETA = 500000.0
ROT_DIM = DIFF_HEAD_DIM // 4
D_FF = -(-8 * D_MODEL // (3 * 256)) * 256
EPS = 1e-6
NEG_INF = -1e30
N_SB = (DEPTH + 1) // 2
N_DIFF = DEPTH // 2

kernel_name = 'hybrid_stickbreak_diffattn_swiglu'


def rms_norm(x, g):
    xf = x.astype(jnp.float32)
    y = xf * lax.rsqrt(jnp.mean(xf * xf, axis=-1, keepdims=True) + EPS)
    return (y * g.astype(jnp.float32)).astype(x.dtype)


def partial_rope(x, positions):
    half = ROT_DIM // 2
    inv_freq = ROPE_THETA ** (-jnp.arange(0, ROT_DIM, 2, dtype=jnp.float32) / ROT_DIM)
    ang = positions.astype(jnp.float32)[..., None] * inv_freq
    cos = jnp.cos(ang)[:, :, None, :]
    sin = jnp.sin(ang)[:, :, None, :]
    xr = x[..., :ROT_DIM].astype(jnp.float32)
    x1, x2 = xr[..., :half], xr[..., half:]
    rot = jnp.concatenate([x1 * cos - x2 * sin, x2 * cos + x1 * sin], axis=-1)
    return jnp.concatenate([rot.astype(x.dtype), x[..., ROT_DIM:]], axis=-1)


def _query_blocks(t):
    b, s, h, d = t.shape
    return jnp.moveaxis(t.reshape(b, s // Q_BLOCK, Q_BLOCK, h, d), 1, 0)


def _merge_blocks(o):
    nb, b, qb, h, d = o.shape
    return jnp.moveaxis(o, 0, 1).reshape(b, nb * qb, h * d)


def stick_breaking_attention(q, k, v):
    seq = k.shape[1]
    scale = SB_HEAD_DIM ** -0.5
    key_pos = jnp.arange(seq, dtype=jnp.int32)

    def block(args):
        q_blk, blk = args
        q_pos = blk * Q_BLOCK + jnp.arange(Q_BLOCK, dtype=jnp.int32)
        z = jnp.einsum('bqhd,bkhd->bhqk', q_blk, k, preferred_element_type=jnp.float32) * scale
        earlier = key_pos[None, :] < q_pos[:, None]
        log_keep = jnp.where(earlier, jax.nn.log_sigmoid(-z), 0.0)
        log_after = lax.cumsum(log_keep, axis=3, reverse=True) - log_keep
        weight = jnp.where(earlier, jnp.exp(jax.nn.log_sigmoid(z) + log_after), 0.0)
        return jnp.einsum('bhqk,bkhd->bqhd', weight.astype(v.dtype), v)

    out = lax.map(block, (_query_blocks(q), jnp.arange(seq // Q_BLOCK, dtype=jnp.int32)))
    return _merge_blocks(out)


def differential_attention(q1, q2, k1, k2, v, lam):
    seq = k1.shape[1]
    scale = DIFF_HEAD_DIM ** -0.5
    key_chunk = jnp.arange(seq, dtype=jnp.int32) // CHUNK

    def block(args):
        q1_blk, q2_blk, blk = args
        q_chunk = (blk * Q_BLOCK + jnp.arange(Q_BLOCK, dtype=jnp.int32)) // CHUNK
        visible = key_chunk[None, :] <= q_chunk[:, None]

        def attn_map(qb, kk):
            s = jnp.einsum('bqhd,bkhd->bhqk', qb, kk, preferred_element_type=jnp.float32) * scale
            return jax.nn.softmax(jnp.where(visible, s, NEG_INF), axis=-1)

        a = attn_map(q1_blk, k1) - lam * attn_map(q2_blk, k2)
        return jnp.einsum('bhqk,bkhd->bqhd', a.astype(v.dtype), v)

    out = lax.map(block, (_query_blocks(q1), _query_blocks(q2),
                          jnp.arange(seq // Q_BLOCK, dtype=jnp.int32)))
    return _merge_blocks(out)


def stick_breaking_mixer(h, w_in, w_out):
    b, s, _ = h.shape
    q, k, v = jnp.split(jnp.einsum('bsd,de->bse', h, w_in), 3, axis=-1)
    shape = (b, s, SB_HEADS, SB_HEAD_DIM)
    o = stick_breaking_attention(q.reshape(shape), k.reshape(shape), v.reshape(shape))
    return jnp.einsum('bse,ed->bsd', o, w_out)


def diff_mixer(h, positions, w_in, w_out, q_g, k_g, lq1, lk1, lq2, lk2, sub_g, layer_idx):
    b, s, _ = h.shape
    q, k, v = jnp.split(jnp.einsum('bsd,de->bse', h, w_in), 3, axis=-1)
    q = rms_norm(q.reshape(b, s, DIFF_HEADS, 2, DIFF_HEAD_DIM), q_g)
    k = rms_norm(k.reshape(b, s, DIFF_HEADS, 2, DIFF_HEAD_DIM), k_g)
    q1 = partial_rope(q[..., 0, :], positions)
    q2 = partial_rope(q[..., 1, :], positions)
    k1 = partial_rope(k[..., 0, :], positions)
    k2 = partial_rope(k[..., 1, :], positions)
    v = v.reshape(b, s, DIFF_HEADS, DIFF_V_DIM)
    lam_init = 0.8 - 0.6 * math.exp(-0.3 * layer_idx)
    lam = (jnp.exp(jnp.sum(lq1.astype(jnp.float32) * lk1.astype(jnp.float32)))
           - jnp.exp(jnp.sum(lq2.astype(jnp.float32) * lk2.astype(jnp.float32))) + lam_init)
    o = differential_attention(q1, q2, k1, k2, v, lam)
    o = rms_norm(o.reshape(b, s, DIFF_HEADS, DIFF_V_DIM), sub_g) * (1.0 - lam_init)
    return jnp.einsum('bse,ed->bsd', o.reshape(b, s, DIFF_HEADS * DIFF_V_DIM), w_out)


def swiglu_ffn(h, w_gate, w_up, w_down):
    g = jnp.einsum('bsd,df->bsf', h, w_gate)
    u = jnp.einsum('bsd,df->bsf', h, w_up)
    return jnp.einsum('bsf,fd->bsd', jax.nn.silu(g) * u, w_down)


def setup_inputs(seed: int = 0) -> dict:
    key = jax.random.key(seed)
    ks = jax.random.split(key, 16)
    f32 = jnp.float32
    d_attn = 3 * D_MODEL

    def normal(k, shape, scale):
        return jax.random.normal(k, shape, dtype=f32) * scale

    def gain(k, shape):
        return 1.0 + 0.02 * jax.random.normal(k, shape, dtype=f32)

    x = jax.random.normal(ks[0], (BATCH, SEQ, D_MODEL), dtype=f32)
    start = jax.random.randint(ks[1], (BATCH, 1), 0, 4096, dtype=jnp.int32)
    positions = start + jnp.arange(SEQ, dtype=jnp.int32)[None, :]
    return {
        'x': x,
        'positions': positions,
        'attn_norm': gain(ks[2], (DEPTH, D_MODEL)),
        'w_in': normal(ks[3], (DEPTH, D_MODEL, d_attn), D_MODEL ** -0.5),
        'w_out': normal(ks[4], (DEPTH, D_MODEL, D_MODEL), D_MODEL ** -0.5),
        'q_norm': gain(ks[5], (N_DIFF, DIFF_HEAD_DIM)),
        'k_norm': gain(ks[6], (N_DIFF, DIFF_HEAD_DIM)),
        'lambda_q1': normal(ks[7], (N_DIFF, DIFF_HEAD_DIM), 0.1),
        'lambda_k1': normal(ks[8], (N_DIFF, DIFF_HEAD_DIM), 0.1),
        'lambda_q2': normal(ks[9], (N_DIFF, DIFF_HEAD_DIM), 0.1),
        'lambda_k2': normal(ks[10], (N_DIFF, DIFF_HEAD_DIM), 0.1),
        'sub_norm': gain(ks[11], (N_DIFF, DIFF_V_DIM)),
        'ffn_norm': gain(ks[12], (DEPTH, D_MODEL)),
        'w_gate': normal(ks[13], (DEPTH, D_MODEL, D_FF), D_MODEL ** -0.5),
        'w_up': normal(ks[14], (DEPTH, D_MODEL, D_FF), D_MODEL ** -0.5),
        'w_down': normal(ks[15], (DEPTH, D_FF, D_MODEL), D_FF ** -0.5),
    }


def reference(x, positions, attn_norm, w_in, w_out, q_norm, k_norm, lambda_q1, lambda_k1,
              lambda_q2, lambda_k2, sub_norm, ffn_norm, w_gate, w_up, w_down):
    for i in range(DEPTH):
        h = rms_norm(x, attn_norm[i])
        if i % N_MIXERS == 0:
            mix = stick_breaking_mixer(h, w_in[i], w_out[i])
        else:
            j = i // N_MIXERS
            mix = diff_mixer(h, positions, w_in[i], w_out[i], q_norm[j], k_norm[j],
                             lambda_q1[j], lambda_k1[j], lambda_q2[j], lambda_k2[j],
                             sub_norm[j], i)
        x = x + mix
        x = x + swiglu_ffn(rms_norm(x, ffn_norm[i]), w_gate[i], w_up[i], w_down[i])
    return x
```

```python
import functools
import math

import jax
import jax.numpy as jnp
from jax import lax
from jax.experimental import pallas as pl
from jax.experimental.pallas import tpu as pltpu

F32 = jnp.float32
BF16 = jnp.bfloat16

N_MIXERS = 2
SB_HEAD_DIM = 64
DIFF_HEAD_DIM = 64
DIFF_V_DIM = 2 * DIFF_HEAD_DIM
CHUNK = 64
ROPE_THETA = 500000.0
ROT_DIM = DIFF_HEAD_DIM // 4
EPS = 1e-6
NEG_INF = -1e30

LANES = 128
VMEM_LIMIT_BYTES = 56 * 1024 * 1024

TOKEN_TILE = 512
SB_Q_TILE = 256
SB_K_TILE = 128
DIFF_TILE = 256
FFN_CHUNKS = ((0, 1024), (1024, 2048), (2048, 2816))

F32_EXP_UNDERFLOW = -104.0


def _rms(x, gain):
    ms = jnp.mean(x * x, axis=-1, keepdims=True)
    return x * lax.rsqrt(ms + EPS) * gain


def _const_spec(shape):
    nd = len(shape)
    return pl.BlockSpec(shape, lambda *_: (0,) * nd, pipeline_mode=pl.Buffered(1))


def _qkv_kernel(x_ref, g_ref, wqk_ref, wvt_ref, q_ref, k_ref, vt_ref, *, d_model, scale):
    h = _rms(x_ref[...], g_ref[...]).astype(BF16)
    q = jnp.dot(h, wqk_ref[:, :d_model], preferred_element_type=F32)
    q_ref[...] = (q * scale).astype(BF16)
    k_ref[...] = jnp.dot(h, wqk_ref[:, d_model:], preferred_element_type=F32).astype(BF16)
    vt = lax.dot_general(wvt_ref[...], h, (((1,), (1,)), ((), ())), preferred_element_type=F32)
    vt_ref[0] = vt.astype(BF16)


def _qk_norm_rope(t, gain, cos, sin_lo, sin_hi, lane_lo):
    sq = t * t
    lo = jnp.sum(jnp.where(lane_lo, sq, 0.0), axis=-1, keepdims=True)
    hi = jnp.sum(jnp.where(lane_lo, 0.0, sq), axis=-1, keepdims=True)
    inv = 1.0 / DIFF_HEAD_DIM
    r = jnp.where(lane_lo, lax.rsqrt(lo * inv + EPS), lax.rsqrt(hi * inv + EPS))
    y = t * r * gain
    half = ROT_DIM // 2
    return y * cos + pltpu.roll(y, LANES - half, 1) * sin_lo + pltpu.roll(y, half, 1) * sin_hi


def _qkv_diff_kernel(x_ref, g_ref, wqk_ref, wvt_ref, qg_ref, kg_ref, cos_ref, sl_ref, sh_ref,
                     q_ref, k_ref, vt_ref, *, d_model, scale):
    h = _rms(x_ref[...], g_ref[...]).astype(BF16)
    cos, sin_lo, sin_hi = cos_ref[...], sl_ref[...], sh_ref[...]
    lane_lo = lax.broadcasted_iota(jnp.int32, cos.shape, 1) < DIFF_HEAD_DIM
    for out_ref, gain_ref, col0, mult in ((q_ref, qg_ref, 0, scale), (k_ref, kg_ref, d_model, 1.0)):
        t = jnp.dot(h, wqk_ref[:, col0:col0 + d_model], preferred_element_type=F32)
        gain = gain_ref[...]
        for c in range(d_model // LANES):
            sl = slice(c * LANES, (c + 1) * LANES)
            y = _qk_norm_rope(t[:, sl], gain, cos, sin_lo, sin_hi, lane_lo)
            out_ref[:, sl] = (y * mult).astype(BF16)
    vt = lax.dot_general(wvt_ref[...], h, (((1,), (1,)), ((), ())), preferred_element_type=F32)
    vt_ref[0] = vt.astype(BF16)


def _qkv_proj(x2, gain, wqk, wvt, batch, seq, scale, diff_args=None):
    n, d = x2.shape
    tm = TOKEN_TILE
    tiles_per_seq = seq // tm
    row_spec = pl.BlockSpec((tm, d), lambda i: (i, 0))
    in_specs = [row_spec, _const_spec((1, d)), _const_spec(wqk.shape), _const_spec(wvt.shape)]
    args = [x2, gain, wqk, wvt]
    if diff_args is None:
        body = functools.partial(_qkv_kernel, d_model=d, scale=scale)
        name = "qkv_sb"
    else:
        qg, kg, cos, sin_lo, sin_hi = diff_args
        tab_spec = pl.BlockSpec((tm, LANES), lambda i: (i, 0))
        in_specs += [_const_spec((1, LANES)), _const_spec((1, LANES)), tab_spec, tab_spec, tab_spec]
        args += [qg, kg, cos, sin_lo, sin_hi]
        body = functools.partial(_qkv_diff_kernel, d_model=d, scale=scale)
        name = "qkv_diff"
    return pl.pallas_call(
        body,
        grid=(n // tm,),
        in_specs=in_specs,
        out_specs=[row_spec, row_spec,
                   pl.BlockSpec((1, d, tm), lambda i: (i // tiles_per_seq, 0, i % tiles_per_seq))],
        out_shape=[jax.ShapeDtypeStruct((n, d), BF16), jax.ShapeDtypeStruct((n, d), BF16),
                   jax.ShapeDtypeStruct((batch, d, seq), BF16)],
        compiler_params=pltpu.CompilerParams(dimension_semantics=("arbitrary",),
                                             vmem_limit_bytes=VMEM_LIMIT_BYTES),
        name=name,
    )(*args)


def _sb_attn_kernel(q_ref, k_ref, vt_ref, o_ref, acc_ref, c_ref):
    tq, tk = SB_Q_TILE, SB_K_TILE
    qi = pl.program_id(2)
    q = q_ref[...]
    lane = lax.broadcasted_iota(jnp.int32, q.shape, 1)
    r_kk = lax.broadcasted_iota(jnp.int32, (tk, tk), 0)
    c_kk = lax.broadcasted_iota(jnp.int32, (tk, tk), 1)
    later = jnp.where(c_kk > r_kk, 1.0, 0.0).astype(BF16)
    r_kq = lax.broadcasted_iota(jnp.int32, (tk, tq), 0)
    c_kq = lax.broadcasted_iota(jnp.int32, (tk, tq), 1)
    row_o = lax.broadcasted_iota(jnp.int32, (LANES, tq), 0)
    blocks_per_tile = tq // tk

    def block(j, qm, diagonal):
        kb = k_ref[pl.ds(pl.multiple_of(j * tk, tk), tk), :]
        z = lax.dot_general(kb, qm, (((1,), (1,)), ((), ())), preferred_element_type=F32)
        sp = jnp.log1p(jnp.exp(-jnp.abs(z)))
        ls = jnp.minimum(z, 0.0) - sp
        lk = ls - z
        if diagonal:
            earlier = (j * tk - qi * tq + r_kq) < c_kq
            lk = jnp.where(earlier, lk, 0.0)
        hi = lk.astype(BF16)
        lo = (lk - hi.astype(F32)).astype(BF16)
        after = (jnp.dot(later, hi, preferred_element_type=F32)
                 + jnp.dot(later, lo, preferred_element_type=F32))
        c = c_ref[...]
        w = jnp.exp(ls + after + c)
        if diagonal:
            w = jnp.where(earlier, w, 0.0)
        vtb = vt_ref[0, :, pl.ds(pl.multiple_of(j * tk, tk), tk)]
        acc_ref[...] += jnp.dot(vtb, w.astype(BF16), preferred_element_type=F32)
        c_new = c + after[0:1, :] + lk[0:1, :]
        c_ref[...] = c_new
        return jnp.max(c_new)

    out_t = None
    for head in range(LANES // SB_HEAD_DIM):
        in_head = (lane >= head * SB_HEAD_DIM) & (lane < (head + 1) * SB_HEAD_DIM)
        qm = jnp.where(in_head, q, jnp.zeros_like(q))
        acc_ref[...] = jnp.zeros_like(acc_ref)
        c_ref[...] = jnp.zeros_like(c_ref)
        j_top = qi * blocks_per_tile + (blocks_per_tile - 1)
        c_max = jnp.float32(0.0)
        for d in range(blocks_per_tile):
            c_max = block(j_top - d, qm, True)

        def cond(carry):
            j, c_max = carry
            return (j >= 0) & (c_max > F32_EXP_UNDERFLOW)

        def body(carry, qm=qm):
            j, _ = carry
            return j - 1, block(j, qm, False)

        lax.while_loop(cond, body, (qi * blocks_per_tile - 1, c_max))
        acc = acc_ref[...]
        if out_t is None:
            out_t = acc
        else:
            out_t = jnp.where(row_o >= head * SB_HEAD_DIM, acc, out_t)
    o_ref[...] = out_t.T.astype(o_ref.dtype)


def _sb_attention(q, k, vt, batch, seq):
    n, d = q.shape
    tq = SB_Q_TILE
    nq = seq // tq
    return pl.pallas_call(
        _sb_attn_kernel,
        grid=(batch, d // LANES, nq),
        in_specs=[pl.BlockSpec((tq, LANES), lambda b, p, i: (b * nq + i, p)),
                  pl.BlockSpec((seq, LANES), lambda b, p, i: (b, p)),
                  pl.BlockSpec((1, LANES, seq), lambda b, p, i: (b, p, 0))],
        out_specs=pl.BlockSpec((tq, LANES), lambda b, p, i: (b * nq + i, p)),
        out_shape=jax.ShapeDtypeStruct((n, d), BF16),
        scratch_shapes=[pltpu.VMEM((LANES, tq), F32), pltpu.VMEM((1, tq), F32)],
        compiler_params=pltpu.CompilerParams(dimension_semantics=("arbitrary",) * 3,
                                             vmem_limit_bytes=VMEM_LIMIT_BYTES),
        name="sb_attention",
    )(q, k, vt)


def _diff_attn_kernel(q_ref, k_ref, vt_ref, lq1_ref, lk1_ref, lq2_ref, lk2_ref, sub_ref, o_ref,
                      m_ref, l_ref, acc_ref, *, lam_init):
    t = DIFF_TILE
    qi = pl.program_id(2)
    q = q_ref[...]
    lane = lax.broadcasted_iota(jnp.int32, q.shape, 1)
    qms = (jnp.where(lane < DIFF_HEAD_DIM, q, jnp.zeros_like(q)),
           jnp.where(lane >= DIFF_HEAD_DIM, q, jnp.zeros_like(q)))
    m_ref[...] = jnp.full_like(m_ref, NEG_INF)
    l_ref[...] = jnp.zeros_like(l_ref)
    acc_ref[...] = jnp.zeros_like(acc_ref)

    def block(j, diagonal):
        start = pl.multiple_of(j * t, t)
        kb = k_ref[pl.ds(start, t), :]
        vtb = vt_ref[0, :, pl.ds(start, t)]
        for m in range(2):
            s = lax.dot_general(kb, qms[m], (((1,), (1,)), ((), ())), preferred_element_type=F32)
            if diagonal:
                key_chunk = lax.broadcasted_iota(jnp.int32, (t, t), 0) // CHUNK
                q_chunk = lax.broadcasted_iota(jnp.int32, (t, t), 1) // CHUNK
                s = jnp.where(key_chunk <= q_chunk, s, NEG_INF)
            m_old = m_ref[m]
            m_new = jnp.maximum(m_old, jnp.max(s, axis=0, keepdims=True))
            p = jnp.exp(s - m_new)
            alpha = jnp.exp(m_old - m_new)
            l_ref[m] = alpha * l_ref[m] + jnp.sum(p, axis=0, keepdims=True)
            acc_ref[m] = alpha * acc_ref[m] + jnp.dot(vtb, p.astype(BF16), preferred_element_type=F32)
            m_ref[m] = m_new

    def body(j, carry):
        block(j, False)
        return carry

    lax.fori_loop(0, qi, body, 0)
    block(qi, True)

    lam = (jnp.exp(jnp.sum(lq1_ref[...] * lk1_ref[...], axis=-1, keepdims=True))
           - jnp.exp(jnp.sum(lq2_ref[...] * lk2_ref[...], axis=-1, keepdims=True)) + lam_init)
    o = acc_ref[0] / l_ref[0] - lam * (acc_ref[1] / l_ref[1])
    ms = jnp.mean(o * o, axis=0, keepdims=True)
    o = o * lax.rsqrt(ms + EPS)
    o = o.T * sub_ref[...] * (1.0 - lam_init)
    o_ref[...] = o.astype(o_ref.dtype)


def _diff_attention(q, k, vt, lq1, lk1, lq2, lk2, sub_g, batch, seq, lam_init):
    n, d = q.shape
    t = DIFF_TILE
    nq = seq // t
    small = _const_spec((1, DIFF_HEAD_DIM))
    return pl.pallas_call(
        functools.partial(_diff_attn_kernel, lam_init=lam_init),
        grid=(batch, d // LANES, nq),
        in_specs=[pl.BlockSpec((t, LANES), lambda b, h, i: (b * nq + i, h)),
                  pl.BlockSpec((seq, LANES), lambda b, h, i: (b, h)),
                  pl.BlockSpec((1, LANES, seq), lambda b, h, i: (b, h, 0)),
                  small, small, small, small, _const_spec((1, LANES))],
        out_specs=pl.BlockSpec((t, LANES), lambda b, h, i: (b * nq + i, h)),
        out_shape=jax.ShapeDtypeStruct((n, d), BF16),
        scratch_shapes=[pltpu.VMEM((2, 1, t), F32), pltpu.VMEM((2, 1, t), F32),
                        pltpu.VMEM((2, LANES, t), F32)],
        compiler_params=pltpu.CompilerParams(dimension_semantics=("arbitrary",) * 3,
                                             vmem_limit_bytes=VMEM_LIMIT_BYTES),
        name="diff_attention",
    )(q, k, vt, lq1, lk1, lq2, lk2, sub_g)


def _mix_ffn_kernel(x_ref, o_ref, wo_ref, g_ref, wg_ref, wu_ref, wd_ref, out_ref):
    x1 = x_ref[...] + jnp.dot(o_ref[...], wo_ref[...], preferred_element_type=F32)
    h = _rms(x1, g_ref[...]).astype(BF16)
    acc = x1
    for lo, hi in FFN_CHUNKS:
        g = jnp.dot(h, wg_ref[:, lo:hi], preferred_element_type=F32)
        u = jnp.dot(h, wu_ref[:, lo:hi], preferred_element_type=F32)
        a = (g * (1.0 / (1.0 + jnp.exp(-g))) * u).astype(BF16)
        acc = acc + jnp.dot(a, wd_ref[lo:hi, :], preferred_element_type=F32)
    out_ref[...] = acc


def _mix_ffn(x2, o, wo, gain, wg, wu, wd):
    n, d = x2.shape
    tm = TOKEN_TILE
    assert FFN_CHUNKS[-1][1] == wg.shape[1]
    row_spec = pl.BlockSpec((tm, d), lambda i: (i, 0))
    return pl.pallas_call(
        _mix_ffn_kernel,
        grid=(n // tm,),
        in_specs=[row_spec, row_spec, _const_spec(wo.shape), _const_spec((1, d)),
                  _const_spec(wg.shape), _const_spec(wu.shape), _const_spec(wd.shape)],
        out_specs=row_spec,
        out_shape=jax.ShapeDtypeStruct((n, d), F32),
        compiler_params=pltpu.CompilerParams(dimension_semantics=("arbitrary",),
                                             vmem_limit_bytes=VMEM_LIMIT_BYTES),
        name="mix_ffn",
    )(x2, o, wo, gain, wg, wu, wd)


def _rope_tables(positions):
    half = ROT_DIM // 2
    inv_freq = ROPE_THETA ** (-jnp.arange(0, ROT_DIM, 2, dtype=F32) / ROT_DIM)
    ang = positions.astype(F32).reshape(-1, 1) * inv_freq
    cos, sin = jnp.cos(ang), jnp.sin(ang)
    n = ang.shape[0]
    pad = DIFF_HEAD_DIM - ROT_DIM
    cos64 = jnp.concatenate([cos, cos, jnp.ones((n, pad), F32)], axis=-1)
    lo64 = jnp.concatenate([-sin, jnp.zeros((n, half + pad), F32)], axis=-1)
    hi64 = jnp.concatenate([jnp.zeros((n, half), F32), sin, jnp.zeros((n, pad), F32)], axis=-1)
    rep = LANES // DIFF_HEAD_DIM
    return tuple(jnp.tile(t, (1, rep)) for t in (cos64, lo64, hi64))


def kernel(x, positions, attn_norm, w_in, w_out, q_norm, k_norm, lambda_q1, lambda_k1, lambda_q2,
           lambda_k2, sub_norm, ffn_norm, w_gate, w_up, w_down):
    batch, seq, d = x.shape
    depth = w_in.shape[0]
    assert seq % TOKEN_TILE == 0 and seq % SB_Q_TILE == 0 and seq % DIFF_TILE == 0
    assert d % LANES == 0 and DIFF_V_DIM == LANES and DIFF_TILE % CHUNK == 0
    rep = LANES // DIFF_HEAD_DIM
    x2 = x.reshape(batch * seq, d)
    tables = _rope_tables(positions)
    for i in range(depth):
        wqk = w_in[i, :, :2 * d].astype(BF16)
        wvt = w_in[i, :, 2 * d:].T.astype(BF16)
        gain = attn_norm[i].reshape(1, d)
        if i % N_MIXERS == 0:
            q, k, vt = _qkv_proj(x2, gain, wqk, wvt, batch, seq, SB_HEAD_DIM ** -0.5)
            o = _sb_attention(q, k, vt, batch, seq)
        else:
            j = i // N_MIXERS
            qg = jnp.tile(q_norm[j].reshape(1, -1), (1, rep))
            kg = jnp.tile(k_norm[j].reshape(1, -1), (1, rep))
            q, k, vt = _qkv_proj(x2, gain, wqk, wvt, batch, seq, DIFF_HEAD_DIM ** -0.5,
                                 (qg, kg) + tables)
            lam_init = 0.8 - 0.6 * math.exp(-0.3 * i)
            o = _diff_attention(q, k, vt, lambda_q1[j].reshape(1, -1), lambda_k1[j].reshape(1, -1),
                                lambda_q2[j].reshape(1, -1), lambda_k2[j].reshape(1, -1),
                                sub_norm[j].reshape(1, -1), batch, seq, lam_init)
        x2 = _mix_ffn(x2, o, w_out[i].astype(BF16), ffn_norm[i].reshape(1, d),
                      w_gate[i].astype(BF16), w_up[i].astype(BF16), w_down[i].astype(BF16))
    return x2.reshape(batch, seq, d)
```

```python
import functools
import math

import jax
import jax.numpy as jnp
from jax import lax
from jax.experimental import pallas as pl
from jax.experimental.pallas import tpu as pltpu

F32 = jnp.float32
BF16 = jnp.bfloat16

N_MIXERS = 2
SB_HEAD_DIM = 64
DIFF_HEAD_DIM = 64
DIFF_V_DIM = 2 * DIFF_HEAD_DIM
CHUNK = 64
ROPE_THETA = 500000.0
ROT_DIM = DIFF_HEAD_DIM // 4
EPS = 1e-6
NEG_INF = -1e30

LANES = 128
VMEM_LIMIT_BYTES = 56 * 1024 * 1024

TOKEN_TILE = 512
SB_Q_TILE = 256
SB_K_TILE = 128
DIFF_TILE = 256
FFN_CHUNKS = ((0, 1024), (1024, 2048), (2048, 2816))

F32_EXP_UNDERFLOW = -104.0


def _rms(x, gain):
    ms = jnp.mean(x * x, axis=-1, keepdims=True)
    return x * lax.rsqrt(ms + EPS) * gain


def _const_spec(shape):
    nd = len(shape)
    return pl.BlockSpec(shape, lambda *_: (0,) * nd, pipeline_mode=pl.Buffered(1))


def _qkv_kernel(x_ref, g_ref, wqk_ref, wvt_ref, q_ref, k_ref, vt_ref, *, d_model, scale):
    h = _rms(x_ref[...], g_ref[...]).astype(BF16)
    q = jnp.dot(h, wqk_ref[:, :d_model], preferred_element_type=F32)
    q_ref[...] = (q * scale).astype(BF16)
    k_ref[...] = jnp.dot(h, wqk_ref[:, d_model:], preferred_element_type=F32).astype(BF16)
    vt = lax.dot_general(wvt_ref[...], h, (((1,), (1,)), ((), ())), preferred_element_type=F32)
    vt_ref[0] = vt.astype(BF16)


def _qk_norm_rope(t, gain, cos, sin_lo, sin_hi, lane_lo):
    sq = t * t
    lo = jnp.sum(jnp.where(lane_lo, sq, 0.0), axis=-1, keepdims=True)
    hi = jnp.sum(jnp.where(lane_lo, 0.0, sq), axis=-1, keepdims=True)
    inv = 1.0 / DIFF_HEAD_DIM
    r = jnp.where(lane_lo, lax.rsqrt(lo * inv + EPS), lax.rsqrt(hi * inv + EPS))
    y = t * r * gain
    half = ROT_DIM // 2
    return y * cos + pltpu.roll(y, LANES - half, 1) * sin_lo + pltpu.roll(y, half, 1) * sin_hi


def _qkv_diff_kernel(x_ref, g_ref, wk_ref, wqvt_ref, qg_ref, kg_ref, cos_ref, sl_ref, sh_ref,
                     cost_ref, sint_ref, qt_ref, k_ref, vt_ref, *, d_model, scale):
    h = _rms(x_ref[...], g_ref[...]).astype(BF16)
    tm = h.shape[0]
    cos, sin_lo, sin_hi = cos_ref[...], sl_ref[...], sh_ref[...]
    lane_lo = lax.broadcasted_iota(jnp.int32, cos.shape, 1) < DIFF_HEAD_DIM
    t = jnp.dot(h, wk_ref[...], preferred_element_type=F32)
    gain = kg_ref[...]
    for c in range(d_model // LANES):
        sl = slice(c * LANES, (c + 1) * LANES)
        k_ref[:, sl] = _qk_norm_rope(t[:, sl], gain, cos, sin_lo, sin_hi, lane_lo).astype(BF16)
    qvt = lax.dot_general(wqvt_ref[...], h, (((1,), (1,)), ((), ())), preferred_element_type=F32)
    vt_ref[0] = qvt[d_model:, :].astype(BF16)
    cos_t, sin_t = cost_ref[0], sint_ref[0]
    gq = jnp.tile(qg_ref[...], (1, tm // LANES))
    half = ROT_DIM // 2
    for g in range(d_model // DIFF_HEAD_DIM):
        rows = slice(g * DIFF_HEAD_DIM, (g + 1) * DIFF_HEAD_DIM)
        xg = qvt[rows, :]
        ms = jnp.mean(xg * xg, axis=0, keepdims=True)
        y = xg * lax.rsqrt(ms + EPS) * gq
        x1, x2 = y[:half], y[half:ROT_DIM]
        out = jnp.concatenate([x1 * cos_t - x2 * sin_t, x2 * cos_t + x1 * sin_t, y[ROT_DIM:]], axis=0)
        qt_ref[0, rows, :] = (out * scale).astype(BF16)


def _qkv_proj(x2, gain, w_in, batch, seq, scale, diff_args=None):
    n, d = x2.shape
    tm = TOKEN_TILE
    tiles_per_seq = seq // tm
    row_spec = pl.BlockSpec((tm, d), lambda i: (i, 0))
    t_spec = pl.BlockSpec((1, d, tm), lambda i: (i // tiles_per_seq, 0, i % tiles_per_seq))
    row_shape = jax.ShapeDtypeStruct((n, d), BF16)
    t_shape = jax.ShapeDtypeStruct((batch, d, seq), BF16)
    if diff_args is None:
        wqk = w_in[:, :2 * d].astype(BF16)
        wvt = w_in[:, 2 * d:].T.astype(BF16)
        body = functools.partial(_qkv_kernel, d_model=d, scale=scale)
        name = "qkv_sb"
        in_specs = [row_spec, _const_spec((1, d)), _const_spec(wqk.shape), _const_spec(wvt.shape)]
        args = [x2, gain, wqk, wvt]
        out_specs = [row_spec, row_spec, t_spec]
        out_shape = [row_shape, row_shape, t_shape]
    else:
        qg, kg, cos, sin_lo, sin_hi, cos_t, sin_t = diff_args
        wk = w_in[:, d:2 * d].astype(BF16)
        wqvt = jnp.concatenate([w_in[:, :d], w_in[:, 2 * d:]], axis=1).T.astype(BF16)
        tab_spec = pl.BlockSpec((tm, LANES), lambda i: (i, 0))
        tabt_spec = pl.BlockSpec((1, ROT_DIM // 2, tm),
                                 lambda i: (i // tiles_per_seq, 0, i % tiles_per_seq))
        body = functools.partial(_qkv_diff_kernel, d_model=d, scale=scale)
        name = "qkv_diff"
        in_specs = [row_spec, _const_spec((1, d)), _const_spec(wk.shape), _const_spec(wqvt.shape),
                    _const_spec(qg.shape), _const_spec(kg.shape), tab_spec, tab_spec, tab_spec,
                    tabt_spec, tabt_spec]
        args = [x2, gain, wk, wqvt, qg, kg, cos, sin_lo, sin_hi, cos_t, sin_t]
        out_specs = [t_spec, row_spec, t_spec]
        out_shape = [t_shape, row_shape, t_shape]
    return pl.pallas_call(
        body,
        grid=(n // tm,),
        in_specs=in_specs,
        out_specs=out_specs,
        out_shape=out_shape,
        compiler_params=pltpu.CompilerParams(dimension_semantics=("arbitrary",),
                                             vmem_limit_bytes=VMEM_LIMIT_BYTES),
        name=name,
    )(*args)


def _sb_attn_kernel(q_ref, k_ref, vt_ref, o_ref, acc_ref, c_ref):
    tq, tk = SB_Q_TILE, SB_K_TILE
    qi = pl.program_id(2)
    q = q_ref[...]
    lane = lax.broadcasted_iota(jnp.int32, q.shape, 1)
    r_kk = lax.broadcasted_iota(jnp.int32, (tk, tk), 0)
    c_kk = lax.broadcasted_iota(jnp.int32, (tk, tk), 1)
    later = jnp.where(c_kk > r_kk, 1.0, 0.0).astype(BF16)
    r_kq = lax.broadcasted_iota(jnp.int32, (tk, tq), 0)
    c_kq = lax.broadcasted_iota(jnp.int32, (tk, tq), 1)
    row_o = lax.broadcasted_iota(jnp.int32, (LANES, tq), 0)
    blocks_per_tile = tq // tk

    def block(j, qm, diagonal):
        kb = k_ref[pl.ds(pl.multiple_of(j * tk, tk), tk), :]
        z = lax.dot_general(kb, qm, (((1,), (1,)), ((), ())), preferred_element_type=F32)
        sp = jnp.log1p(jnp.exp(-jnp.abs(z)))
        ls = jnp.minimum(z, 0.0) - sp
        lk = ls - z
        if diagonal:
            earlier = (j * tk - qi * tq + r_kq) < c_kq
            lk = jnp.where(earlier, lk, 0.0)
        hi = lk.astype(BF16)
        lo = (lk - hi.astype(F32)).astype(BF16)
        after = (jnp.dot(later, hi, preferred_element_type=F32)
                 + jnp.dot(later, lo, preferred_element_type=F32))
        c = c_ref[...]
        w = jnp.exp(ls + after + c)
        if diagonal:
            w = jnp.where(earlier, w, 0.0)
        vtb = vt_ref[0, :, pl.ds(pl.multiple_of(j * tk, tk), tk)]
        acc_ref[...] += jnp.dot(vtb, w.astype(BF16), preferred_element_type=F32)
        c_new = c + after[0:1, :] + lk[0:1, :]
        c_ref[...] = c_new
        return jnp.max(c_new)

    out_t = None
    for head in range(LANES // SB_HEAD_DIM):
        in_head = (lane >= head * SB_HEAD_DIM) & (lane < (head + 1) * SB_HEAD_DIM)
        qm = jnp.where(in_head, q, jnp.zeros_like(q))
        acc_ref[...] = jnp.zeros_like(acc_ref)
        c_ref[...] = jnp.zeros_like(c_ref)
        j_top = qi * blocks_per_tile + (blocks_per_tile - 1)
        c_max = jnp.float32(0.0)
        for d in range(blocks_per_tile):
            c_max = block(j_top - d, qm, True)

        def cond(carry):
            j, c_max = carry
            return (j >= 0) & (c_max > F32_EXP_UNDERFLOW)

        def body(carry, qm=qm):
            j, _ = carry
            return j - 1, block(j, qm, False)

        lax.while_loop(cond, body, (qi * blocks_per_tile - 1, c_max))
        acc = acc_ref[...]
        if out_t is None:
            out_t = acc
        else:
            out_t = jnp.where(row_o >= head * SB_HEAD_DIM, acc, out_t)
    o_ref[...] = out_t.T.astype(o_ref.dtype)


def _sb_attention(q, k, vt, batch, seq):
    n, d = q.shape
    tq = SB_Q_TILE
    nq = seq // tq
    return pl.pallas_call(
        _sb_attn_kernel,
        grid=(batch, d // LANES, nq),
        in_specs=[pl.BlockSpec((tq, LANES), lambda b, p, i: (b * nq + i, p)),
                  pl.BlockSpec((seq, LANES), lambda b, p, i: (b, p)),
                  pl.BlockSpec((1, LANES, seq), lambda b, p, i: (b, p, 0))],
        out_specs=pl.BlockSpec((tq, LANES), lambda b, p, i: (b * nq + i, p)),
        out_shape=jax.ShapeDtypeStruct((n, d), BF16),
        scratch_shapes=[pltpu.VMEM((LANES, tq), F32), pltpu.VMEM((1, tq), F32)],
        compiler_params=pltpu.CompilerParams(dimension_semantics=("arbitrary",) * 3,
                                             vmem_limit_bytes=VMEM_LIMIT_BYTES),
        name="sb_attention",
    )(q, k, vt)


def _diff_attn_kernel(qt_ref, k_ref, vt_ref, lq1_ref, lk1_ref, lq2_ref, lk2_ref, sub_ref, o_ref,
                      m_ref, l_ref, alpha_ref, acc_ref, s_ref, p_ref, smax_ref, *, lam_init):
    t = DIFF_TILE
    qi = pl.program_id(2)
    qt = qt_ref[0]
    row = lax.broadcasted_iota(jnp.int32, qt.shape, 0)
    zero = jnp.zeros_like(qt)
    q2x = jnp.concatenate([jnp.where(row < DIFF_HEAD_DIM, qt, zero),
                           jnp.where(row >= DIFF_HEAD_DIM, qt, zero)], axis=1)

    m_ref[...] = jnp.full_like(m_ref, NEG_INF)
    l_ref[...] = jnp.zeros_like(l_ref)
    alpha_ref[...] = jnp.ones_like(alpha_ref)
    acc_ref[...] = jnp.zeros_like(acc_ref)
    p_ref[...] = jnp.zeros_like(p_ref)

    def scores(j):
        kb = k_ref[pl.ds(pl.multiple_of(j * t, t), t), :]
        return jnp.dot(kb, q2x, preferred_element_type=F32)

    def stage_scores(s):
        s_ref[...] = s
        smax_ref[...] = jnp.max(s.reshape(t // 8, 8, 2 * t), axis=0)

    def softmax(diagonal):
        s = s_ref[...]
        if diagonal:
            key_chunk = lax.broadcasted_iota(jnp.int32, s.shape, 0) // CHUNK
            q_chunk = (lax.broadcasted_iota(jnp.int32, s.shape, 1) % t) // CHUNK
            s = jnp.where(key_chunk <= q_chunk, s, NEG_INF)
            smax = jnp.max(s, axis=0, keepdims=True)
        else:
            smax = jnp.max(smax_ref[...], axis=0, keepdims=True)
        m_old = m_ref[...]
        m_new = jnp.maximum(m_old, smax)
        p = jnp.exp(s - m_new)
        alpha = jnp.exp(m_old - m_new)
        l_ref[...] = alpha * l_ref[...] + jnp.sum(p, axis=0, keepdims=True)
        p_ref[...] = p.astype(BF16)
        alpha_ref[...] = alpha
        m_ref[...] = m_new

    def pv(j):
        vtb = vt_ref[0, :, pl.ds(pl.multiple_of(j * t, t), t)]
        acc_ref[...] = alpha_ref[...] * acc_ref[...] + jnp.dot(vtb, p_ref[...],
                                                               preferred_element_type=F32)

    stage_scores(scores(0))

    def body(j, carry):
        s_next = scores(j + 1)
        pv(jnp.maximum(j - 1, 0))
        softmax(False)
        stage_scores(s_next)
        return carry

    lax.fori_loop(0, qi, body, 0)
    pv(jnp.maximum(qi - 1, 0))
    softmax(True)
    pv(qi)

    lam = (jnp.exp(jnp.sum(lq1_ref[...] * lk1_ref[...], axis=-1, keepdims=True))
           - jnp.exp(jnp.sum(lq2_ref[...] * lk2_ref[...], axis=-1, keepdims=True)) + lam_init)
    on = acc_ref[...] / l_ref[...]
    o = on[:, :t] - lam * on[:, t:]
    ms = jnp.mean(o * o, axis=0, keepdims=True)
    o = o * lax.rsqrt(ms + EPS)
    o = o.T * sub_ref[...] * (1.0 - lam_init)
    o_ref[...] = o.astype(o_ref.dtype)


def _diff_attention(qt, k, vt, lq1, lk1, lq2, lk2, sub_g, batch, seq, lam_init):
    n, d = k.shape
    t = DIFF_TILE
    nq = seq // t
    small = _const_spec((1, DIFF_HEAD_DIM))
    return pl.pallas_call(
        functools.partial(_diff_attn_kernel, lam_init=lam_init),
        grid=(batch, d // LANES, nq),
        in_specs=[pl.BlockSpec((1, LANES, t), lambda b, h, i: (b, h, i)),
                  pl.BlockSpec((seq, LANES), lambda b, h, i: (b, h)),
                  pl.BlockSpec((1, LANES, seq), lambda b, h, i: (b, h, 0)),
                  small, small, small, small, _const_spec((1, LANES))],
        out_specs=pl.BlockSpec((t, LANES), lambda b, h, i: (b * nq + i, h)),
        out_shape=jax.ShapeDtypeStruct((n, d), BF16),
        scratch_shapes=[pltpu.VMEM((1, 2 * t), F32), pltpu.VMEM((1, 2 * t), F32),
                        pltpu.VMEM((1, 2 * t), F32), pltpu.VMEM((LANES, 2 * t), F32),
                        pltpu.VMEM((t, 2 * t), F32), pltpu.VMEM((t, 2 * t), BF16),
                        pltpu.VMEM((8, 2 * t), F32)],
        compiler_params=pltpu.CompilerParams(dimension_semantics=("arbitrary",) * 3,
                                             vmem_limit_bytes=VMEM_LIMIT_BYTES),
        name="diff_attention",
    )(qt, k, vt, lq1, lk1, lq2, lk2, sub_g)


def _mix_ffn_kernel(x_ref, o_ref, wo_ref, g_ref, wg_ref, wu_ref, wd_ref, out_ref):
    x1 = x_ref[...] + jnp.dot(o_ref[...], wo_ref[...], preferred_element_type=F32)
    h = _rms(x1, g_ref[...]).astype(BF16)
    acc = x1
    for lo, hi in FFN_CHUNKS:
        g = jnp.dot(h, wg_ref[:, lo:hi], preferred_element_type=F32)
        u = jnp.dot(h, wu_ref[:, lo:hi], preferred_element_type=F32)
        a = (g * (1.0 / (1.0 + jnp.exp(-g))) * u).astype(BF16)
        acc = acc + jnp.dot(a, wd_ref[lo:hi, :], preferred_element_type=F32)
    out_ref[...] = acc


def _mix_ffn(x2, o, wo, gain, wg, wu, wd):
    n, d = x2.shape
    tm = TOKEN_TILE
    assert FFN_CHUNKS[-1][1] == wg.shape[1]
    row_spec = pl.BlockSpec((tm, d), lambda i: (i, 0))
    return pl.pallas_call(
        _mix_ffn_kernel,
        grid=(n // tm,),
        in_specs=[row_spec, row_spec, _const_spec(wo.shape), _const_spec((1, d)),
                  _const_spec(wg.shape), _const_spec(wu.shape), _const_spec(wd.shape)],
        out_specs=row_spec,
        out_shape=jax.ShapeDtypeStruct((n, d), F32),
        compiler_params=pltpu.CompilerParams(dimension_semantics=("arbitrary",),
                                             vmem_limit_bytes=VMEM_LIMIT_BYTES),
        name="mix_ffn",
    )(x2, o, wo, gain, wg, wu, wd)


def _rope_tables(positions):
    batch, seq = positions.shape
    half = ROT_DIM // 2
    inv_freq = ROPE_THETA ** (-jnp.arange(0, ROT_DIM, 2, dtype=F32) / ROT_DIM)
    ang = positions.astype(F32).reshape(-1, 1) * inv_freq
    cos, sin = jnp.cos(ang), jnp.sin(ang)
    n = ang.shape[0]
    pad = DIFF_HEAD_DIM - ROT_DIM
    cos64 = jnp.concatenate([cos, cos, jnp.ones((n, pad), F32)], axis=-1)
    lo64 = jnp.concatenate([-sin, jnp.zeros((n, half + pad), F32)], axis=-1)
    hi64 = jnp.concatenate([jnp.zeros((n, half), F32), sin, jnp.zeros((n, pad), F32)], axis=-1)
    rep = LANES // DIFF_HEAD_DIM
    token_major = tuple(jnp.tile(t, (1, rep)) for t in (cos64, lo64, hi64))
    freq_major = tuple(jnp.swapaxes(t.reshape(batch, seq, half), 1, 2) for t in (cos, sin))
    return token_major + freq_major


def kernel(x, positions, attn_norm, w_in, w_out, q_norm, k_norm, lambda_q1, lambda_k1, lambda_q2,
           lambda_k2, sub_norm, ffn_norm, w_gate, w_up, w_down):
    batch, seq, d = x.shape
    depth = w_in.shape[0]
    assert seq % TOKEN_TILE == 0 and seq % SB_Q_TILE == 0 and seq % DIFF_TILE == 0
    assert d % LANES == 0 and DIFF_V_DIM == LANES and DIFF_TILE % CHUNK == 0
    rep = LANES // DIFF_HEAD_DIM
    x2 = x.reshape(batch * seq, d)
    tables = _rope_tables(positions)
    for i in range(depth):
        gain = attn_norm[i].reshape(1, d)
        if i % N_MIXERS == 0:
            q, k, vt = _qkv_proj(x2, gain, w_in[i], batch, seq, SB_HEAD_DIM ** -0.5)
            o = _sb_attention(q, k, vt, batch, seq)
        else:
            j = i // N_MIXERS
            qg = jnp.broadcast_to(q_norm[j].reshape(-1, 1), (DIFF_HEAD_DIM, LANES))
            kg = jnp.tile(k_norm[j].reshape(1, -1), (1, rep))
            qt, k, vt = _qkv_proj(x2, gain, w_in[i], batch, seq, DIFF_HEAD_DIM ** -0.5,
                                  (qg, kg) + tables)
            lam_init = 0.8 - 0.6 * math.exp(-0.3 * i)
            o = _diff_attention(qt, k, vt, lambda_q1[j].reshape(1, -1), lambda_k1[j].reshape(1, -1),
                                lambda_q2[j].reshape(1, -1), lambda_k2[j].reshape(1, -1),
                                sub_norm[j].reshape(1, -1), batch, seq, lam_init)
        x2 = _mix_ffn(x2, o, w_out[i].astype(BF16), ffn_norm[i].reshape(1, d),
                      w_gate[i].astype(BF16), w_up[i].astype(BF16), w_down[i].astype(BF16))
    return x2.reshape(batch, seq, d)
```

```python
import functools
import math

import jax
import jax.numpy as jnp
from jax import lax
from jax.experimental import pallas as pl
from jax.experimental.pallas import tpu as pltpu

F32 = jnp.float32
BF16 = jnp.bfloat16

N_MIXERS = 2
SB_HEAD_DIM = 64
DIFF_HEAD_DIM = 64
DIFF_V_DIM = 2 * DIFF_HEAD_DIM
CHUNK = 64
ROPE_THETA = 500000.0
ROT_DIM = DIFF_HEAD_DIM // 4
EPS = 1e-6
NEG_INF = -1e30

LANES = 128
VMEM_LIMIT_BYTES = 56 * 1024 * 1024

TOKEN_TILE = 512
SB_Q_TILE = 256
SB_SUB = 128
SB_PRE_SUBS = 2
SB_NEXT_SUBS = 2
DIFF_TILE = 256
FFN_CHUNKS = ((0, 1024), (1024, 2048), (2048, 2816))

LOG2E = 1.4426950408889634
F32_EXP2_UNDERFLOW = -150.0


def _rms(x, gain):
    ms = jnp.mean(x * x, axis=-1, keepdims=True)
    return x * lax.rsqrt(ms + EPS) * gain


def _const_spec(shape):
    nd = len(shape)
    return pl.BlockSpec(shape, lambda *_: (0,) * nd, pipeline_mode=pl.Buffered(1))


def _qkv_kernel(x_ref, g_ref, wk_ref, wqvt_ref, qt_ref, k_ref, vt_ref, *, d_model, scale):
    h = _rms(x_ref[...], g_ref[...]).astype(BF16)
    k_ref[...] = jnp.dot(h, wk_ref[...], preferred_element_type=F32).astype(BF16)
    qvt = lax.dot_general(wqvt_ref[...], h, (((1,), (1,)), ((), ())), preferred_element_type=F32)
    qt_ref[0] = (qvt[:d_model, :] * scale).astype(BF16)
    vt_ref[0] = qvt[d_model:, :].astype(BF16)


def _qk_norm_rope(t, gain, cos, sin_lo, sin_hi, lane_lo):
    sq = t * t
    lo = jnp.sum(jnp.where(lane_lo, sq, 0.0), axis=-1, keepdims=True)
    hi = jnp.sum(jnp.where(lane_lo, 0.0, sq), axis=-1, keepdims=True)
    inv = 1.0 / DIFF_HEAD_DIM
    r = jnp.where(lane_lo, lax.rsqrt(lo * inv + EPS), lax.rsqrt(hi * inv + EPS))
    y = t * r * gain
    half = ROT_DIM // 2
    return y * cos + pltpu.roll(y, LANES - half, 1) * sin_lo + pltpu.roll(y, half, 1) * sin_hi


def _qkv_diff_kernel(x_ref, g_ref, wk_ref, wqvt_ref, qg_ref, kg_ref, cos_ref, sl_ref, sh_ref,
                     cost_ref, sint_ref, qt_ref, k_ref, vt_ref, *, d_model, scale):
    h = _rms(x_ref[...], g_ref[...]).astype(BF16)
    tm = h.shape[0]
    cos, sin_lo, sin_hi = cos_ref[...], sl_ref[...], sh_ref[...]
    lane_lo = lax.broadcasted_iota(jnp.int32, cos.shape, 1) < DIFF_HEAD_DIM
    t = jnp.dot(h, wk_ref[...], preferred_element_type=F32)
    gain = kg_ref[...]
    for c in range(d_model // LANES):
        sl = slice(c * LANES, (c + 1) * LANES)
        k_ref[:, sl] = _qk_norm_rope(t[:, sl], gain, cos, sin_lo, sin_hi, lane_lo).astype(BF16)
    qvt = lax.dot_general(wqvt_ref[...], h, (((1,), (1,)), ((), ())), preferred_element_type=F32)
    vt_ref[0] = qvt[d_model:, :].astype(BF16)
    cos_t, sin_t = cost_ref[0], sint_ref[0]
    gq = jnp.tile(qg_ref[...], (1, tm // LANES))
    half = ROT_DIM // 2
    for g in range(d_model // DIFF_HEAD_DIM):
        rows = slice(g * DIFF_HEAD_DIM, (g + 1) * DIFF_HEAD_DIM)
        xg = qvt[rows, :]
        ms = jnp.mean(xg * xg, axis=0, keepdims=True)
        y = xg * lax.rsqrt(ms + EPS) * gq
        x1, x2 = y[:half], y[half:ROT_DIM]
        out = jnp.concatenate([x1 * cos_t - x2 * sin_t, x2 * cos_t + x1 * sin_t, y[ROT_DIM:]], axis=0)
        qt_ref[0, rows, :] = (out * scale).astype(BF16)


def _qkv_proj(x2, gain, w_in, batch, seq, scale, diff_args=None):
    n, d = x2.shape
    tm = TOKEN_TILE
    tiles_per_seq = seq // tm
    row_spec = pl.BlockSpec((tm, d), lambda i: (i, 0))
    t_spec = pl.BlockSpec((1, d, tm), lambda i: (i // tiles_per_seq, 0, i % tiles_per_seq))
    row_shape = jax.ShapeDtypeStruct((n, d), BF16)
    t_shape = jax.ShapeDtypeStruct((batch, d, seq), BF16)
    wk = w_in[:, d:2 * d].astype(BF16)
    wqvt = jnp.concatenate([w_in[:, :d], w_in[:, 2 * d:]], axis=1).T.astype(BF16)
    if diff_args is None:
        body = functools.partial(_qkv_kernel, d_model=d, scale=scale)
        name = "qkv_sb"
        in_specs = [row_spec, _const_spec((1, d)), _const_spec(wk.shape), _const_spec(wqvt.shape)]
        args = [x2, gain, wk, wqvt]
    else:
        qg, kg, cos, sin_lo, sin_hi, cos_t, sin_t = diff_args
        tab_spec = pl.BlockSpec((tm, LANES), lambda i: (i, 0))
        tabt_spec = pl.BlockSpec((1, ROT_DIM // 2, tm),
                                 lambda i: (i // tiles_per_seq, 0, i % tiles_per_seq))
        body = functools.partial(_qkv_diff_kernel, d_model=d, scale=scale)
        name = "qkv_diff"
        in_specs = [row_spec, _const_spec((1, d)), _const_spec(wk.shape), _const_spec(wqvt.shape),
                    _const_spec(qg.shape), _const_spec(kg.shape), tab_spec, tab_spec, tab_spec,
                    tabt_spec, tabt_spec]
        args = [x2, gain, wk, wqvt, qg, kg, cos, sin_lo, sin_hi, cos_t, sin_t]
    return pl.pallas_call(
        body,
        grid=(n // tm,),
        in_specs=in_specs,
        out_specs=[t_spec, row_spec, t_spec],
        out_shape=[t_shape, row_shape, t_shape],
        compiler_params=pltpu.CompilerParams(dimension_semantics=("arbitrary",),
                                             vmem_limit_bytes=VMEM_LIMIT_BYTES),
        name=name,
    )(*args)


def _sb_attn_kernel(qt_ref, k_ref, vt_ref, o_ref, z_ref, w_ref, c_ref, accx_ref):
    tq, sub = SB_Q_TILE, SB_SUB
    half = tq // 2
    gw = 2 * half
    n_tiles = k_ref.shape[0] // tq
    n_sub = SB_PRE_SUBS + tq // sub
    r_u = lax.broadcasted_iota(jnp.int32, (sub, 2 * sub), 0)
    c_u = lax.broadcasted_iota(jnp.int32, (sub, 2 * sub), 1) % sub
    later2 = jnp.where(c_u > r_u, 1.0, 0.0).astype(BF16)
    tri = (lax.broadcasted_iota(jnp.int32, (sub, gw), 0)
           < lax.broadcasted_iota(jnp.int32, (sub, gw), 1) % half)

    def q4x(t):
        qt = qt_ref[0, :, pl.ds(pl.multiple_of(t * tq, tq), tq)]
        row = lax.broadcasted_iota(jnp.int32, (LANES, half), 0)
        zero = jnp.zeros((LANES, half), qt.dtype)
        parts = []
        for g in range(2):
            qg = qt[:, g * half:(g + 1) * half]
            parts += [jnp.where(row < SB_HEAD_DIM, qg, zero), jnp.where(row >= SB_HEAD_DIM, qg, zero)]
        return jnp.concatenate(parts, axis=1)

    def first_block(t):
        return jnp.maximum(t * (tq // sub) - SB_PRE_SUBS, 0)

    def first_key(t):
        return pl.multiple_of(first_block(t) * sub, sub)

    def scores(t):
        kb = k_ref[pl.ds(first_key(t), n_sub * sub), :]
        return jnp.dot(kb, q4x(t), preferred_element_type=F32)

    def sub_block(z, c, mask):
        nabs = pltpu.bitcast(pltpu.bitcast(z, jnp.uint32) | jnp.uint32(0x80000000), F32)
        sp = jnp.log(1.0 + jnp.exp2(nabs)) * LOG2E
        ls = jnp.minimum(z, 0.0) - sp
        lk = ls - z
        if mask is not None:
            lk = jnp.where(mask, lk, 0.0)
        hi = lk.astype(BF16)
        lo = (lk - hi.astype(F32)).astype(BF16)
        after = jnp.dot(later2, jnp.concatenate([hi, lo], axis=0), preferred_element_type=F32)
        w = jnp.exp2(ls + after + c)
        if mask is not None:
            w = jnp.where(mask, w, 0.0)
        return w.astype(BF16), c + after[0:1, :] + lk[0:1, :]

    def first_step(first_sb):
        zero_c = jnp.zeros((1, gw), F32)
        lo_rows = slice(first_sb * sub, (first_sb + 1) * sub)
        hi_rows = slice((first_sb + 1) * sub, (first_sb + 2) * sub)
        wb, cb = sub_block(z_ref[hi_rows, gw:], zero_c, tri)
        w_ref[hi_rows, gw:] = wb
        wa, ca = sub_block(z_ref[lo_rows, :gw], zero_c, tri)
        wb, cb = sub_block(z_ref[lo_rows, gw:], cb, None)
        w_ref[lo_rows, :gw] = wa
        w_ref[lo_rows, gw:] = wb
        c = jnp.concatenate([ca, cb], axis=1)
        for sb in reversed(range(first_sb)):
            w, c = sub_block(z_ref[sb * sub:(sb + 1) * sub, :], c, None)
            w_ref[sb * sub:(sb + 1) * sub, :] = w
        c_ref[...] = c
        return jnp.max(c)

    def extra_steps(t, c_max):
        def cond(carry):
            j, c_max = carry
            return (j > 0) & (c_max > F32_EXP2_UNDERFLOW)

        def body(carry):
            j, _ = carry
            j = j - SB_NEXT_SUBS
            start = pl.multiple_of(j * sub, sub)
            kb = k_ref[pl.ds(start, SB_NEXT_SUBS * sub), :]
            z_all = jnp.dot(kb, q4x(t), preferred_element_type=F32)
            c = c_ref[...]
            ws = [None] * SB_NEXT_SUBS
            for sb in reversed(range(SB_NEXT_SUBS)):
                ws[sb], c = sub_block(z_all[sb * sub:(sb + 1) * sub], c, None)
            vtb = vt_ref[0, :, pl.ds(start, SB_NEXT_SUBS * sub)]
            accx_ref[...] += jnp.dot(vtb, jnp.concatenate(ws, axis=0), preferred_element_type=F32)
            c_ref[...] = c
            return j, jnp.max(c)

        lax.while_loop(cond, body, (first_block(t), c_max))

    def finish(t):
        vtb = vt_ref[0, :, pl.ds(first_key(t), n_sub * sub)]
        acc = jnp.dot(vtb, w_ref[...], preferred_element_type=F32) + accx_ref[...]
        row_o = lax.broadcasted_iota(jnp.int32, (LANES, half), 0)
        out_t = jnp.concatenate(
            [jnp.where(row_o < SB_HEAD_DIM, acc[:, g * gw:g * gw + half],
                       acc[:, g * gw + half:(g + 1) * gw]) for g in range(2)], axis=1)
        o_ref[pl.ds(pl.multiple_of(t * tq, tq), tq), :] = out_t.T.astype(o_ref.dtype)

    w_ref[...] = jnp.zeros_like(w_ref)
    accx_ref[...] = jnp.zeros_like(accx_ref)
    z_ref[...] = scores(0)
    first_step(0)
    z_ref[...] = scores(1)

    def body(t, carry):
        finish(t - 1)
        z_next = scores(jnp.minimum(t + 1, n_tiles - 1))
        accx_ref[...] = jnp.zeros_like(accx_ref)
        c_max = first_step(SB_PRE_SUBS)
        z_ref[...] = z_next
        extra_steps(t, c_max)
        return carry

    lax.fori_loop(1, n_tiles, body, 0)
    finish(n_tiles - 1)


def _sb_attention(qt, k, vt, batch, seq):
    n, d = k.shape
    tq = SB_Q_TILE
    rows = SB_PRE_SUBS * SB_SUB + tq
    assert seq >= rows and seq // tq >= 2
    return pl.pallas_call(
        _sb_attn_kernel,
        grid=(batch, d // LANES),
        in_specs=[pl.BlockSpec((1, LANES, seq), lambda b, p: (b, p, 0)),
                  pl.BlockSpec((seq, LANES), lambda b, p: (b, p)),
                  pl.BlockSpec((1, LANES, seq), lambda b, p: (b, p, 0))],
        out_specs=pl.BlockSpec((seq, LANES), lambda b, p: (b, p)),
        out_shape=jax.ShapeDtypeStruct((n, d), BF16),
        scratch_shapes=[pltpu.VMEM((rows, 2 * tq), F32), pltpu.VMEM((rows, 2 * tq), BF16),
                        pltpu.VMEM((1, 2 * tq), F32), pltpu.VMEM((LANES, 2 * tq), F32)],
        compiler_params=pltpu.CompilerParams(dimension_semantics=("arbitrary",) * 2,
                                             vmem_limit_bytes=VMEM_LIMIT_BYTES),
        name="sb_attention",
    )(qt, k, vt)


def _diff_attn_kernel(qt_ref, k_ref, vt_ref, lq1_ref, lk1_ref, lq2_ref, lk2_ref, sub_ref, o_ref,
                      m_ref, l_ref, alpha_ref, acc_ref, s_ref, p_ref, smax_ref, *, lam_init):
    t = DIFF_TILE
    qi = pl.program_id(2)
    qt = qt_ref[0]
    row = lax.broadcasted_iota(jnp.int32, qt.shape, 0)
    zero = jnp.zeros_like(qt)
    q2x = jnp.concatenate([jnp.where(row < DIFF_HEAD_DIM, qt, zero),
                           jnp.where(row >= DIFF_HEAD_DIM, qt, zero)], axis=1)

    m_ref[...] = jnp.full_like(m_ref, NEG_INF)
    l_ref[...] = jnp.zeros_like(l_ref)
    alpha_ref[...] = jnp.ones_like(alpha_ref)
    acc_ref[...] = jnp.zeros_like(acc_ref)
    p_ref[...] = jnp.zeros_like(p_ref)

    def scores(j):
        kb = k_ref[pl.ds(pl.multiple_of(j * t, t), t), :]
        return jnp.dot(kb, q2x, preferred_element_type=F32)

    def stage_scores(s):
        s_ref[...] = s
        smax_ref[...] = jnp.max(s.reshape(t // 8, 8, 2 * t), axis=0)

    def softmax(diagonal):
        s = s_ref[...]
        if diagonal:
            key_chunk = lax.broadcasted_iota(jnp.int32, s.shape, 0) // CHUNK
            q_chunk = (lax.broadcasted_iota(jnp.int32, s.shape, 1) % t) // CHUNK
            s = jnp.where(key_chunk <= q_chunk, s, NEG_INF)
            smax = jnp.max(s, axis=0, keepdims=True)
        else:
            smax = jnp.max(smax_ref[...], axis=0, keepdims=True)
        m_old = m_ref[...]
        m_new = jnp.maximum(m_old, smax)
        p = jnp.exp(s - m_new)
        alpha = jnp.exp(m_old - m_new)
        l_ref[...] = alpha * l_ref[...] + jnp.sum(p, axis=0, keepdims=True)
        p_ref[...] = p.astype(BF16)
        alpha_ref[...] = alpha
        m_ref[...] = m_new

    def pv(j):
        vtb = vt_ref[0, :, pl.ds(pl.multiple_of(j * t, t), t)]
        acc_ref[...] = alpha_ref[...] * acc_ref[...] + jnp.dot(vtb, p_ref[...],
                                                               preferred_element_type=F32)

    stage_scores(scores(0))

    def body(j, carry):
        s_next = scores(j + 1)
        pv(jnp.maximum(j - 1, 0))
        softmax(False)
        stage_scores(s_next)
        return carry

    lax.fori_loop(0, qi, body, 0)
    pv(jnp.maximum(qi - 1, 0))
    softmax(True)
    pv(qi)

    lam = (jnp.exp(jnp.sum(lq1_ref[...] * lk1_ref[...], axis=-1, keepdims=True))
           - jnp.exp(jnp.sum(lq2_ref[...] * lk2_ref[...], axis=-1, keepdims=True)) + lam_init)
    on = acc_ref[...] / l_ref[...]
    o = on[:, :t] - lam * on[:, t:]
    ms = jnp.mean(o * o, axis=0, keepdims=True)
    o = o * lax.rsqrt(ms + EPS)
    o = o.T * sub_ref[...] * (1.0 - lam_init)
    o_ref[...] = o.astype(o_ref.dtype)


def _diff_attention(qt, k, vt, lq1, lk1, lq2, lk2, sub_g, batch, seq, lam_init):
    n, d = k.shape
    t = DIFF_TILE
    nq = seq // t
    small = _const_spec((1, DIFF_HEAD_DIM))
    return pl.pallas_call(
        functools.partial(_diff_attn_kernel, lam_init=lam_init),
        grid=(batch, d // LANES, nq),
        in_specs=[pl.BlockSpec((1, LANES, t), lambda b, h, i: (b, h, i)),
                  pl.BlockSpec((seq, LANES), lambda b, h, i: (b, h)),
                  pl.BlockSpec((1, LANES, seq), lambda b, h, i: (b, h, 0)),
                  small, small, small, small, _const_spec((1, LANES))],
        out_specs=pl.BlockSpec((t, LANES), lambda b, h, i: (b * nq + i, h)),
        out_shape=jax.ShapeDtypeStruct((n, d), BF16),
        scratch_shapes=[pltpu.VMEM((1, 2 * t), F32), pltpu.VMEM((1, 2 * t), F32),
                        pltpu.VMEM((1, 2 * t), F32), pltpu.VMEM((LANES, 2 * t), F32),
                        pltpu.VMEM((t, 2 * t), F32), pltpu.VMEM((t, 2 * t), BF16),
                        pltpu.VMEM((8, 2 * t), F32)],
        compiler_params=pltpu.CompilerParams(dimension_semantics=("arbitrary",) * 3,
                                             vmem_limit_bytes=VMEM_LIMIT_BYTES),
        name="diff_attention",
    )(qt, k, vt, lq1, lk1, lq2, lk2, sub_g)


def _mix_ffn_kernel(x_ref, o_ref, wo_ref, g_ref, wg_ref, wu_ref, wd_ref, out_ref):
    x1 = x_ref[...] + jnp.dot(o_ref[...], wo_ref[...], preferred_element_type=F32)
    h = _rms(x1, g_ref[...]).astype(BF16)
    acc = x1
    for lo, hi in FFN_CHUNKS:
        g = jnp.dot(h, wg_ref[:, lo:hi], preferred_element_type=F32)
        u = jnp.dot(h, wu_ref[:, lo:hi], preferred_element_type=F32)
        a = (g * (1.0 / (1.0 + jnp.exp(-g))) * u).astype(BF16)
        acc = acc + jnp.dot(a, wd_ref[lo:hi, :], preferred_element_type=F32)
    out_ref[...] = acc


def _mix_ffn(x2, o, wo, gain, wg, wu, wd):
    n, d = x2.shape
    tm = TOKEN_TILE
    assert FFN_CHUNKS[-1][1] == wg.shape[1]
    row_spec = pl.BlockSpec((tm, d), lambda i: (i, 0))
    return pl.pallas_call(
        _mix_ffn_kernel,
        grid=(n // tm,),
        in_specs=[row_spec, row_spec, _const_spec(wo.shape), _const_spec((1, d)),
                  _const_spec(wg.shape), _const_spec(wu.shape), _const_spec(wd.shape)],
        out_specs=row_spec,
        out_shape=jax.ShapeDtypeStruct((n, d), F32),
        compiler_params=pltpu.CompilerParams(dimension_semantics=("arbitrary",),
                                             vmem_limit_bytes=VMEM_LIMIT_BYTES),
        name="mix_ffn",
    )(x2, o, wo, gain, wg, wu, wd)


def _rope_tables(positions):
    batch, seq = positions.shape
    half = ROT_DIM // 2
    inv_freq = ROPE_THETA ** (-jnp.arange(0, ROT_DIM, 2, dtype=F32) / ROT_DIM)
    ang = positions.astype(F32).reshape(-1, 1) * inv_freq
    cos, sin = jnp.cos(ang), jnp.sin(ang)
    n = ang.shape[0]
    pad = DIFF_HEAD_DIM - ROT_DIM
    cos64 = jnp.concatenate([cos, cos, jnp.ones((n, pad), F32)], axis=-1)
    lo64 = jnp.concatenate([-sin, jnp.zeros((n, half + pad), F32)], axis=-1)
    hi64 = jnp.concatenate([jnp.zeros((n, half), F32), sin, jnp.zeros((n, pad), F32)], axis=-1)
    rep = LANES // DIFF_HEAD_DIM
    token_major = tuple(jnp.tile(t, (1, rep)) for t in (cos64, lo64, hi64))
    freq_major = tuple(jnp.swapaxes(t.reshape(batch, seq, half), 1, 2) for t in (cos, sin))
    return token_major + freq_major


def kernel(x, positions, attn_norm, w_in, w_out, q_norm, k_norm, lambda_q1, lambda_k1, lambda_q2,
           lambda_k2, sub_norm, ffn_norm, w_gate, w_up, w_down):
    batch, seq, d = x.shape
    depth = w_in.shape[0]
    assert seq % TOKEN_TILE == 0 and seq % SB_Q_TILE == 0 and seq % DIFF_TILE == 0
    assert d % LANES == 0 and DIFF_V_DIM == LANES and DIFF_TILE % CHUNK == 0
    rep = LANES // DIFF_HEAD_DIM
    x2 = x.reshape(batch * seq, d)
    tables = _rope_tables(positions)
    for i in range(depth):
        gain = attn_norm[i].reshape(1, d)
        if i % N_MIXERS == 0:
            qt, k, vt = _qkv_proj(x2, gain, w_in[i], batch, seq, SB_HEAD_DIM ** -0.5 * LOG2E)
            o = _sb_attention(qt, k, vt, batch, seq)
        else:
            j = i // N_MIXERS
            qg = jnp.broadcast_to(q_norm[j].reshape(-1, 1), (DIFF_HEAD_DIM, LANES))
            kg = jnp.tile(k_norm[j].reshape(1, -1), (1, rep))
            qt, k, vt = _qkv_proj(x2, gain, w_in[i], batch, seq, DIFF_HEAD_DIM ** -0.5,
                                  (qg, kg) + tables)
            lam_init = 0.8 - 0.6 * math.exp(-0.3 * i)
            o = _diff_attention(qt, k, vt, lambda_q1[j].reshape(1, -1), lambda_k1[j].reshape(1, -1),
                                lambda_q2[j].reshape(1, -1), lambda_k2[j].reshape(1, -1),
                                sub_norm[j].reshape(1, -1), batch, seq, lam_init)
        x2 = _mix_ffn(x2, o, w_out[i].astype(BF16), ffn_norm[i].reshape(1, d),
                      w_gate[i].astype(BF16), w_up[i].astype(BF16), w_down[i].astype(BF16))
    return x2.reshape(batch, seq, d)
```

```python
import functools
import math

import jax
import jax.numpy as jnp
from jax import lax
from jax.experimental import pallas as pl
from jax.experimental.pallas import tpu as pltpu

F32 = jnp.float32
BF16 = jnp.bfloat16

N_MIXERS = 2
SB_HEAD_DIM = 64
DIFF_HEAD_DIM = 64
DIFF_V_DIM = 2 * DIFF_HEAD_DIM
CHUNK = 64
ROPE_THETA = 500000.0
ROT_DIM = DIFF_HEAD_DIM // 4
EPS = 1e-6
NEG_INF = -1e30

LANES = 128
VMEM_LIMIT_BYTES = 56 * 1024 * 1024

TOKEN_TILE = 512
SB_Q_TILE = 256
SB_SUB = 128
SB_PRE_SUBS = 2
SB_NEXT_SUBS = 2
DIFF_TILE = 512
FFN_CHUNKS = ((0, 1024), (1024, 2048), (2048, 2816))

LOG2E = 1.4426950408889634
F32_EXP2_UNDERFLOW = -150.0


def _rms(x, gain):
    ms = jnp.mean(x * x, axis=-1, keepdims=True)
    return x * lax.rsqrt(ms + EPS) * gain


def _const_spec(shape):
    nd = len(shape)
    return pl.BlockSpec(shape, lambda *_: (0,) * nd, pipeline_mode=pl.Buffered(1))


def _qkv_kernel(x_ref, g_ref, wk_ref, wqvt_ref, qt_ref, k_ref, vt_ref, *, d_model, scale):
    h = _rms(x_ref[...], g_ref[...]).astype(BF16)
    k_ref[...] = jnp.dot(h, wk_ref[...], preferred_element_type=F32).astype(BF16)
    qvt = lax.dot_general(wqvt_ref[...], h, (((1,), (1,)), ((), ())), preferred_element_type=F32)
    qt_ref[0] = (qvt[:d_model, :] * scale).astype(BF16)
    vt_ref[0] = qvt[d_model:, :].astype(BF16)


def _qk_norm_rope(t, gain, cos, sin_lo, sin_hi, lane_lo):
    sq = t * t
    lo = jnp.sum(jnp.where(lane_lo, sq, 0.0), axis=-1, keepdims=True)
    hi = jnp.sum(jnp.where(lane_lo, 0.0, sq), axis=-1, keepdims=True)
    inv = 1.0 / DIFF_HEAD_DIM
    r = jnp.where(lane_lo, lax.rsqrt(lo * inv + EPS), lax.rsqrt(hi * inv + EPS))
    y = t * r * gain
    half = ROT_DIM // 2
    return y * cos + pltpu.roll(y, LANES - half, 1) * sin_lo + pltpu.roll(y, half, 1) * sin_hi


def _qkv_diff_kernel(x_ref, g_ref, wk_ref, wqvt_ref, qg_ref, kg_ref, cos_ref, sl_ref, sh_ref,
                     cost_ref, sint_ref, qt_ref, k_ref, vt_ref, *, d_model, scale):
    h = _rms(x_ref[...], g_ref[...]).astype(BF16)
    tm = h.shape[0]
    cos, sin_lo, sin_hi = cos_ref[...], sl_ref[...], sh_ref[...]
    lane_lo = lax.broadcasted_iota(jnp.int32, cos.shape, 1) < DIFF_HEAD_DIM
    t = jnp.dot(h, wk_ref[...], preferred_element_type=F32)
    gain = kg_ref[...]
    for c in range(d_model // LANES):
        sl = slice(c * LANES, (c + 1) * LANES)
        k_ref[:, sl] = _qk_norm_rope(t[:, sl], gain, cos, sin_lo, sin_hi, lane_lo).astype(BF16)
    qvt = lax.dot_general(wqvt_ref[...], h, (((1,), (1,)), ((), ())), preferred_element_type=F32)
    vt_ref[0] = qvt[d_model:, :].astype(BF16)
    cos_t, sin_t = cost_ref[0], sint_ref[0]
    gq = jnp.tile(qg_ref[...], (1, tm // LANES))
    half = ROT_DIM // 2
    for g in range(d_model // DIFF_HEAD_DIM):
        rows = slice(g * DIFF_HEAD_DIM, (g + 1) * DIFF_HEAD_DIM)
        xg = qvt[rows, :]
        ms = jnp.mean(xg * xg, axis=0, keepdims=True)
        y = xg * lax.rsqrt(ms + EPS) * gq
        x1, x2 = y[:half], y[half:ROT_DIM]
        out = jnp.concatenate([x1 * cos_t - x2 * sin_t, x2 * cos_t + x1 * sin_t, y[ROT_DIM:]], axis=0)
        qt_ref[0, rows, :] = (out * scale).astype(BF16)


def _qkv_proj(x2, gain, w_in, batch, seq, scale, diff_args=None):
    n, d = x2.shape
    tm = TOKEN_TILE
    tiles_per_seq = seq // tm
    row_spec = pl.BlockSpec((tm, d), lambda i: (i, 0))
    t_spec = pl.BlockSpec((1, d, tm), lambda i: (i // tiles_per_seq, 0, i % tiles_per_seq))
    row_shape = jax.ShapeDtypeStruct((n, d), BF16)
    t_shape = jax.ShapeDtypeStruct((batch, d, seq), BF16)
    wk = w_in[:, d:2 * d].astype(BF16)
    wqvt = jnp.concatenate([w_in[:, :d], w_in[:, 2 * d:]], axis=1).T.astype(BF16)
    if diff_args is None:
        body = functools.partial(_qkv_kernel, d_model=d, scale=scale)
        name = "qkv_sb"
        in_specs = [row_spec, _const_spec((1, d)), _const_spec(wk.shape), _const_spec(wqvt.shape)]
        args = [x2, gain, wk, wqvt]
    else:
        qg, kg, cos, sin_lo, sin_hi, cos_t, sin_t = diff_args
        tab_spec = pl.BlockSpec((tm, LANES), lambda i: (i, 0))
        tabt_spec = pl.BlockSpec((1, ROT_DIM // 2, tm),
                                 lambda i: (i // tiles_per_seq, 0, i % tiles_per_seq))
        body = functools.partial(_qkv_diff_kernel, d_model=d, scale=scale)
        name = "qkv_diff"
        in_specs = [row_spec, _const_spec((1, d)), _const_spec(wk.shape), _const_spec(wqvt.shape),
                    _const_spec(qg.shape), _const_spec(kg.shape), tab_spec, tab_spec, tab_spec,
                    tabt_spec, tabt_spec]
        args = [x2, gain, wk, wqvt, qg, kg, cos, sin_lo, sin_hi, cos_t, sin_t]
    return pl.pallas_call(
        body,
        grid=(n // tm,),
        in_specs=in_specs,
        out_specs=[t_spec, row_spec, t_spec],
        out_shape=[t_shape, row_shape, t_shape],
        compiler_params=pltpu.CompilerParams(dimension_semantics=("arbitrary",),
                                             vmem_limit_bytes=VMEM_LIMIT_BYTES),
        name=name,
    )(*args)


def _sb_attn_kernel(qt_ref, k_ref, vt_ref, o_ref, z_ref, w_ref, c_ref, accx_ref):
    tq, sub = SB_Q_TILE, SB_SUB
    half = tq // 2
    gw = 2 * half
    n_tiles = k_ref.shape[0] // tq
    n_sub = SB_PRE_SUBS + tq // sub
    r_u = lax.broadcasted_iota(jnp.int32, (sub, 2 * sub), 0)
    c_u = lax.broadcasted_iota(jnp.int32, (sub, 2 * sub), 1) % sub
    later2 = jnp.where(c_u > r_u, 1.0, 0.0).astype(BF16)
    tri = (lax.broadcasted_iota(jnp.int32, (sub, gw), 0)
           < lax.broadcasted_iota(jnp.int32, (sub, gw), 1) % half)

    def q4x(t):
        qt = qt_ref[0, :, pl.ds(pl.multiple_of(t * tq, tq), tq)]
        row = lax.broadcasted_iota(jnp.int32, (LANES, half), 0)
        zero = jnp.zeros((LANES, half), qt.dtype)
        parts = []
        for g in range(2):
            qg = qt[:, g * half:(g + 1) * half]
            parts += [jnp.where(row < SB_HEAD_DIM, qg, zero), jnp.where(row >= SB_HEAD_DIM, qg, zero)]
        return jnp.concatenate(parts, axis=1)

    def first_block(t):
        return jnp.maximum(t * (tq // sub) - SB_PRE_SUBS, 0)

    def first_key(t):
        return pl.multiple_of(first_block(t) * sub, sub)

    def scores(t):
        kb = k_ref[pl.ds(first_key(t), n_sub * sub), :]
        return jnp.dot(kb, q4x(t), preferred_element_type=F32)

    def sub_block(z, c, mask):
        nabs = pltpu.bitcast(pltpu.bitcast(z, jnp.uint32) | jnp.uint32(0x80000000), F32)
        sp = jnp.log(1.0 + jnp.exp2(nabs)) * LOG2E
        ls = jnp.minimum(z, 0.0) - sp
        lk = ls - z
        if mask is not None:
            lk = jnp.where(mask, lk, 0.0)
        hi = lk.astype(BF16)
        lo = (lk - hi.astype(F32)).astype(BF16)
        after = jnp.dot(later2, jnp.concatenate([hi, lo], axis=0), preferred_element_type=F32)
        w = jnp.exp2(ls + after + c)
        if mask is not None:
            w = jnp.where(mask, w, 0.0)
        return w.astype(BF16), c + after[0:1, :] + lk[0:1, :]

    def first_step(first_sb):
        zero_c = jnp.zeros((1, gw), F32)
        lo_rows = slice(first_sb * sub, (first_sb + 1) * sub)
        hi_rows = slice((first_sb + 1) * sub, (first_sb + 2) * sub)
        wb, cb = sub_block(z_ref[hi_rows, gw:], zero_c, tri)
        w_ref[hi_rows, gw:] = wb
        wa, ca = sub_block(z_ref[lo_rows, :gw], zero_c, tri)
        wb, cb = sub_block(z_ref[lo_rows, gw:], cb, None)
        w_ref[lo_rows, :gw] = wa
        w_ref[lo_rows, gw:] = wb
        c = jnp.concatenate([ca, cb], axis=1)
        for sb in reversed(range(first_sb)):
            w, c = sub_block(z_ref[sb * sub:(sb + 1) * sub, :], c, None)
            w_ref[sb * sub:(sb + 1) * sub, :] = w
        c_ref[...] = c
        return jnp.max(c)

    def extra_steps(t, c_max):
        def cond(carry):
            j, c_max = carry
            return (j > 0) & (c_max > F32_EXP2_UNDERFLOW)

        def body(carry):
            j, _ = carry
            j = j - SB_NEXT_SUBS
            start = pl.multiple_of(j * sub, sub)
            kb = k_ref[pl.ds(start, SB_NEXT_SUBS * sub), :]
            z_all = jnp.dot(kb, q4x(t), preferred_element_type=F32)
            c = c_ref[...]
            ws = [None] * SB_NEXT_SUBS
            for sb in reversed(range(SB_NEXT_SUBS)):
                ws[sb], c = sub_block(z_all[sb * sub:(sb + 1) * sub], c, None)
            vtb = vt_ref[0, :, pl.ds(start, SB_NEXT_SUBS * sub)]
            accx_ref[...] += jnp.dot(vtb, jnp.concatenate(ws, axis=0), preferred_element_type=F32)
            c_ref[...] = c
            return j, jnp.max(c)

        lax.while_loop(cond, body, (first_block(t), c_max))

    def finish(t):
        vtb = vt_ref[0, :, pl.ds(first_key(t), n_sub * sub)]
        acc = jnp.dot(vtb, w_ref[...], preferred_element_type=F32) + accx_ref[...]
        row_o = lax.broadcasted_iota(jnp.int32, (LANES, half), 0)
        out_t = jnp.concatenate(
            [jnp.where(row_o < SB_HEAD_DIM, acc[:, g * gw:g * gw + half],
                       acc[:, g * gw + half:(g + 1) * gw]) for g in range(2)], axis=1)
        o_ref[pl.ds(pl.multiple_of(t * tq, tq), tq), :] = out_t.T.astype(o_ref.dtype)

    w_ref[...] = jnp.zeros_like(w_ref)
    accx_ref[...] = jnp.zeros_like(accx_ref)
    z_ref[...] = scores(0)
    first_step(0)
    z_ref[...] = scores(1)

    def body(t, carry):
        finish(t - 1)
        z_next = scores(jnp.minimum(t + 1, n_tiles - 1))
        accx_ref[...] = jnp.zeros_like(accx_ref)
        c_max = first_step(SB_PRE_SUBS)
        z_ref[...] = z_next
        extra_steps(t, c_max)
        return carry

    lax.fori_loop(1, n_tiles, body, 0)
    finish(n_tiles - 1)


def _sb_attention(qt, k, vt, batch, seq):
    n, d = k.shape
    tq = SB_Q_TILE
    rows = SB_PRE_SUBS * SB_SUB + tq
    assert seq >= rows and seq // tq >= 2
    return pl.pallas_call(
        _sb_attn_kernel,
        grid=(batch, d // LANES),
        in_specs=[pl.BlockSpec((1, LANES, seq), lambda b, p: (b, p, 0)),
                  pl.BlockSpec((seq, LANES), lambda b, p: (b, p)),
                  pl.BlockSpec((1, LANES, seq), lambda b, p: (b, p, 0))],
        out_specs=pl.BlockSpec((seq, LANES), lambda b, p: (b, p)),
        out_shape=jax.ShapeDtypeStruct((n, d), BF16),
        scratch_shapes=[pltpu.VMEM((rows, 2 * tq), F32), pltpu.VMEM((rows, 2 * tq), BF16),
                        pltpu.VMEM((1, 2 * tq), F32), pltpu.VMEM((LANES, 2 * tq), F32)],
        compiler_params=pltpu.CompilerParams(dimension_semantics=("arbitrary",) * 2,
                                             vmem_limit_bytes=VMEM_LIMIT_BYTES),
        name="sb_attention",
    )(qt, k, vt)


def _diff_attn_kernel(qt_ref, k_ref, vt_ref, lq1_ref, lk1_ref, lq2_ref, lk2_ref, sub_ref, o_ref,
                      m_ref, l_ref, alpha_ref, acc_ref, s_ref, p_ref, smax_ref, *, lam_init):
    t = DIFF_TILE
    qi = pl.program_id(2)
    qt = qt_ref[0]
    row = lax.broadcasted_iota(jnp.int32, qt.shape, 0)
    zero = jnp.zeros_like(qt)
    q2x = jnp.concatenate([jnp.where(row < DIFF_HEAD_DIM, qt, zero),
                           jnp.where(row >= DIFF_HEAD_DIM, qt, zero)], axis=1)

    m_ref[...] = jnp.full_like(m_ref, NEG_INF)
    l_ref[...] = jnp.zeros_like(l_ref)
    alpha_ref[...] = jnp.ones_like(alpha_ref)
    acc_ref[...] = jnp.zeros_like(acc_ref)
    p_ref[...] = jnp.zeros_like(p_ref)

    def scores(j):
        kb = k_ref[pl.ds(pl.multiple_of(j * t, t), t), :]
        return jnp.dot(kb, q2x, preferred_element_type=F32)

    def stage_scores(s):
        s_ref[...] = s
        smax_ref[...] = jnp.max(s.reshape(t // 8, 8, 2 * t), axis=0)

    def softmax(diagonal):
        s = s_ref[...]
        if diagonal:
            key_chunk = lax.broadcasted_iota(jnp.int32, s.shape, 0) // CHUNK
            q_chunk = (lax.broadcasted_iota(jnp.int32, s.shape, 1) % t) // CHUNK
            s = jnp.where(key_chunk <= q_chunk, s, NEG_INF)
            smax = jnp.max(s, axis=0, keepdims=True)
        else:
            smax = jnp.max(smax_ref[...], axis=0, keepdims=True)
        m_old = m_ref[...]
        m_new = jnp.maximum(m_old, smax)
        p = jnp.exp2(s - m_new)
        alpha = jnp.exp2(m_old - m_new)
        l_ref[...] = alpha * l_ref[...] + jnp.sum(p, axis=0, keepdims=True)
        p_ref[...] = p.astype(BF16)
        alpha_ref[...] = alpha
        m_ref[...] = m_new

    def pv(j):
        vtb = vt_ref[0, :, pl.ds(pl.multiple_of(j * t, t), t)]
        acc_ref[...] = alpha_ref[...] * acc_ref[...] + jnp.dot(vtb, p_ref[...],
                                                               preferred_element_type=F32)

    stage_scores(scores(0))

    def body(j, carry):
        s_next = scores(j + 1)
        pv(jnp.maximum(j - 1, 0))
        softmax(False)
        stage_scores(s_next)
        return carry

    lax.fori_loop(0, qi, body, 0)
    pv(jnp.maximum(qi - 1, 0))
    softmax(True)
    pv(qi)

    lam = (jnp.exp(jnp.sum(lq1_ref[...] * lk1_ref[...], axis=-1, keepdims=True))
           - jnp.exp(jnp.sum(lq2_ref[...] * lk2_ref[...], axis=-1, keepdims=True)) + lam_init)
    on = acc_ref[...] / l_ref[...]
    o = on[:, :t] - lam * on[:, t:]
    ms = jnp.mean(o * o, axis=0, keepdims=True)
    o = o * lax.rsqrt(ms + EPS)
    o = o.T * sub_ref[...] * (1.0 - lam_init)
    o_ref[...] = o.astype(o_ref.dtype)


def _diff_attention(qt, k, vt, lq1, lk1, lq2, lk2, sub_g, batch, seq, lam_init):
    n, d = k.shape
    t = DIFF_TILE
    nq = seq // t
    small = _const_spec((1, DIFF_HEAD_DIM))
    return pl.pallas_call(
        functools.partial(_diff_attn_kernel, lam_init=lam_init),
        grid=(batch, d // LANES, nq),
        in_specs=[pl.BlockSpec((1, LANES, t), lambda b, h, i: (b, h, i)),
                  pl.BlockSpec((seq, LANES), lambda b, h, i: (b, h)),
                  pl.BlockSpec((1, LANES, seq), lambda b, h, i: (b, h, 0)),
                  small, small, small, small, _const_spec((1, LANES))],
        out_specs=pl.BlockSpec((t, LANES), lambda b, h, i: (b * nq + i, h)),
        out_shape=jax.ShapeDtypeStruct((n, d), BF16),
        scratch_shapes=[pltpu.VMEM((1, 2 * t), F32), pltpu.VMEM((1, 2 * t), F32),
                        pltpu.VMEM((1, 2 * t), F32), pltpu.VMEM((LANES, 2 * t), F32),
                        pltpu.VMEM((t, 2 * t), F32), pltpu.VMEM((t, 2 * t), BF16),
                        pltpu.VMEM((8, 2 * t), F32)],
        compiler_params=pltpu.CompilerParams(dimension_semantics=("arbitrary",) * 3,
                                             vmem_limit_bytes=VMEM_LIMIT_BYTES),
        name="diff_attention",
    )(qt, k, vt, lq1, lk1, lq2, lk2, sub_g)


def _mix_ffn_kernel(x_ref, o_ref, wo_ref, g_ref, wg_ref, wu_ref, wd_ref, out_ref):
    x1 = x_ref[...] + jnp.dot(o_ref[...], wo_ref[...], preferred_element_type=F32)
    h = _rms(x1, g_ref[...]).astype(BF16)
    acc = x1
    for lo, hi in FFN_CHUNKS:
        g = jnp.dot(h, wg_ref[:, lo:hi], preferred_element_type=F32)
        u = jnp.dot(h, wu_ref[:, lo:hi], preferred_element_type=F32)
        a = (g * (1.0 / (1.0 + jnp.exp(-g))) * u).astype(BF16)
        acc = acc + jnp.dot(a, wd_ref[lo:hi, :], preferred_element_type=F32)
    out_ref[...] = acc


def _mix_ffn(x2, o, wo, gain, wg, wu, wd):
    n, d = x2.shape
    tm = TOKEN_TILE
    assert FFN_CHUNKS[-1][1] == wg.shape[1]
    row_spec = pl.BlockSpec((tm, d), lambda i: (i, 0))
    return pl.pallas_call(
        _mix_ffn_kernel,
        grid=(n // tm,),
        in_specs=[row_spec, row_spec, _const_spec(wo.shape), _const_spec((1, d)),
                  _const_spec(wg.shape), _const_spec(wu.shape), _const_spec(wd.shape)],
        out_specs=row_spec,
        out_shape=jax.ShapeDtypeStruct((n, d), F32),
        compiler_params=pltpu.CompilerParams(dimension_semantics=("arbitrary",),
                                             vmem_limit_bytes=VMEM_LIMIT_BYTES),
        name="mix_ffn",
    )(x2, o, wo, gain, wg, wu, wd)


def _rope_tables(positions):
    batch, seq = positions.shape
    half = ROT_DIM // 2
    inv_freq = ROPE_THETA ** (-jnp.arange(0, ROT_DIM, 2, dtype=F32) / ROT_DIM)
    ang = positions.astype(F32).reshape(-1, 1) * inv_freq
    cos, sin = jnp.cos(ang), jnp.sin(ang)
    n = ang.shape[0]
    pad = DIFF_HEAD_DIM - ROT_DIM
    cos64 = jnp.concatenate([cos, cos, jnp.ones((n, pad), F32)], axis=-1)
    lo64 = jnp.concatenate([-sin, jnp.zeros((n, half + pad), F32)], axis=-1)
    hi64 = jnp.concatenate([jnp.zeros((n, half), F32), sin, jnp.zeros((n, pad), F32)], axis=-1)
    rep = LANES // DIFF_HEAD_DIM
    token_major = tuple(jnp.tile(t, (1, rep)) for t in (cos64, lo64, hi64))
    freq_major = tuple(jnp.swapaxes(t.reshape(batch, seq, half), 1, 2) for t in (cos, sin))
    return token_major + freq_major


def kernel(x, positions, attn_norm, w_in, w_out, q_norm, k_norm, lambda_q1, lambda_k1, lambda_q2,
           lambda_k2, sub_norm, ffn_norm, w_gate, w_up, w_down):
    batch, seq, d = x.shape
    depth = w_in.shape[0]
    assert seq % TOKEN_TILE == 0 and seq % SB_Q_TILE == 0 and seq % DIFF_TILE == 0
    assert d % LANES == 0 and DIFF_V_DIM == LANES and DIFF_TILE % CHUNK == 0
    rep = LANES // DIFF_HEAD_DIM
    x2 = x.reshape(batch * seq, d)
    tables = _rope_tables(positions)
    for i in range(depth):
        gain = attn_norm[i].reshape(1, d)
        if i % N_MIXERS == 0:
            qt, k, vt = _qkv_proj(x2, gain, w_in[i], batch, seq, SB_HEAD_DIM ** -0.5 * LOG2E)
            o = _sb_attention(qt, k, vt, batch, seq)
        else:
            j = i // N_MIXERS
            qg = jnp.broadcast_to(q_norm[j].reshape(-1, 1), (DIFF_HEAD_DIM, LANES))
            kg = jnp.tile(k_norm[j].reshape(1, -1), (1, rep))
            qt, k, vt = _qkv_proj(x2, gain, w_in[i], batch, seq, DIFF_HEAD_DIM ** -0.5 * LOG2E,
                                  (qg, kg) + tables)
            lam_init = 0.8 - 0.6 * math.exp(-0.3 * i)
            o = _diff_attention(qt, k, vt, lambda_q1[j].reshape(1, -1), lambda_k1[j].reshape(1, -1),
                                lambda_q2[j].reshape(1, -1), lambda_k2[j].reshape(1, -1),
                                sub_norm[j].reshape(1, -1), batch, seq, lam_init)
        x2 = _mix_ffn(x2, o, w_out[i].astype(BF16), ffn_norm[i].reshape(1, d),
                      w_gate[i].astype(BF16), w_up[i].astype(BF16), w_down[i].astype(BF16))
    return x2.reshape(batch, seq, d)
```

```python
import functools
import math

import jax
import jax.numpy as jnp
from jax import lax
from jax.experimental import pallas as pl
from jax.experimental.pallas import tpu as pltpu

F32 = jnp.float32
BF16 = jnp.bfloat16

N_MIXERS = 2
SB_HEAD_DIM = 64
DIFF_HEAD_DIM = 64
DIFF_V_DIM = 2 * DIFF_HEAD_DIM
CHUNK = 64
ROPE_THETA = 500000.0
ROT_DIM = DIFF_HEAD_DIM // 4
EPS = 1e-6
NEG_INF = -1e30

LANES = 128
VMEM_LIMIT_BYTES = 56 * 1024 * 1024

TOKEN_TILE = 512
SB_Q_TILE = 256
SB_SUB = 128
SB_PRE_SUBS = 2
SB_NEXT_SUBS = 2
DIFF_TILE = 512
FFN_CHUNKS = ((0, 1024), (1024, 2048), (2048, 2816))

LOG2E = 1.4426950408889634
F32_EXP2_UNDERFLOW = -150.0


def _rms(x, gain):
    ms = jnp.mean(x * x, axis=-1, keepdims=True)
    return x * lax.rsqrt(ms + EPS) * gain


def _const_spec(shape):
    nd = len(shape)
    return pl.BlockSpec(shape, lambda *_: (0,) * nd, pipeline_mode=pl.Buffered(1))


def _qkv_kernel(x_ref, g_ref, wk_ref, wqvt_ref, qt_ref, k_ref, vt_ref, *, d_model, scale):
    h = _rms(x_ref[...], g_ref[...]).astype(BF16)
    k_ref[...] = jnp.dot(h, wk_ref[...], preferred_element_type=F32).astype(BF16)
    qvt = lax.dot_general(wqvt_ref[...], h, (((1,), (1,)), ((), ())), preferred_element_type=F32)
    qt_ref[0] = (qvt[:d_model, :] * scale).astype(BF16)
    vt_ref[0] = qvt[d_model:, :].astype(BF16)


def _qk_norm_rope(t, gain, cos, sin_lo, sin_hi, lane_lo):
    sq = t * t
    lo = jnp.sum(jnp.where(lane_lo, sq, 0.0), axis=-1, keepdims=True)
    hi = jnp.sum(jnp.where(lane_lo, 0.0, sq), axis=-1, keepdims=True)
    inv = 1.0 / DIFF_HEAD_DIM
    r = jnp.where(lane_lo, lax.rsqrt(lo * inv + EPS), lax.rsqrt(hi * inv + EPS))
    y = t * r * gain
    half = ROT_DIM // 2
    return y * cos + pltpu.roll(y, LANES - half, 1) * sin_lo + pltpu.roll(y, half, 1) * sin_hi


def _qkv_diff_kernel(x_ref, g_ref, wk_ref, wqvt_ref, qg_ref, kg_ref, cos_ref, sl_ref, sh_ref,
                     cost_ref, sint_ref, qt_ref, k_ref, vt_ref, *, d_model, scale):
    h = _rms(x_ref[...], g_ref[...]).astype(BF16)
    tm = h.shape[0]
    cos, sin_lo, sin_hi = cos_ref[...], sl_ref[...], sh_ref[...]
    lane_lo = lax.broadcasted_iota(jnp.int32, cos.shape, 1) < DIFF_HEAD_DIM
    t = jnp.dot(h, wk_ref[...], preferred_element_type=F32)
    gain = kg_ref[...]
    for c in range(d_model // LANES):
        sl = slice(c * LANES, (c + 1) * LANES)
        k_ref[:, sl] = _qk_norm_rope(t[:, sl], gain, cos, sin_lo, sin_hi, lane_lo).astype(BF16)
    qvt = lax.dot_general(wqvt_ref[...], h, (((1,), (1,)), ((), ())), preferred_element_type=F32)
    vt_ref[0] = qvt[d_model:, :].astype(BF16)
    cos_t, sin_t = cost_ref[0], sint_ref[0]
    gq = jnp.tile(qg_ref[...], (1, tm // LANES))
    half = ROT_DIM // 2
    for g in range(d_model // DIFF_HEAD_DIM):
        rows = slice(g * DIFF_HEAD_DIM, (g + 1) * DIFF_HEAD_DIM)
        xg = qvt[rows, :]
        ms = jnp.mean(xg * xg, axis=0, keepdims=True)
        y = xg * lax.rsqrt(ms + EPS) * gq
        x1, x2 = y[:half], y[half:ROT_DIM]
        out = jnp.concatenate([x1 * cos_t - x2 * sin_t, x2 * cos_t + x1 * sin_t, y[ROT_DIM:]], axis=0)
        qt_ref[0, rows, :] = (out * scale).astype(BF16)


def _qkv_proj(x2, gain, w_in, batch, seq, scale, diff_args=None):
    n, d = x2.shape
    tm = TOKEN_TILE
    tiles_per_seq = seq // tm
    row_spec = pl.BlockSpec((tm, d), lambda i: (i, 0))
    t_spec = pl.BlockSpec((1, d, tm), lambda i: (i // tiles_per_seq, 0, i % tiles_per_seq))
    row_shape = jax.ShapeDtypeStruct((n, d), BF16)
    t_shape = jax.ShapeDtypeStruct((batch, d, seq), BF16)
    wk = w_in[:, d:2 * d].astype(BF16)
    wqvt = jnp.concatenate([w_in[:, :d], w_in[:, 2 * d:]], axis=1).T.astype(BF16)
    if diff_args is None:
        body = functools.partial(_qkv_kernel, d_model=d, scale=scale)
        name = "qkv_sb"
        in_specs = [row_spec, _const_spec((1, d)), _const_spec(wk.shape), _const_spec(wqvt.shape)]
        args = [x2, gain, wk, wqvt]
    else:
        qg, kg, cos, sin_lo, sin_hi, cos_t, sin_t = diff_args
        tab_spec = pl.BlockSpec((tm, LANES), lambda i: (i, 0))
        tabt_spec = pl.BlockSpec((1, ROT_DIM // 2, tm),
                                 lambda i: (i // tiles_per_seq, 0, i % tiles_per_seq))
        body = functools.partial(_qkv_diff_kernel, d_model=d, scale=scale)
        name = "qkv_diff"
        in_specs = [row_spec, _const_spec((1, d)), _const_spec(wk.shape), _const_spec(wqvt.shape),
                    _const_spec(qg.shape), _const_spec(kg.shape), tab_spec, tab_spec, tab_spec,
                    tabt_spec, tabt_spec]
        args = [x2, gain, wk, wqvt, qg, kg, cos, sin_lo, sin_hi, cos_t, sin_t]
    return pl.pallas_call(
        body,
        grid=(n // tm,),
        in_specs=in_specs,
        out_specs=[t_spec, row_spec, t_spec],
        out_shape=[t_shape, row_shape, t_shape],
        compiler_params=pltpu.CompilerParams(dimension_semantics=("arbitrary",),
                                             vmem_limit_bytes=VMEM_LIMIT_BYTES),
        name=name,
    )(*args)


def _sb_attn_kernel(qt_ref, k_ref, vt_ref, o_ref, z_ref, w_ref, c_ref, accx_ref):
    tq, sub = SB_Q_TILE, SB_SUB
    half = tq // 2
    gw = 2 * half
    n_tiles = k_ref.shape[0] // tq
    n_sub = SB_PRE_SUBS + tq // sub
    r_u = lax.broadcasted_iota(jnp.int32, (sub, 2 * sub), 0)
    c_u = lax.broadcasted_iota(jnp.int32, (sub, 2 * sub), 1) % sub
    later2 = jnp.where(c_u > r_u, 1.0, 0.0).astype(BF16)
    tri = (lax.broadcasted_iota(jnp.int32, (sub, gw), 0)
           < lax.broadcasted_iota(jnp.int32, (sub, gw), 1) % half)

    def q4x(t):
        qt = qt_ref[0, :, pl.ds(pl.multiple_of(t * tq, tq), tq)]
        row = lax.broadcasted_iota(jnp.int32, (LANES, half), 0)
        zero = jnp.zeros((LANES, half), qt.dtype)
        parts = []
        for g in range(2):
            qg = qt[:, g * half:(g + 1) * half]
            parts += [jnp.where(row < SB_HEAD_DIM, qg, zero), jnp.where(row >= SB_HEAD_DIM, qg, zero)]
        return jnp.concatenate(parts, axis=1)

    def first_block(t):
        return jnp.maximum(t * (tq // sub) - SB_PRE_SUBS, 0)

    def first_key(t):
        return pl.multiple_of(first_block(t) * sub, sub)

    def scores(t):
        kb = k_ref[pl.ds(first_key(t), n_sub * sub), :]
        return jnp.dot(kb, q4x(t), preferred_element_type=F32)

    def sub_block(z, c, mask):
        nabs = pltpu.bitcast(pltpu.bitcast(z, jnp.uint32) | jnp.uint32(0x80000000), F32)
        sp = jnp.log(1.0 + jnp.exp2(nabs)) * LOG2E
        ls = jnp.minimum(z, 0.0) - sp
        lk = ls - z
        if mask is not None:
            lk = jnp.where(mask, lk, 0.0)
        hi = lk.astype(BF16)
        lo = (lk - hi.astype(F32)).astype(BF16)
        after = jnp.dot(later2, jnp.concatenate([hi, lo], axis=0), preferred_element_type=F32)
        w = jnp.exp2(ls + after + c)
        if mask is not None:
            w = jnp.where(mask, w, 0.0)
        return w.astype(BF16), c + after[0:1, :] + lk[0:1, :]

    def first_step(first_sb):
        zero_c = jnp.zeros((1, gw), F32)
        lo_rows = slice(first_sb * sub, (first_sb + 1) * sub)
        hi_rows = slice((first_sb + 1) * sub, (first_sb + 2) * sub)
        wb, cb = sub_block(z_ref[hi_rows, gw:], zero_c, tri)
        w_ref[hi_rows, gw:] = wb
        wa, ca = sub_block(z_ref[lo_rows, :gw], zero_c, tri)
        wb, cb = sub_block(z_ref[lo_rows, gw:], cb, None)
        w_ref[lo_rows, :gw] = wa
        w_ref[lo_rows, gw:] = wb
        c = jnp.concatenate([ca, cb], axis=1)
        for sb in reversed(range(first_sb)):
            w, c = sub_block(z_ref[sb * sub:(sb + 1) * sub, :], c, None)
            w_ref[sb * sub:(sb + 1) * sub, :] = w
        c_ref[...] = c
        return jnp.max(c)

    def extra_steps(t, c_max):
        def cond(carry):
            j, c_max = carry
            return (j > 0) & (c_max > F32_EXP2_UNDERFLOW)

        def body(carry):
            j, _ = carry
            j = j - SB_NEXT_SUBS
            start = pl.multiple_of(j * sub, sub)
            kb = k_ref[pl.ds(start, SB_NEXT_SUBS * sub), :]
            z_all = jnp.dot(kb, q4x(t), preferred_element_type=F32)
            c = c_ref[...]
            ws = [None] * SB_NEXT_SUBS
            for sb in reversed(range(SB_NEXT_SUBS)):
                ws[sb], c = sub_block(z_all[sb * sub:(sb + 1) * sub], c, None)
            vtb = vt_ref[0, :, pl.ds(start, SB_NEXT_SUBS * sub)]
            accx_ref[...] += jnp.dot(vtb, jnp.concatenate(ws, axis=0), preferred_element_type=F32)
            c_ref[...] = c
            return j, jnp.max(c)

        lax.while_loop(cond, body, (first_block(t), c_max))

    def finish(t):
        vtb = vt_ref[0, :, pl.ds(first_key(t), n_sub * sub)]
        acc = jnp.dot(vtb, w_ref[...], preferred_element_type=F32) + accx_ref[...]
        row_o = lax.broadcasted_iota(jnp.int32, (LANES, half), 0)
        out_t = jnp.concatenate(
            [jnp.where(row_o < SB_HEAD_DIM, acc[:, g * gw:g * gw + half],
                       acc[:, g * gw + half:(g + 1) * gw]) for g in range(2)], axis=1)
        o_ref[pl.ds(pl.multiple_of(t * tq, tq), tq), :] = out_t.T.astype(o_ref.dtype)

    w_ref[...] = jnp.zeros_like(w_ref)
    accx_ref[...] = jnp.zeros_like(accx_ref)
    z_ref[...] = scores(0)
    first_step(0)
    z_ref[...] = scores(1)

    def body(t, carry):
        finish(t - 1)
        z_next = scores(jnp.minimum(t + 1, n_tiles - 1))
        accx_ref[...] = jnp.zeros_like(accx_ref)
        c_max = first_step(SB_PRE_SUBS)
        z_ref[...] = z_next
        extra_steps(t, c_max)
        return carry

    lax.fori_loop(1, n_tiles, body, 0)
    finish(n_tiles - 1)


def _sb_attention(qt, k, vt, batch, seq):
    n, d = k.shape
    tq = SB_Q_TILE
    rows = SB_PRE_SUBS * SB_SUB + tq
    assert seq >= rows and seq // tq >= 2
    return pl.pallas_call(
        _sb_attn_kernel,
        grid=(batch, d // LANES),
        in_specs=[pl.BlockSpec((1, LANES, seq), lambda b, p: (b, p, 0)),
                  pl.BlockSpec((seq, LANES), lambda b, p: (b, p)),
                  pl.BlockSpec((1, LANES, seq), lambda b, p: (b, p, 0))],
        out_specs=pl.BlockSpec((seq, LANES), lambda b, p: (b, p)),
        out_shape=jax.ShapeDtypeStruct((n, d), BF16),
        scratch_shapes=[pltpu.VMEM((rows, 2 * tq), F32), pltpu.VMEM((rows, 2 * tq), BF16),
                        pltpu.VMEM((1, 2 * tq), F32), pltpu.VMEM((LANES, 2 * tq), F32)],
        compiler_params=pltpu.CompilerParams(dimension_semantics=("arbitrary",) * 2,
                                             vmem_limit_bytes=VMEM_LIMIT_BYTES),
        name="sb_attention",
    )(qt, k, vt)


def _diff_attn_kernel(qt_ref, k_ref, vt_ref, lq1_ref, lk1_ref, lq2_ref, lk2_ref, sub_ref, o_ref,
                      m_ref, l_ref, acc_ref, alpha_ref, s_ref, p_ref, smax_ref, *, lam_init):
    t = DIFF_TILE
    n_tiles = k_ref.shape[0] // t

    def q2x(i):
        qt = qt_ref[0, :, pl.ds(pl.multiple_of(i * t, t), t)]
        row = lax.broadcasted_iota(jnp.int32, qt.shape, 0)
        zero = jnp.zeros_like(qt)
        return jnp.concatenate([jnp.where(row < DIFF_HEAD_DIM, qt, zero),
                                jnp.where(row >= DIFF_HEAD_DIM, qt, zero)], axis=1)

    def scores(i, j):
        kb = k_ref[pl.ds(pl.multiple_of(j * t, t), t), :]
        return jnp.dot(kb, q2x(i), preferred_element_type=F32)

    def stage_scores(s):
        s_ref[...] = s
        smax_ref[...] = jnp.max(s.reshape(t // 8, 8, 2 * t), axis=0)

    def softmax_first(i):
        s = s_ref[...]
        key_chunk = lax.broadcasted_iota(jnp.int32, s.shape, 0) // CHUNK
        q_chunk = (lax.broadcasted_iota(jnp.int32, s.shape, 1) % t) // CHUNK
        s = jnp.where(key_chunk <= q_chunk, s, NEG_INF)
        m_new = jnp.max(s, axis=0, keepdims=True)
        p = jnp.exp2(s - m_new)
        l_ref[i] = jnp.sum(p, axis=0, keepdims=True)
        p_ref[...] = p.astype(BF16)
        m_ref[i] = m_new

    def softmax_next(i):
        s = s_ref[...]
        m_old = m_ref[i]
        m_new = jnp.maximum(m_old, jnp.max(smax_ref[...], axis=0, keepdims=True))
        p = jnp.exp2(s - m_new)
        alpha = jnp.exp2(m_old - m_new)
        l_ref[i] = alpha * l_ref[i] + jnp.sum(p, axis=0, keepdims=True)
        p_ref[...] = p.astype(BF16)
        alpha_ref[...] = alpha
        m_ref[i] = m_new

    def pv(j):
        vtb = vt_ref[0, :, pl.ds(pl.multiple_of(j * t, t), t)]
        return jnp.dot(vtb, p_ref[...], preferred_element_type=F32)

    s_ref[...] = scores(0, 0)
    softmax_first(0)
    s_ref[...] = scores(1, 1)

    def diag_body(i, carry):
        acc_ref[i - 1] = pv(i - 1)
        nxt = jnp.minimum(i + 1, n_tiles - 1)
        s_next = scores(nxt, nxt)
        softmax_first(i)
        s_ref[...] = s_next
        return carry

    lax.fori_loop(1, n_tiles, diag_body, 0)
    acc_ref[n_tiles - 1] = pv(n_tiles - 1)

    def next_pair(i, j):
        wrap = j + 1 >= i
        return jnp.where(wrap, i + 1, i), jnp.where(wrap, 0, j + 1)

    stage_scores(scores(1, 0))
    softmax_next(1)
    i1, j1 = next_pair(1, 0)
    i1c = jnp.minimum(i1, n_tiles - 1)
    stage_scores(scores(i1c, jnp.minimum(j1, i1c - 1)))
    n_pairs = n_tiles * (n_tiles - 1) // 2

    def pair_body(n, carry):
        ip, jp, i, j = carry
        acc_ref[ip] = alpha_ref[...] * acc_ref[ip] + pv(jp)
        i2, j2 = next_pair(i, j)
        i2c = jnp.minimum(i2, n_tiles - 1)
        s_next = scores(i2c, jnp.minimum(j2, i2c - 1))
        softmax_next(i)
        stage_scores(s_next)
        return i, j, i2, j2

    ip, jp, _, _ = lax.fori_loop(1, n_pairs, pair_body, (1, 0, i1, j1))
    acc_ref[ip] = alpha_ref[...] * acc_ref[ip] + pv(jp)

    lam = (jnp.exp(jnp.sum(lq1_ref[...] * lk1_ref[...], axis=-1, keepdims=True))
           - jnp.exp(jnp.sum(lq2_ref[...] * lk2_ref[...], axis=-1, keepdims=True)) + lam_init)

    def out_body(i, carry):
        on = acc_ref[i] / l_ref[i]
        o = on[:, :t] - lam * on[:, t:]
        ms = jnp.mean(o * o, axis=0, keepdims=True)
        o = o * lax.rsqrt(ms + EPS)
        o = o.T * sub_ref[...] * (1.0 - lam_init)
        o_ref[pl.ds(pl.multiple_of(i * t, t), t), :] = o.astype(o_ref.dtype)
        return carry

    lax.fori_loop(0, n_tiles, out_body, 0)


def _diff_attention(qt, k, vt, lq1, lk1, lq2, lk2, sub_g, batch, seq, lam_init):
    n, d = k.shape
    t = DIFF_TILE
    n_tiles = seq // t
    assert n_tiles >= 2
    small = _const_spec((1, DIFF_HEAD_DIM))
    return pl.pallas_call(
        functools.partial(_diff_attn_kernel, lam_init=lam_init),
        grid=(batch, d // LANES),
        in_specs=[pl.BlockSpec((1, LANES, seq), lambda b, h: (b, h, 0)),
                  pl.BlockSpec((seq, LANES), lambda b, h: (b, h)),
                  pl.BlockSpec((1, LANES, seq), lambda b, h: (b, h, 0)),
                  small, small, small, small, _const_spec((1, LANES))],
        out_specs=pl.BlockSpec((seq, LANES), lambda b, h: (b, h)),
        out_shape=jax.ShapeDtypeStruct((n, d), BF16),
        scratch_shapes=[pltpu.VMEM((n_tiles, 1, 2 * t), F32), pltpu.VMEM((n_tiles, 1, 2 * t), F32),
                        pltpu.VMEM((n_tiles, LANES, 2 * t), F32), pltpu.VMEM((1, 2 * t), F32),
                        pltpu.VMEM((t, 2 * t), F32), pltpu.VMEM((t, 2 * t), BF16),
                        pltpu.VMEM((8, 2 * t), F32)],
        compiler_params=pltpu.CompilerParams(dimension_semantics=("arbitrary",) * 2,
                                             vmem_limit_bytes=VMEM_LIMIT_BYTES),
        name="diff_attention",
    )(qt, k, vt, lq1, lk1, lq2, lk2, sub_g)


def _mix_ffn_kernel(x_ref, o_ref, wo_ref, g_ref, wg_ref, wu_ref, wd_ref, out_ref):
    x1 = x_ref[...] + jnp.dot(o_ref[...], wo_ref[...], preferred_element_type=F32)
    h = _rms(x1, g_ref[...]).astype(BF16)
    acc = x1
    for lo, hi in FFN_CHUNKS:
        g = jnp.dot(h, wg_ref[:, lo:hi], preferred_element_type=F32)
        u = jnp.dot(h, wu_ref[:, lo:hi], preferred_element_type=F32)
        a = (g * (1.0 / (1.0 + jnp.exp(-g))) * u).astype(BF16)
        acc = acc + jnp.dot(a, wd_ref[lo:hi, :], preferred_element_type=F32)
    out_ref[...] = acc


def _mix_ffn(x2, o, wo, gain, wg, wu, wd):
    n, d = x2.shape
    tm = TOKEN_TILE
    assert FFN_CHUNKS[-1][1] == wg.shape[1]
    row_spec = pl.BlockSpec((tm, d), lambda i: (i, 0))
    return pl.pallas_call(
        _mix_ffn_kernel,
        grid=(n // tm,),
        in_specs=[row_spec, row_spec, _const_spec(wo.shape), _const_spec((1, d)),
                  _const_spec(wg.shape), _const_spec(wu.shape), _const_spec(wd.shape)],
        out_specs=row_spec,
        out_shape=jax.ShapeDtypeStruct((n, d), F32),
        compiler_params=pltpu.CompilerParams(dimension_semantics=("arbitrary",),
                                             vmem_limit_bytes=VMEM_LIMIT_BYTES),
        name="mix_ffn",
    )(x2, o, wo, gain, wg, wu, wd)


def _rope_tables(positions):
    batch, seq = positions.shape
    half = ROT_DIM // 2
    inv_freq = ROPE_THETA ** (-jnp.arange(0, ROT_DIM, 2, dtype=F32) / ROT_DIM)
    ang = positions.astype(F32).reshape(-1, 1) * inv_freq
    cos, sin = jnp.cos(ang), jnp.sin(ang)
    n = ang.shape[0]
    pad = DIFF_HEAD_DIM - ROT_DIM
    cos64 = jnp.concatenate([cos, cos, jnp.ones((n, pad), F32)], axis=-1)
    lo64 = jnp.concatenate([-sin, jnp.zeros((n, half + pad), F32)], axis=-1)
    hi64 = jnp.concatenate([jnp.zeros((n, half), F32), sin, jnp.zeros((n, pad), F32)], axis=-1)
    rep = LANES // DIFF_HEAD_DIM
    token_major = tuple(jnp.tile(t, (1, rep)) for t in (cos64, lo64, hi64))
    freq_major = tuple(jnp.swapaxes(t.reshape(batch, seq, half), 1, 2) for t in (cos, sin))
    return token_major + freq_major


def kernel(x, positions, attn_norm, w_in, w_out, q_norm, k_norm, lambda_q1, lambda_k1, lambda_q2,
           lambda_k2, sub_norm, ffn_norm, w_gate, w_up, w_down):
    batch, seq, d = x.shape
    depth = w_in.shape[0]
    assert seq % TOKEN_TILE == 0 and seq % SB_Q_TILE == 0 and seq % DIFF_TILE == 0
    assert d % LANES == 0 and DIFF_V_DIM == LANES and DIFF_TILE % CHUNK == 0
    rep = LANES // DIFF_HEAD_DIM
    x2 = x.reshape(batch * seq, d)
    tables = _rope_tables(positions)
    for i in range(depth):
        gain = attn_norm[i].reshape(1, d)
        if i % N_MIXERS == 0:
            qt, k, vt = _qkv_proj(x2, gain, w_in[i], batch, seq, SB_HEAD_DIM ** -0.5 * LOG2E)
            o = _sb_attention(qt, k, vt, batch, seq)
        else:
            j = i // N_MIXERS
            qg = jnp.broadcast_to(q_norm[j].reshape(-1, 1), (DIFF_HEAD_DIM, LANES))
            kg = jnp.tile(k_norm[j].reshape(1, -1), (1, rep))
            qt, k, vt = _qkv_proj(x2, gain, w_in[i], batch, seq, DIFF_HEAD_DIM ** -0.5 * LOG2E,
                                  (qg, kg) + tables)
            lam_init = 0.8 - 0.6 * math.exp(-0.3 * i)
            o = _diff_attention(qt, k, vt, lambda_q1[j].reshape(1, -1), lambda_k1[j].reshape(1, -1),
                                lambda_q2[j].reshape(1, -1), lambda_k2[j].reshape(1, -1),
                                sub_norm[j].reshape(1, -1), batch, seq, lam_init)
        x2 = _mix_ffn(x2, o, w_out[i].astype(BF16), ffn_norm[i].reshape(1, d),
                      w_gate[i].astype(BF16), w_up[i].astype(BF16), w_down[i].astype(BF16))
    return x2.reshape(batch, seq, d)
```

```python
import functools
import math

import jax
import jax.numpy as jnp
from jax import lax
from jax.experimental import pallas as pl
from jax.experimental.pallas import tpu as pltpu

F32 = jnp.float32
BF16 = jnp.bfloat16

N_MIXERS = 2
SB_HEAD_DIM = 64
DIFF_HEAD_DIM = 64
DIFF_V_DIM = 2 * DIFF_HEAD_DIM
CHUNK = 64
ROPE_THETA = 500000.0
ROT_DIM = DIFF_HEAD_DIM // 4
EPS = 1e-6
NEG_INF = -1e30

LANES = 128
VMEM_LIMIT_BYTES = 56 * 1024 * 1024

TOKEN_TILE = 512
SB_Q_TILE = 256
SB_SUB = 128
SB_PRE_SUBS = 2
SB_NEXT_SUBS = 2
DIFF_TILE = 512
FFN_CHUNKS = ((0, 1024), (1024, 2048), (2048, 2816))

LOG2E = 1.4426950408889634
F32_EXP2_UNDERFLOW = -150.0


def _rms(x, gain):
    ms = jnp.mean(x * x, axis=-1, keepdims=True)
    return x * lax.rsqrt(ms + EPS) * gain


def _const_spec(shape):
    nd = len(shape)
    return pl.BlockSpec(shape, lambda *_: (0,) * nd, pipeline_mode=pl.Buffered(1))


def _qkv_kernel(x_ref, g_ref, wk_ref, wqvt_ref, qt_ref, k_ref, vt_ref, *, d_model, scale):
    h = _rms(x_ref[...], g_ref[...]).astype(BF16)
    k_ref[...] = jnp.dot(h, wk_ref[...], preferred_element_type=F32).astype(BF16)
    qvt = lax.dot_general(wqvt_ref[...], h, (((1,), (1,)), ((), ())), preferred_element_type=F32)
    qt_ref[0] = (qvt[:d_model, :] * scale).astype(BF16)
    vt_ref[0] = qvt[d_model:, :].astype(BF16)


def _qkv_diff_kernel(x_ref, g_ref, wt_ref, qg_ref, kg_ref, cost_ref, sint_ref,
                     qt_ref, k_ref, vt_ref, *, d_model, scale):
    h = _rms(x_ref[...], g_ref[...]).astype(BF16)
    tm = h.shape[0]
    cos_t, sin_t = cost_ref[0], sint_ref[0]
    half = ROT_DIM // 2

    def project(part):
        w = wt_ref[part * d_model:(part + 1) * d_model, :]
        return lax.dot_general(w, h, (((1,), (1,)), ((), ())), preferred_element_type=F32)

    def norm_rope(xg, gain):
        ms = jnp.mean(xg * xg, axis=0, keepdims=True)
        y = xg * lax.rsqrt(ms + EPS) * gain
        x1, x2 = y[:half], y[half:ROT_DIM]
        return jnp.concatenate([x1 * cos_t - x2 * sin_t, x2 * cos_t + x1 * sin_t, y[ROT_DIM:]],
                               axis=0)

    def slabs(xt, gain):
        per_slab = LANES // DIFF_HEAD_DIM
        for c in range(d_model // LANES):
            parts = [norm_rope(xt[g * DIFF_HEAD_DIM:(g + 1) * DIFF_HEAD_DIM, :], gain)
                     for g in range(c * per_slab, (c + 1) * per_slab)]
            yield slice(c * LANES, (c + 1) * LANES), jnp.concatenate(parts, axis=0)

    for sl, y in slabs(project(1), jnp.tile(kg_ref[...], (1, tm // LANES))):
        k_ref[:, sl] = y.T.astype(BF16)
    for sl, y in slabs(project(0), jnp.tile(qg_ref[...], (1, tm // LANES))):
        qt_ref[0, sl, :] = (y * scale).astype(BF16)
    vt_ref[0] = project(2).astype(BF16)


def _qkv_proj(x2, gain, w_in, batch, seq, scale, diff_args=None):
    n, d = x2.shape
    tm = TOKEN_TILE
    tiles_per_seq = seq // tm
    row_spec = pl.BlockSpec((tm, d), lambda i: (i, 0))
    t_spec = pl.BlockSpec((1, d, tm), lambda i: (i // tiles_per_seq, 0, i % tiles_per_seq))
    row_shape = jax.ShapeDtypeStruct((n, d), BF16)
    t_shape = jax.ShapeDtypeStruct((batch, d, seq), BF16)
    if diff_args is None:
        wk = w_in[:, d:2 * d].astype(BF16)
        wqvt = jnp.concatenate([w_in[:, :d], w_in[:, 2 * d:]], axis=1).T.astype(BF16)
        body = functools.partial(_qkv_kernel, d_model=d, scale=scale)
        name = "qkv_sb"
        in_specs = [row_spec, _const_spec((1, d)), _const_spec(wk.shape), _const_spec(wqvt.shape)]
        args = [x2, gain, wk, wqvt]
    else:
        qg, kg, cos_t, sin_t = diff_args
        wt = w_in.T.astype(BF16)
        tabt_spec = pl.BlockSpec((1, ROT_DIM // 2, tm),
                                 lambda i: (i // tiles_per_seq, 0, i % tiles_per_seq))
        body = functools.partial(_qkv_diff_kernel, d_model=d, scale=scale)
        name = "qkv_diff"
        in_specs = [row_spec, _const_spec((1, d)), _const_spec(wt.shape),
                    _const_spec(qg.shape), _const_spec(kg.shape), tabt_spec, tabt_spec]
        args = [x2, gain, wt, qg, kg, cos_t, sin_t]
    return pl.pallas_call(
        body,
        grid=(n // tm,),
        in_specs=in_specs,
        out_specs=[t_spec, row_spec, t_spec],
        out_shape=[t_shape, row_shape, t_shape],
        compiler_params=pltpu.CompilerParams(dimension_semantics=("arbitrary",),
                                             vmem_limit_bytes=VMEM_LIMIT_BYTES),
        name=name,
    )(*args)


def _sb_attn_kernel(qt_ref, k_ref, vt_ref, o_ref, z_ref, w_ref, c_ref, accx_ref):
    tq, sub = SB_Q_TILE, SB_SUB
    half = tq // 2
    gw = 2 * half
    n_tiles = k_ref.shape[0] // tq
    n_sub = SB_PRE_SUBS + tq // sub
    r_u = lax.broadcasted_iota(jnp.int32, (sub, 2 * sub), 0)
    c_u = lax.broadcasted_iota(jnp.int32, (sub, 2 * sub), 1) % sub
    later2 = jnp.where(c_u > r_u, 1.0, 0.0).astype(BF16)
    tri = (lax.broadcasted_iota(jnp.int32, (sub, gw), 0)
           < lax.broadcasted_iota(jnp.int32, (sub, gw), 1) % half)

    def q4x(t):
        qt = qt_ref[0, :, pl.ds(pl.multiple_of(t * tq, tq), tq)]
        row = lax.broadcasted_iota(jnp.int32, (LANES, half), 0)
        zero = jnp.zeros((LANES, half), qt.dtype)
        parts = []
        for g in range(2):
            qg = qt[:, g * half:(g + 1) * half]
            parts += [jnp.where(row < SB_HEAD_DIM, qg, zero), jnp.where(row >= SB_HEAD_DIM, qg, zero)]
        return jnp.concatenate(parts, axis=1)

    def first_block(t):
        return jnp.maximum(t * (tq // sub) - SB_PRE_SUBS, 0)

    def first_key(t):
        return pl.multiple_of(first_block(t) * sub, sub)

    def scores(t):
        kb = k_ref[pl.ds(first_key(t), n_sub * sub), :]
        return jnp.dot(kb, q4x(t), preferred_element_type=F32)

    def sub_block(z, c, mask):
        nabs = pltpu.bitcast(pltpu.bitcast(z, jnp.uint32) | jnp.uint32(0x80000000), F32)
        sp = jnp.log(1.0 + jnp.exp2(nabs)) * LOG2E
        ls = jnp.minimum(z, 0.0) - sp
        lk = ls - z
        if mask is not None:
            lk = jnp.where(mask, lk, 0.0)
        hi = lk.astype(BF16)
        lo = (lk - hi.astype(F32)).astype(BF16)
        after = jnp.dot(later2, jnp.concatenate([hi, lo], axis=0), preferred_element_type=F32)
        w = jnp.exp2(ls + after + c)
        if mask is not None:
            w = jnp.where(mask, w, 0.0)
        return w.astype(BF16), c + after[0:1, :] + lk[0:1, :]

    def first_step(first_sb):
        zero_c = jnp.zeros((1, gw), F32)
        lo_rows = slice(first_sb * sub, (first_sb + 1) * sub)
        hi_rows = slice((first_sb + 1) * sub, (first_sb + 2) * sub)
        wb, cb = sub_block(z_ref[hi_rows, gw:], zero_c, tri)
        w_ref[hi_rows, gw:] = wb
        wa, ca = sub_block(z_ref[lo_rows, :gw], zero_c, tri)
        wb, cb = sub_block(z_ref[lo_rows, gw:], cb, None)
        w_ref[lo_rows, :gw] = wa
        w_ref[lo_rows, gw:] = wb
        c = jnp.concatenate([ca, cb], axis=1)
        for sb in reversed(range(first_sb)):
            w, c = sub_block(z_ref[sb * sub:(sb + 1) * sub, :], c, None)
            w_ref[sb * sub:(sb + 1) * sub, :] = w
        c_ref[...] = c
        return jnp.max(c)

    def extra_steps(t, c_max):
        def cond(carry):
            j, c_max = carry
            return (j > 0) & (c_max > F32_EXP2_UNDERFLOW)

        def body(carry):
            j, _ = carry
            j = j - SB_NEXT_SUBS
            start = pl.multiple_of(j * sub, sub)
            kb = k_ref[pl.ds(start, SB_NEXT_SUBS * sub), :]
            z_all = jnp.dot(kb, q4x(t), preferred_element_type=F32)
            c = c_ref[...]
            ws = [None] * SB_NEXT_SUBS
            for sb in reversed(range(SB_NEXT_SUBS)):
                ws[sb], c = sub_block(z_all[sb * sub:(sb + 1) * sub], c, None)
            vtb = vt_ref[0, :, pl.ds(start, SB_NEXT_SUBS * sub)]
            accx_ref[...] += jnp.dot(vtb, jnp.concatenate(ws, axis=0), preferred_element_type=F32)
            c_ref[...] = c
            return j, jnp.max(c)

        lax.while_loop(cond, body, (first_block(t), c_max))

    def finish(t):
        vtb = vt_ref[0, :, pl.ds(first_key(t), n_sub * sub)]
        acc = jnp.dot(vtb, w_ref[...], preferred_element_type=F32) + accx_ref[...]
        row_o = lax.broadcasted_iota(jnp.int32, (LANES, half), 0)
        out_t = jnp.concatenate(
            [jnp.where(row_o < SB_HEAD_DIM, acc[:, g * gw:g * gw + half],
                       acc[:, g * gw + half:(g + 1) * gw]) for g in range(2)], axis=1)
        o_ref[pl.ds(pl.multiple_of(t * tq, tq), tq), :] = out_t.T.astype(o_ref.dtype)

    w_ref[...] = jnp.zeros_like(w_ref)
    accx_ref[...] = jnp.zeros_like(accx_ref)
    z_ref[...] = scores(0)
    first_step(0)
    z_ref[...] = scores(1)

    def body(t, carry):
        finish(t - 1)
        z_next = scores(jnp.minimum(t + 1, n_tiles - 1))
        accx_ref[...] = jnp.zeros_like(accx_ref)
        c_max = first_step(SB_PRE_SUBS)
        z_ref[...] = z_next
        extra_steps(t, c_max)
        return carry

    lax.fori_loop(1, n_tiles, body, 0)
    finish(n_tiles - 1)


def _sb_attention(qt, k, vt, batch, seq):
    n, d = k.shape
    tq = SB_Q_TILE
    rows = SB_PRE_SUBS * SB_SUB + tq
    assert seq >= rows and seq // tq >= 2
    return pl.pallas_call(
        _sb_attn_kernel,
        grid=(batch, d // LANES),
        in_specs=[pl.BlockSpec((1, LANES, seq), lambda b, p: (b, p, 0)),
                  pl.BlockSpec((seq, LANES), lambda b, p: (b, p)),
                  pl.BlockSpec((1, LANES, seq), lambda b, p: (b, p, 0))],
        out_specs=pl.BlockSpec((seq, LANES), lambda b, p: (b, p)),
        out_shape=jax.ShapeDtypeStruct((n, d), BF16),
        scratch_shapes=[pltpu.VMEM((rows, 2 * tq), F32), pltpu.VMEM((rows, 2 * tq), BF16),
                        pltpu.VMEM((1, 2 * tq), F32), pltpu.VMEM((LANES, 2 * tq), F32)],
        compiler_params=pltpu.CompilerParams(dimension_semantics=("arbitrary",) * 2,
                                             vmem_limit_bytes=VMEM_LIMIT_BYTES),
        name="sb_attention",
    )(qt, k, vt)


def _diff_attn_kernel(qt_ref, k_ref, vt_ref, lq1_ref, lk1_ref, lq2_ref, lk2_ref, sub_ref, o_ref,
                      m_ref, l_ref, acc_ref, alpha_ref, s_ref, p_ref, smax_ref, *, lam_init):
    t = DIFF_TILE
    n_tiles = k_ref.shape[0] // t

    def q2x(i):
        qt = qt_ref[0, :, pl.ds(pl.multiple_of(i * t, t), t)]
        row = lax.broadcasted_iota(jnp.int32, qt.shape, 0)
        zero = jnp.zeros_like(qt)
        return jnp.concatenate([jnp.where(row < DIFF_HEAD_DIM, qt, zero),
                                jnp.where(row >= DIFF_HEAD_DIM, qt, zero)], axis=1)

    def scores(i, j):
        kb = k_ref[pl.ds(pl.multiple_of(j * t, t), t), :]
        return jnp.dot(kb, q2x(i), preferred_element_type=F32)

    def stage_scores(s):
        s_ref[...] = s
        smax_ref[...] = jnp.max(s.reshape(t // 8, 8, 2 * t), axis=0)

    def softmax_first(i):
        s = s_ref[...]
        key_chunk = lax.broadcasted_iota(jnp.int32, s.shape, 0) // CHUNK
        q_chunk = (lax.broadcasted_iota(jnp.int32, s.shape, 1) % t) // CHUNK
        s = jnp.where(key_chunk <= q_chunk, s, NEG_INF)
        m_new = jnp.max(s, axis=0, keepdims=True)
        p = jnp.exp2(s - m_new)
        l_ref[i] = jnp.sum(p, axis=0, keepdims=True)
        p_ref[...] = p.astype(BF16)
        m_ref[i] = m_new

    def softmax_next(i):
        s = s_ref[...]
        m_old = m_ref[i]
        m_new = jnp.maximum(m_old, jnp.max(smax_ref[...], axis=0, keepdims=True))
        p = jnp.exp2(s - m_new)
        alpha = jnp.exp2(m_old - m_new)
        l_ref[i] = alpha * l_ref[i] + jnp.sum(p, axis=0, keepdims=True)
        p_ref[...] = p.astype(BF16)
        alpha_ref[...] = alpha
        m_ref[i] = m_new

    def pv(j):
        vtb = vt_ref[0, :, pl.ds(pl.multiple_of(j * t, t), t)]
        return jnp.dot(vtb, p_ref[...], preferred_element_type=F32)

    s_ref[...] = scores(0, 0)
    softmax_first(0)
    s_ref[...] = scores(1, 1)

    def diag_body(i, carry):
        acc_ref[i - 1] = pv(i - 1)
        nxt = jnp.minimum(i + 1, n_tiles - 1)
        s_next = scores(nxt, nxt)
        softmax_first(i)
        s_ref[...] = s_next
        return carry

    lax.fori_loop(1, n_tiles, diag_body, 0)
    acc_ref[n_tiles - 1] = pv(n_tiles - 1)

    def next_pair(i, j):
        wrap = j + 1 >= i
        return jnp.where(wrap, i + 1, i), jnp.where(wrap, 0, j + 1)

    stage_scores(scores(1, 0))
    softmax_next(1)
    i1, j1 = next_pair(1, 0)
    i1c = jnp.minimum(i1, n_tiles - 1)
    stage_scores(scores(i1c, jnp.minimum(j1, i1c - 1)))
    n_pairs = n_tiles * (n_tiles - 1) // 2

    def pair_body(n, carry):
        ip, jp, i, j = carry
        acc_ref[ip] = alpha_ref[...] * acc_ref[ip] + pv(jp)
        i2, j2 = next_pair(i, j)
        i2c = jnp.minimum(i2, n_tiles - 1)
        s_next = scores(i2c, jnp.minimum(j2, i2c - 1))
        softmax_next(i)
        stage_scores(s_next)
        return i, j, i2, j2

    ip, jp, _, _ = lax.fori_loop(1, n_pairs, pair_body, (1, 0, i1, j1))
    acc_ref[ip] = alpha_ref[...] * acc_ref[ip] + pv(jp)

    lam = (jnp.exp(jnp.sum(lq1_ref[...] * lk1_ref[...], axis=-1, keepdims=True))
           - jnp.exp(jnp.sum(lq2_ref[...] * lk2_ref[...], axis=-1, keepdims=True)) + lam_init)

    def out_body(i, carry):
        on = acc_ref[i] / l_ref[i]
        o = on[:, :t] - lam * on[:, t:]
        ms = jnp.mean(o * o, axis=0, keepdims=True)
        o = o * lax.rsqrt(ms + EPS)
        o = o.T * sub_ref[...] * (1.0 - lam_init)
        o_ref[pl.ds(pl.multiple_of(i * t, t), t), :] = o.astype(o_ref.dtype)
        return carry

    lax.fori_loop(0, n_tiles, out_body, 0)


def _diff_attention(qt, k, vt, lq1, lk1, lq2, lk2, sub_g, batch, seq, lam_init):
    n, d = k.shape
    t = DIFF_TILE
    n_tiles = seq // t
    assert n_tiles >= 2
    small = _const_spec((1, DIFF_HEAD_DIM))
    return pl.pallas_call(
        functools.partial(_diff_attn_kernel, lam_init=lam_init),
        grid=(batch, d // LANES),
        in_specs=[pl.BlockSpec((1, LANES, seq), lambda b, h: (b, h, 0)),
                  pl.BlockSpec((seq, LANES), lambda b, h: (b, h)),
                  pl.BlockSpec((1, LANES, seq), lambda b, h: (b, h, 0)),
                  small, small, small, small, _const_spec((1, LANES))],
        out_specs=pl.BlockSpec((seq, LANES), lambda b, h: (b, h)),
        out_shape=jax.ShapeDtypeStruct((n, d), BF16),
        scratch_shapes=[pltpu.VMEM((n_tiles, 1, 2 * t), F32), pltpu.VMEM((n_tiles, 1, 2 * t), F32),
                        pltpu.VMEM((n_tiles, LANES, 2 * t), F32), pltpu.VMEM((1, 2 * t), F32),
                        pltpu.VMEM((t, 2 * t), F32), pltpu.VMEM((t, 2 * t), BF16),
                        pltpu.VMEM((8, 2 * t), F32)],
        compiler_params=pltpu.CompilerParams(dimension_semantics=("arbitrary",) * 2,
                                             vmem_limit_bytes=VMEM_LIMIT_BYTES),
        name="diff_attention",
    )(qt, k, vt, lq1, lk1, lq2, lk2, sub_g)


def _mix_ffn_kernel(x_ref, o_ref, wo_ref, g_ref, wg_ref, wu_ref, wd_ref, out_ref):
    x1 = x_ref[...] + jnp.dot(o_ref[...], wo_ref[...], preferred_element_type=F32)
    h = _rms(x1, g_ref[...]).astype(BF16)
    acc = x1
    for lo, hi in FFN_CHUNKS:
        g = jnp.dot(h, wg_ref[:, lo:hi], preferred_element_type=F32)
        u = jnp.dot(h, wu_ref[:, lo:hi], preferred_element_type=F32)
        a = (g * (1.0 / (1.0 + jnp.exp(-g))) * u).astype(BF16)
        acc = acc + jnp.dot(a, wd_ref[lo:hi, :], preferred_element_type=F32)
    out_ref[...] = acc


def _mix_ffn(x2, o, wo, gain, wg, wu, wd):
    n, d = x2.shape
    tm = TOKEN_TILE
    assert FFN_CHUNKS[-1][1] == wg.shape[1]
    row_spec = pl.BlockSpec((tm, d), lambda i: (i, 0))
    return pl.pallas_call(
        _mix_ffn_kernel,
        grid=(n // tm,),
        in_specs=[row_spec, row_spec, _const_spec(wo.shape), _const_spec((1, d)),
                  _const_spec(wg.shape), _const_spec(wu.shape), _const_spec(wd.shape)],
        out_specs=row_spec,
        out_shape=jax.ShapeDtypeStruct((n, d), F32),
        compiler_params=pltpu.CompilerParams(dimension_semantics=("arbitrary",),
                                             vmem_limit_bytes=VMEM_LIMIT_BYTES),
        name="mix_ffn",
    )(x2, o, wo, gain, wg, wu, wd)


def _rope_tables(positions):
    inv_freq = ROPE_THETA ** (-jnp.arange(0, ROT_DIM, 2, dtype=F32) / ROT_DIM)
    ang = positions.astype(F32)[:, None, :] * inv_freq[None, :, None]
    return jnp.cos(ang), jnp.sin(ang)


def kernel(x, positions, attn_norm, w_in, w_out, q_norm, k_norm, lambda_q1, lambda_k1, lambda_q2,
           lambda_k2, sub_norm, ffn_norm, w_gate, w_up, w_down):
    batch, seq, d = x.shape
    depth = w_in.shape[0]
    assert seq % TOKEN_TILE == 0 and seq % SB_Q_TILE == 0 and seq % DIFF_TILE == 0
    assert d % LANES == 0 and DIFF_V_DIM == LANES and DIFF_TILE % CHUNK == 0
    x2 = x.reshape(batch * seq, d)
    tables = _rope_tables(positions)
    for i in range(depth):
        gain = attn_norm[i].reshape(1, d)
        if i % N_MIXERS == 0:
            qt, k, vt = _qkv_proj(x2, gain, w_in[i], batch, seq, SB_HEAD_DIM ** -0.5 * LOG2E)
            o = _sb_attention(qt, k, vt, batch, seq)
        else:
            j = i // N_MIXERS
            qg = jnp.broadcast_to(q_norm[j].reshape(-1, 1), (DIFF_HEAD_DIM, LANES))
            kg = jnp.broadcast_to(k_norm[j].reshape(-1, 1), (DIFF_HEAD_DIM, LANES))
            qt, k, vt = _qkv_proj(x2, gain, w_in[i], batch, seq, DIFF_HEAD_DIM ** -0.5 * LOG2E,
                                  (qg, kg) + tables)
            lam_init = 0.8 - 0.6 * math.exp(-0.3 * i)
            o = _diff_attention(qt, k, vt, lambda_q1[j].reshape(1, -1), lambda_k1[j].reshape(1, -1),
                                lambda_q2[j].reshape(1, -1), lambda_k2[j].reshape(1, -1),
                                sub_norm[j].reshape(1, -1), batch, seq, lam_init)
        x2 = _mix_ffn(x2, o, w_out[i].astype(BF16), ffn_norm[i].reshape(1, d),
                      w_gate[i].astype(BF16), w_up[i].astype(BF16), w_down[i].astype(BF16))
    return x2.reshape(batch, seq, d)
```

```python
import functools
import math

import jax
import jax.numpy as jnp
from jax import lax
from jax.experimental import pallas as pl
from jax.experimental.pallas import tpu as pltpu

F32 = jnp.float32
BF16 = jnp.bfloat16

N_MIXERS = 2
SB_HEAD_DIM = 64
DIFF_HEAD_DIM = 64
DIFF_V_DIM = 2 * DIFF_HEAD_DIM
CHUNK = 64
ROPE_THETA = 500000.0
ROT_DIM = DIFF_HEAD_DIM // 4
EPS = 1e-6
NEG_INF = -1e30

LANES = 128
VMEM_LIMIT_BYTES = 56 * 1024 * 1024

TOKEN_TILE = 512
SB_Q_TILE = 256
SB_SUB = 128
SB_PRE_SUBS = 2
SB_NEXT_SUBS = 2
DIFF_TILE = 512
FFN_CHUNKS = ((0, 1024), (1024, 2048), (2048, 2816))

LOG2E = 1.4426950408889634
F32_EXP2_UNDERFLOW = -150.0


def _rms(x, gain):
    ms = jnp.mean(x * x, axis=-1, keepdims=True)
    return x * lax.rsqrt(ms + EPS) * gain


def _const_spec(shape):
    nd = len(shape)
    return pl.BlockSpec(shape, lambda *_: (0,) * nd, pipeline_mode=pl.Buffered(1))


def _qkv_kernel(x_ref, g_ref, wk_ref, wqvt_ref, qt_ref, k_ref, vt_ref, *, d_model, scale):
    h = _rms(x_ref[...], g_ref[...]).astype(BF16)
    k_ref[...] = jnp.dot(h, wk_ref[...], preferred_element_type=F32).astype(BF16)
    qvt = lax.dot_general(wqvt_ref[...], h, (((1,), (1,)), ((), ())), preferred_element_type=F32)
    qt_ref[0] = (qvt[:d_model, :] * scale).astype(BF16)
    vt_ref[0] = qvt[d_model:, :].astype(BF16)


def _qkv_diff_kernel(x_ref, g_ref, wt_ref, qg_ref, kg_ref, cost_ref, sint_ref,
                     qt_ref, k_ref, vt_ref, *, d_model, scale):
    h = _rms(x_ref[...], g_ref[...]).astype(BF16)
    tm = h.shape[0]
    cos_t, sin_t = cost_ref[0], sint_ref[0]
    half = ROT_DIM // 2

    def project(part):
        w = wt_ref[part * d_model:(part + 1) * d_model, :]
        return lax.dot_general(w, h, (((1,), (1,)), ((), ())), preferred_element_type=F32)

    def norm_rope(xg, gain):
        ms = jnp.mean(xg * xg, axis=0, keepdims=True)
        y = xg * lax.rsqrt(ms + EPS) * gain
        x1, x2 = y[:half], y[half:ROT_DIM]
        return jnp.concatenate([x1 * cos_t - x2 * sin_t, x2 * cos_t + x1 * sin_t, y[ROT_DIM:]],
                               axis=0)

    def slabs(xt, gain):
        per_slab = LANES // DIFF_HEAD_DIM
        for c in range(d_model // LANES):
            parts = [norm_rope(xt[g * DIFF_HEAD_DIM:(g + 1) * DIFF_HEAD_DIM, :], gain)
                     for g in range(c * per_slab, (c + 1) * per_slab)]
            yield slice(c * LANES, (c + 1) * LANES), jnp.concatenate(parts, axis=0)

    for sl, y in slabs(project(1), jnp.tile(kg_ref[...], (1, tm // LANES))):
        k_ref[:, sl] = y.T.astype(BF16)
    for sl, y in slabs(project(0), jnp.tile(qg_ref[...], (1, tm // LANES))):
        qt_ref[0, sl, :] = (y * scale).astype(BF16)
    vt_ref[0] = project(2).astype(BF16)


def _qkv_proj(x2, gain, w_in, batch, seq, scale, diff_args=None):
    n, d = x2.shape
    tm = TOKEN_TILE
    tiles_per_seq = seq // tm
    row_spec = pl.BlockSpec((tm, d), lambda i: (i, 0))
    t_spec = pl.BlockSpec((1, d, tm), lambda i: (i // tiles_per_seq, 0, i % tiles_per_seq))
    row_shape = jax.ShapeDtypeStruct((n, d), BF16)
    t_shape = jax.ShapeDtypeStruct((batch, d, seq), BF16)
    if diff_args is None:
        wk = w_in[:, d:2 * d].astype(BF16)
        wqvt = jnp.concatenate([w_in[:, :d], w_in[:, 2 * d:]], axis=1).T.astype(BF16)
        body = functools.partial(_qkv_kernel, d_model=d, scale=scale)
        name = "qkv_sb"
        in_specs = [row_spec, _const_spec((1, d)), _const_spec(wk.shape), _const_spec(wqvt.shape)]
        args = [x2, gain, wk, wqvt]
    else:
        qg, kg, cos_t, sin_t = diff_args
        wt = w_in.T.astype(BF16)
        tabt_spec = pl.BlockSpec((1, ROT_DIM // 2, tm),
                                 lambda i: (i // tiles_per_seq, 0, i % tiles_per_seq))
        body = functools.partial(_qkv_diff_kernel, d_model=d, scale=scale)
        name = "qkv_diff"
        in_specs = [row_spec, _const_spec((1, d)), _const_spec(wt.shape),
                    _const_spec(qg.shape), _const_spec(kg.shape), tabt_spec, tabt_spec]
        args = [x2, gain, wt, qg, kg, cos_t, sin_t]
    return pl.pallas_call(
        body,
        grid=(n // tm,),
        in_specs=in_specs,
        out_specs=[t_spec, row_spec, t_spec],
        out_shape=[t_shape, row_shape, t_shape],
        compiler_params=pltpu.CompilerParams(dimension_semantics=("arbitrary",),
                                             vmem_limit_bytes=VMEM_LIMIT_BYTES),
        name=name,
    )(*args)


def _sb_attn_kernel(qt_ref, k_ref, vt_ref, o_ref, z_ref, w_ref, c_ref, accx_ref, call_ref):
    tq, sub = SB_Q_TILE, SB_SUB
    half = tq // 2
    gw = 2 * half
    n_tiles = k_ref.shape[0] // tq
    n_sub = SB_PRE_SUBS + tq // sub
    r_u = lax.broadcasted_iota(jnp.int32, (sub, 2 * sub), 0)
    c_u = lax.broadcasted_iota(jnp.int32, (sub, 2 * sub), 1) % sub
    later2 = jnp.where(c_u > r_u, 1.0, 0.0).astype(BF16)
    tri = (lax.broadcasted_iota(jnp.int32, (sub, gw), 0)
           < lax.broadcasted_iota(jnp.int32, (sub, gw), 1) % half)

    def q4x(t):
        qt = qt_ref[0, :, pl.ds(pl.multiple_of(t * tq, tq), tq)]
        row = lax.broadcasted_iota(jnp.int32, (LANES, half), 0)
        zero = jnp.zeros((LANES, half), qt.dtype)
        parts = []
        for g in range(2):
            qg = qt[:, g * half:(g + 1) * half]
            parts += [jnp.where(row < SB_HEAD_DIM, qg, zero), jnp.where(row >= SB_HEAD_DIM, qg, zero)]
        return jnp.concatenate(parts, axis=1)

    def first_block(t):
        return jnp.maximum(t * (tq // sub) - SB_PRE_SUBS, 0)

    def first_key(t):
        return pl.multiple_of(first_block(t) * sub, sub)

    def scores(t):
        kb = k_ref[pl.ds(first_key(t), n_sub * sub), :]
        return jnp.dot(kb, q4x(t), preferred_element_type=F32)

    def sub_block(z, c, mask):
        nabs = pltpu.bitcast(pltpu.bitcast(z, jnp.uint32) | jnp.uint32(0x80000000), F32)
        sp = jnp.log(1.0 + jnp.exp2(nabs)) * LOG2E
        ls = jnp.minimum(z, 0.0) - sp
        lk = ls - z
        if mask is not None:
            lk = jnp.where(mask, lk, 0.0)
        hi = lk.astype(BF16)
        lo = (lk - hi.astype(F32)).astype(BF16)
        after = jnp.dot(later2, jnp.concatenate([hi, lo], axis=0), preferred_element_type=F32)
        w = jnp.exp2(ls + after + c)
        if mask is not None:
            w = jnp.where(mask, w, 0.0)
        return w.astype(BF16), c + after[0:1, :] + lk[0:1, :]

    def first_step(first_sb):
        zero_c = jnp.zeros((1, gw), F32)
        lo_rows = slice(first_sb * sub, (first_sb + 1) * sub)
        hi_rows = slice((first_sb + 1) * sub, (first_sb + 2) * sub)
        wb, cb = sub_block(z_ref[hi_rows, gw:], zero_c, tri)
        w_ref[hi_rows, gw:] = wb
        wa, ca = sub_block(z_ref[lo_rows, :gw], zero_c, tri)
        wb, cb = sub_block(z_ref[lo_rows, gw:], cb, None)
        w_ref[lo_rows, :gw] = wa
        w_ref[lo_rows, gw:] = wb
        c = jnp.concatenate([ca, cb], axis=1)
        for sb in reversed(range(first_sb)):
            w, c = sub_block(z_ref[sb * sub:(sb + 1) * sub, :], c, None)
            w_ref[sb * sub:(sb + 1) * sub, :] = w
        c_ref[...] = c
        return jnp.max(c)

    def extra_steps(t, c_max):
        def cond(carry):
            j, c_max = carry
            return (j > 0) & (c_max > F32_EXP2_UNDERFLOW)

        def body(carry):
            j, _ = carry
            j = j - SB_NEXT_SUBS
            start = pl.multiple_of(j * sub, sub)
            kb = k_ref[pl.ds(start, SB_NEXT_SUBS * sub), :]
            z_all = jnp.dot(kb, q4x(t), preferred_element_type=F32)
            c = c_ref[...]
            ws = [None] * SB_NEXT_SUBS
            for sb in reversed(range(SB_NEXT_SUBS)):
                ws[sb], c = sub_block(z_all[sb * sub:(sb + 1) * sub], c, None)
            vtb = vt_ref[0, :, pl.ds(start, SB_NEXT_SUBS * sub)]
            accx_ref[...] += jnp.dot(vtb, jnp.concatenate(ws, axis=0), preferred_element_type=F32)
            c_ref[...] = c
            return j, jnp.max(c)

        lax.while_loop(cond, body, (first_block(t), c_max))

    def finish(t, with_extra):
        vtb = vt_ref[0, :, pl.ds(first_key(t), n_sub * sub)]
        acc = jnp.dot(vtb, w_ref[...], preferred_element_type=F32)
        if with_extra:
            acc = acc + accx_ref[...]
        row_o = lax.broadcasted_iota(jnp.int32, (LANES, half), 0)
        out_t = jnp.concatenate(
            [jnp.where(row_o < SB_HEAD_DIM, acc[:, g * gw:g * gw + half],
                       acc[:, g * gw + half:(g + 1) * gw]) for g in range(2)], axis=1)
        o_ref[pl.ds(pl.multiple_of(t * tq, tq), tq), :] = out_t.T.astype(o_ref.dtype)

    w_ref[...] = jnp.zeros_like(w_ref)
    z_ref[...] = scores(0)
    first_step(0)
    z_ref[...] = scores(1)

    def body(t, carry):
        finish(t - 1, False)
        z_next = scores(jnp.minimum(t + 1, n_tiles - 1))
        first_step(SB_PRE_SUBS)
        call_ref[t] = c_ref[...]
        z_ref[...] = z_next
        return carry

    lax.fori_loop(1, n_tiles, body, 0)
    finish(n_tiles - 1, False)

    @pl.when(jnp.max(call_ref[1:]) > F32_EXP2_UNDERFLOW)
    def _():
        def redo(t, carry):
            @pl.when(jnp.max(call_ref[t]) > F32_EXP2_UNDERFLOW)
            def _():
                z_ref[...] = scores(t)
                accx_ref[...] = jnp.zeros_like(accx_ref)
                extra_steps(t, first_step(SB_PRE_SUBS))
                finish(t, True)
            return carry

        lax.fori_loop(1, n_tiles, redo, 0)


def _sb_attention(qt, k, vt, batch, seq):
    n, d = k.shape
    tq = SB_Q_TILE
    rows = SB_PRE_SUBS * SB_SUB + tq
    assert seq >= rows and seq // tq >= 2
    return pl.pallas_call(
        _sb_attn_kernel,
        grid=(batch, d // LANES),
        in_specs=[pl.BlockSpec((1, LANES, seq), lambda b, p: (b, p, 0)),
                  pl.BlockSpec((seq, LANES), lambda b, p: (b, p)),
                  pl.BlockSpec((1, LANES, seq), lambda b, p: (b, p, 0))],
        out_specs=pl.BlockSpec((seq, LANES), lambda b, p: (b, p)),
        out_shape=jax.ShapeDtypeStruct((n, d), BF16),
        scratch_shapes=[pltpu.VMEM((rows, 2 * tq), F32), pltpu.VMEM((rows, 2 * tq), BF16),
                        pltpu.VMEM((1, 2 * tq), F32), pltpu.VMEM((LANES, 2 * tq), F32),
                        pltpu.VMEM((seq // tq, 1, 2 * tq), F32)],
        compiler_params=pltpu.CompilerParams(dimension_semantics=("arbitrary",) * 2,
                                             vmem_limit_bytes=VMEM_LIMIT_BYTES),
        name="sb_attention",
    )(qt, k, vt)


def _diff_attn_kernel(qt_ref, k_ref, vt_ref, lq1_ref, lk1_ref, lq2_ref, lk2_ref, sub_ref, o_ref,
                      m_ref, l_ref, acc_ref, alpha_ref, s_ref, p_ref, smax_ref, *, lam_init):
    t = DIFF_TILE
    n_tiles = k_ref.shape[0] // t

    def q2x(i):
        qt = qt_ref[0, :, pl.ds(pl.multiple_of(i * t, t), t)]
        row = lax.broadcasted_iota(jnp.int32, qt.shape, 0)
        zero = jnp.zeros_like(qt)
        return jnp.concatenate([jnp.where(row < DIFF_HEAD_DIM, qt, zero),
                                jnp.where(row >= DIFF_HEAD_DIM, qt, zero)], axis=1)

    def scores(i, j):
        kb = k_ref[pl.ds(pl.multiple_of(j * t, t), t), :]
        return jnp.dot(kb, q2x(i), preferred_element_type=F32)

    def stage_scores(s):
        s_ref[...] = s
        smax_ref[...] = jnp.max(s.reshape(t // 8, 8, 2 * t), axis=0)

    def softmax_first(i):
        s = s_ref[...]
        key_chunk = lax.broadcasted_iota(jnp.int32, s.shape, 0) // CHUNK
        q_chunk = (lax.broadcasted_iota(jnp.int32, s.shape, 1) % t) // CHUNK
        s = jnp.where(key_chunk <= q_chunk, s, NEG_INF)
        m_new = jnp.max(s, axis=0, keepdims=True)
        p = jnp.exp2(s - m_new)
        l_ref[i] = jnp.sum(p, axis=0, keepdims=True)
        p_ref[...] = p.astype(BF16)
        m_ref[i] = m_new

    def softmax_next(i):
        s = s_ref[...]
        m_old = m_ref[i]
        m_new = jnp.maximum(m_old, jnp.max(smax_ref[...], axis=0, keepdims=True))
        p = jnp.exp2(s - m_new)
        alpha = jnp.exp2(m_old - m_new)
        l_ref[i] = alpha * l_ref[i] + jnp.sum(p, axis=0, keepdims=True)
        p_ref[...] = p.astype(BF16)
        alpha_ref[...] = alpha
        m_ref[i] = m_new

    def pv(j):
        vtb = vt_ref[0, :, pl.ds(pl.multiple_of(j * t, t), t)]
        return jnp.dot(vtb, p_ref[...], preferred_element_type=F32)

    s_ref[...] = scores(0, 0)
    softmax_first(0)
    s_ref[...] = scores(1, 1)

    def diag_body(i, carry):
        acc_ref[i - 1] = pv(i - 1)
        nxt = jnp.minimum(i + 1, n_tiles - 1)
        s_next = scores(nxt, nxt)
        softmax_first(i)
        s_ref[...] = s_next
        return carry

    lax.fori_loop(1, n_tiles, diag_body, 0)
    acc_ref[n_tiles - 1] = pv(n_tiles - 1)

    def next_pair(i, j):
        wrap = j + 1 >= i
        return jnp.where(wrap, i + 1, i), jnp.where(wrap, 0, j + 1)

    stage_scores(scores(1, 0))
    softmax_next(1)
    i1, j1 = next_pair(1, 0)
    i1c = jnp.minimum(i1, n_tiles - 1)
    stage_scores(scores(i1c, jnp.minimum(j1, i1c - 1)))
    n_pairs = n_tiles * (n_tiles - 1) // 2

    def pair_body(n, carry):
        ip, jp, i, j = carry
        acc_ref[ip] = alpha_ref[...] * acc_ref[ip] + pv(jp)
        i2, j2 = next_pair(i, j)
        i2c = jnp.minimum(i2, n_tiles - 1)
        s_next = scores(i2c, jnp.minimum(j2, i2c - 1))
        softmax_next(i)
        stage_scores(s_next)
        return i, j, i2, j2

    ip, jp, _, _ = lax.fori_loop(1, n_pairs, pair_body, (1, 0, i1, j1))
    acc_ref[ip] = alpha_ref[...] * acc_ref[ip] + pv(jp)

    lam = (jnp.exp(jnp.sum(lq1_ref[...] * lk1_ref[...], axis=-1, keepdims=True))
           - jnp.exp(jnp.sum(lq2_ref[...] * lk2_ref[...], axis=-1, keepdims=True)) + lam_init)

    def out_body(i, carry):
        on = acc_ref[i] / l_ref[i]
        o = on[:, :t] - lam * on[:, t:]
        ms = jnp.mean(o * o, axis=0, keepdims=True)
        o = o * lax.rsqrt(ms + EPS)
        o = o.T * sub_ref[...] * (1.0 - lam_init)
        o_ref[pl.ds(pl.multiple_of(i * t, t), t), :] = o.astype(o_ref.dtype)
        return carry

    lax.fori_loop(0, n_tiles, out_body, 0)


def _diff_attention(qt, k, vt, lq1, lk1, lq2, lk2, sub_g, batch, seq, lam_init):
    n, d = k.shape
    t = DIFF_TILE
    n_tiles = seq // t
    assert n_tiles >= 2
    small = _const_spec((1, DIFF_HEAD_DIM))
    return pl.pallas_call(
        functools.partial(_diff_attn_kernel, lam_init=lam_init),
        grid=(batch, d // LANES),
        in_specs=[pl.BlockSpec((1, LANES, seq), lambda b, h: (b, h, 0)),
                  pl.BlockSpec((seq, LANES), lambda b, h: (b, h)),
                  pl.BlockSpec((1, LANES, seq), lambda b, h: (b, h, 0)),
                  small, small, small, small, _const_spec((1, LANES))],
        out_specs=pl.BlockSpec((seq, LANES), lambda b, h: (b, h)),
        out_shape=jax.ShapeDtypeStruct((n, d), BF16),
        scratch_shapes=[pltpu.VMEM((n_tiles, 1, 2 * t), F32), pltpu.VMEM((n_tiles, 1, 2 * t), F32),
                        pltpu.VMEM((n_tiles, LANES, 2 * t), F32), pltpu.VMEM((1, 2 * t), F32),
                        pltpu.VMEM((t, 2 * t), F32), pltpu.VMEM((t, 2 * t), BF16),
                        pltpu.VMEM((8, 2 * t), F32)],
        compiler_params=pltpu.CompilerParams(dimension_semantics=("arbitrary",) * 2,
                                             vmem_limit_bytes=VMEM_LIMIT_BYTES),
        name="diff_attention",
    )(qt, k, vt, lq1, lk1, lq2, lk2, sub_g)


def _mix_ffn_kernel(x_ref, o_ref, wo_ref, g_ref, wg_ref, wu_ref, wd_ref, out_ref):
    x1 = x_ref[...] + jnp.dot(o_ref[...], wo_ref[...], preferred_element_type=F32)
    h = _rms(x1, g_ref[...]).astype(BF16)
    acc = x1
    for lo, hi in FFN_CHUNKS:
        g = jnp.dot(h, wg_ref[:, lo:hi], preferred_element_type=F32)
        u = jnp.dot(h, wu_ref[:, lo:hi], preferred_element_type=F32)
        a = (g * (1.0 / (1.0 + jnp.exp(-g))) * u).astype(BF16)
        acc = acc + jnp.dot(a, wd_ref[lo:hi, :], preferred_element_type=F32)
    out_ref[...] = acc


def _mix_ffn(x2, o, wo, gain, wg, wu, wd):
    n, d = x2.shape
    tm = TOKEN_TILE
    assert FFN_CHUNKS[-1][1] == wg.shape[1]
    row_spec = pl.BlockSpec((tm, d), lambda i: (i, 0))
    return pl.pallas_call(
        _mix_ffn_kernel,
        grid=(n // tm,),
        in_specs=[row_spec, row_spec, _const_spec(wo.shape), _const_spec((1, d)),
                  _const_spec(wg.shape), _const_spec(wu.shape), _const_spec(wd.shape)],
        out_specs=row_spec,
        out_shape=jax.ShapeDtypeStruct((n, d), F32),
        compiler_params=pltpu.CompilerParams(dimension_semantics=("arbitrary",),
                                             vmem_limit_bytes=VMEM_LIMIT_BYTES),
        name="mix_ffn",
    )(x2, o, wo, gain, wg, wu, wd)


def _rope_tables(positions):
    inv_freq = ROPE_THETA ** (-jnp.arange(0, ROT_DIM, 2, dtype=F32) / ROT_DIM)
    ang = positions.astype(F32)[:, None, :] * inv_freq[None, :, None]
    return jnp.cos(ang), jnp.sin(ang)


def kernel(x, positions, attn_norm, w_in, w_out, q_norm, k_norm, lambda_q1, lambda_k1, lambda_q2,
           lambda_k2, sub_norm, ffn_norm, w_gate, w_up, w_down):
    batch, seq, d = x.shape
    depth = w_in.shape[0]
    assert seq % TOKEN_TILE == 0 and seq % SB_Q_TILE == 0 and seq % DIFF_TILE == 0
    assert d % LANES == 0 and DIFF_V_DIM == LANES and DIFF_TILE % CHUNK == 0
    x2 = x.reshape(batch * seq, d)
    tables = _rope_tables(positions)
    for i in range(depth):
        gain = attn_norm[i].reshape(1, d)
        if i % N_MIXERS == 0:
            qt, k, vt = _qkv_proj(x2, gain, w_in[i], batch, seq, SB_HEAD_DIM ** -0.5 * LOG2E)
            o = _sb_attention(qt, k, vt, batch, seq)
        else:
            j = i // N_MIXERS
            qg = jnp.broadcast_to(q_norm[j].reshape(-1, 1), (DIFF_HEAD_DIM, LANES))
            kg = jnp.broadcast_to(k_norm[j].reshape(-1, 1), (DIFF_HEAD_DIM, LANES))
            qt, k, vt = _qkv_proj(x2, gain, w_in[i], batch, seq, DIFF_HEAD_DIM ** -0.5 * LOG2E,
                                  (qg, kg) + tables)
            lam_init = 0.8 - 0.6 * math.exp(-0.3 * i)
            o = _diff_attention(qt, k, vt, lambda_q1[j].reshape(1, -1), lambda_k1[j].reshape(1, -1),
                                lambda_q2[j].reshape(1, -1), lambda_k2[j].reshape(1, -1),
                                sub_norm[j].reshape(1, -1), batch, seq, lam_init)
        x2 = _mix_ffn(x2, o, w_out[i].astype(BF16), ffn_norm[i].reshape(1, d),
                      w_gate[i].astype(BF16), w_up[i].astype(BF16), w_down[i].astype(BF16))
    return x2.reshape(batch, seq, d)
```

```python
import functools
import math

import jax
import jax.numpy as jnp
from jax import lax
from jax.experimental import pallas as pl
from jax.experimental.pallas import tpu as pltpu

F32 = jnp.float32
BF16 = jnp.bfloat16

N_MIXERS = 2
SB_HEAD_DIM = 64
DIFF_HEAD_DIM = 64
DIFF_V_DIM = 2 * DIFF_HEAD_DIM
CHUNK = 64
ROPE_THETA = 500000.0
ROT_DIM = DIFF_HEAD_DIM // 4
EPS = 1e-6
NEG_INF = -1e30

LANES = 128
VMEM_LIMIT_BYTES = 56 * 1024 * 1024

TOKEN_TILE = 512
SB_Q_TILE = 256
SB_SUB = 128
SB_PRE_SUBS = 2
SB_NEXT_SUBS = 2
DIFF_TILE = 512
FFN_CHUNKS = ((0, 1024), (1024, 2048), (2048, 2816))

LOG2E = 1.4426950408889634
F32_EXP2_UNDERFLOW = -150.0


def _rms(x, gain):
    ms = jnp.mean(x * x, axis=-1, keepdims=True)
    return x * lax.rsqrt(ms + EPS) * gain


def _const_spec(shape):
    nd = len(shape)
    return pl.BlockSpec(shape, lambda *_: (0,) * nd, pipeline_mode=pl.Buffered(1))


def _qkv_kernel(x_ref, g_ref, wk_ref, wqvt_ref, qt_ref, k_ref, vt_ref, *, d_model, scale):
    h = _rms(x_ref[...], g_ref[...]).astype(BF16)
    k_ref[...] = jnp.dot(h, wk_ref[...], preferred_element_type=F32).astype(BF16)
    qvt = lax.dot_general(wqvt_ref[...], h, (((1,), (1,)), ((), ())), preferred_element_type=F32)
    qt_ref[0] = (qvt[:d_model, :] * scale).astype(BF16)
    vt_ref[0] = qvt[d_model:, :].astype(BF16)


def _qkv_diff_kernel(x_ref, g_ref, wt_ref, qg_ref, kg_ref, cost_ref, sint_ref,
                     qt_ref, k_ref, vt_ref, *, d_model, scale):
    h = _rms(x_ref[...], g_ref[...]).astype(BF16)
    tm = h.shape[0]
    cos_t, sin_t = cost_ref[0], sint_ref[0]
    half = ROT_DIM // 2

    def project(part):
        w = wt_ref[part * d_model:(part + 1) * d_model, :]
        return lax.dot_general(w, h, (((1,), (1,)), ((), ())), preferred_element_type=F32)

    def norm_rope(xg, gain):
        ms = jnp.mean(xg * xg, axis=0, keepdims=True)
        y = xg * lax.rsqrt(ms + EPS) * gain
        x1, x2 = y[:half], y[half:ROT_DIM]
        return jnp.concatenate([x1 * cos_t - x2 * sin_t, x2 * cos_t + x1 * sin_t, y[ROT_DIM:]],
                               axis=0)

    def slabs(xt, gain):
        per_slab = LANES // DIFF_HEAD_DIM
        for c in range(d_model // LANES):
            parts = [norm_rope(xt[g * DIFF_HEAD_DIM:(g + 1) * DIFF_HEAD_DIM, :], gain)
                     for g in range(c * per_slab, (c + 1) * per_slab)]
            yield slice(c * LANES, (c + 1) * LANES), jnp.concatenate(parts, axis=0)

    for sl, y in slabs(project(1), jnp.tile(kg_ref[...], (1, tm // LANES))):
        k_ref[:, sl] = y.T.astype(BF16)
    for sl, y in slabs(project(0), jnp.tile(qg_ref[...], (1, tm // LANES))):
        qt_ref[0, sl, :] = (y * scale).astype(BF16)
    vt_ref[0] = project(2).astype(BF16)


def _qkv_proj(x2, gain, w_in, batch, seq, scale, diff_args=None):
    n, d = x2.shape
    tm = TOKEN_TILE
    tiles_per_seq = seq // tm
    row_spec = pl.BlockSpec((tm, d), lambda i: (i, 0))
    t_spec = pl.BlockSpec((1, d, tm), lambda i: (i // tiles_per_seq, 0, i % tiles_per_seq))
    row_shape = jax.ShapeDtypeStruct((n, d), BF16)
    t_shape = jax.ShapeDtypeStruct((batch, d, seq), BF16)
    if diff_args is None:
        wk = w_in[:, d:2 * d].astype(BF16)
        wqvt = jnp.concatenate([w_in[:, :d], w_in[:, 2 * d:]], axis=1).T.astype(BF16)
        body = functools.partial(_qkv_kernel, d_model=d, scale=scale)
        name = "qkv_sb"
        in_specs = [row_spec, _const_spec((1, d)), _const_spec(wk.shape), _const_spec(wqvt.shape)]
        args = [x2, gain, wk, wqvt]
    else:
        qg, kg, cos_t, sin_t = diff_args
        wt = w_in.T.astype(BF16)
        tabt_spec = pl.BlockSpec((1, ROT_DIM // 2, tm),
                                 lambda i: (i // tiles_per_seq, 0, i % tiles_per_seq))
        body = functools.partial(_qkv_diff_kernel, d_model=d, scale=scale)
        name = "qkv_diff"
        in_specs = [row_spec, _const_spec((1, d)), _const_spec(wt.shape),
                    _const_spec(qg.shape), _const_spec(kg.shape), tabt_spec, tabt_spec]
        args = [x2, gain, wt, qg, kg, cos_t, sin_t]
    return pl.pallas_call(
        body,
        grid=(n // tm,),
        in_specs=in_specs,
        out_specs=[t_spec, row_spec, t_spec],
        out_shape=[t_shape, row_shape, t_shape],
        compiler_params=pltpu.CompilerParams(dimension_semantics=("arbitrary",),
                                             vmem_limit_bytes=VMEM_LIMIT_BYTES),
        name=name,
    )(*args)


def _sb_attn_kernel(qt_ref, k_ref, vt_ref, o_ref, z_ref, w_ref, c_ref, accx_ref, call_ref):
    tq, sub = SB_Q_TILE, SB_SUB
    half = tq // 2
    gw = 2 * half
    n_tiles = k_ref.shape[0] // tq
    n_sub = SB_PRE_SUBS + tq // sub
    r_u = lax.broadcasted_iota(jnp.int32, (sub, 2 * sub), 0)
    c_u = lax.broadcasted_iota(jnp.int32, (sub, 2 * sub), 1) % sub
    later2 = jnp.where(c_u > r_u, 1.0, 0.0).astype(BF16)
    tri = (lax.broadcasted_iota(jnp.int32, (sub, gw), 0)
           < lax.broadcasted_iota(jnp.int32, (sub, gw), 1) % half)

    def q4x(t):
        qt = qt_ref[0, :, pl.ds(pl.multiple_of(t * tq, tq), tq)]
        row = lax.broadcasted_iota(jnp.int32, (LANES, half), 0)
        zero = jnp.zeros((LANES, half), qt.dtype)
        parts = []
        for g in range(2):
            qg = qt[:, g * half:(g + 1) * half]
            parts += [jnp.where(row < SB_HEAD_DIM, qg, zero), jnp.where(row >= SB_HEAD_DIM, qg, zero)]
        return jnp.concatenate(parts, axis=1)

    def first_block(t):
        return jnp.maximum(t * (tq // sub) - SB_PRE_SUBS, 0)

    def first_key(t):
        return pl.multiple_of(first_block(t) * sub, sub)

    def scores(t):
        kb = k_ref[pl.ds(first_key(t), n_sub * sub), :]
        return jnp.dot(kb, q4x(t), preferred_element_type=F32)

    def sub_block(z, c, mask):
        if mask is not None:
            z = jnp.where(mask, z, NEG_INF)
        nabs = pltpu.bitcast(pltpu.bitcast(z, jnp.uint32) | jnp.uint32(0x80000000), F32)
        sp = jnp.log(1.0 + jnp.exp2(nabs)) * LOG2E
        ls = jnp.minimum(z, 0.0) - sp
        lk = ls - z
        hi = lk.astype(BF16)
        lo = (lk - hi.astype(F32)).astype(BF16)
        after = jnp.dot(later2, jnp.concatenate([hi, lo], axis=0), preferred_element_type=F32)
        w = jnp.exp2(ls + after + c)
        return w.astype(BF16), c + after[0:1, :] + lk[0:1, :]

    def first_step(first_sb):
        zero_c = jnp.zeros((1, gw), F32)
        lo_rows = slice(first_sb * sub, (first_sb + 1) * sub)
        hi_rows = slice((first_sb + 1) * sub, (first_sb + 2) * sub)
        wb, cb = sub_block(z_ref[hi_rows, gw:], zero_c, tri)
        w_ref[hi_rows, gw:] = wb
        wa, ca = sub_block(z_ref[lo_rows, :gw], zero_c, tri)
        wb, cb = sub_block(z_ref[lo_rows, gw:], cb, None)
        w_ref[lo_rows, :gw] = wa
        w_ref[lo_rows, gw:] = wb
        c = jnp.concatenate([ca, cb], axis=1)
        for sb in reversed(range(first_sb)):
            w, c = sub_block(z_ref[sb * sub:(sb + 1) * sub, :], c, None)
            w_ref[sb * sub:(sb + 1) * sub, :] = w
        c_ref[...] = c
        return jnp.max(c)

    def extra_steps(t, c_max):
        def cond(carry):
            j, c_max = carry
            return (j > 0) & (c_max > F32_EXP2_UNDERFLOW)

        def body(carry):
            j, _ = carry
            j = j - SB_NEXT_SUBS
            start = pl.multiple_of(j * sub, sub)
            kb = k_ref[pl.ds(start, SB_NEXT_SUBS * sub), :]
            z_all = jnp.dot(kb, q4x(t), preferred_element_type=F32)
            c = c_ref[...]
            ws = [None] * SB_NEXT_SUBS
            for sb in reversed(range(SB_NEXT_SUBS)):
                ws[sb], c = sub_block(z_all[sb * sub:(sb + 1) * sub], c, None)
            vtb = vt_ref[0, :, pl.ds(start, SB_NEXT_SUBS * sub)]
            accx_ref[...] += jnp.dot(vtb, jnp.concatenate(ws, axis=0), preferred_element_type=F32)
            c_ref[...] = c
            return j, jnp.max(c)

        lax.while_loop(cond, body, (first_block(t), c_max))

    def finish(t, with_extra):
        vtb = vt_ref[0, :, pl.ds(first_key(t), n_sub * sub)]
        acc = jnp.dot(vtb, w_ref[...], preferred_element_type=F32)
        if with_extra:
            acc = acc + accx_ref[...]
        row_o = lax.broadcasted_iota(jnp.int32, (LANES, half), 0)
        out_t = jnp.concatenate(
            [jnp.where(row_o < SB_HEAD_DIM, acc[:, g * gw:g * gw + half],
                       acc[:, g * gw + half:(g + 1) * gw]) for g in range(2)], axis=1)
        o_ref[pl.ds(pl.multiple_of(t * tq, tq), tq), :] = out_t.T.astype(o_ref.dtype)

    w_ref[...] = jnp.zeros_like(w_ref)
    z_ref[...] = scores(0)
    first_step(0)
    z_ref[...] = scores(1)

    def body(t, carry):
        finish(t - 1, False)
        z_next = scores(jnp.minimum(t + 1, n_tiles - 1))
        first_step(SB_PRE_SUBS)
        call_ref[t] = c_ref[...]
        z_ref[...] = z_next
        return carry

    lax.fori_loop(1, n_tiles, body, 0)
    finish(n_tiles - 1, False)

    @pl.when(jnp.max(call_ref[1:]) > F32_EXP2_UNDERFLOW)
    def _():
        def redo(t, carry):
            @pl.when(jnp.max(call_ref[t]) > F32_EXP2_UNDERFLOW)
            def _():
                z_ref[...] = scores(t)
                accx_ref[...] = jnp.zeros_like(accx_ref)
                extra_steps(t, first_step(SB_PRE_SUBS))
                finish(t, True)
            return carry

        lax.fori_loop(1, n_tiles, redo, 0)


def _sb_attention(qt, k, vt, batch, seq):
    n, d = k.shape
    tq = SB_Q_TILE
    rows = SB_PRE_SUBS * SB_SUB + tq
    assert seq >= rows and seq // tq >= 2
    return pl.pallas_call(
        _sb_attn_kernel,
        grid=(batch, d // LANES),
        in_specs=[pl.BlockSpec((1, LANES, seq), lambda b, p: (b, p, 0)),
                  pl.BlockSpec((seq, LANES), lambda b, p: (b, p)),
                  pl.BlockSpec((1, LANES, seq), lambda b, p: (b, p, 0))],
        out_specs=pl.BlockSpec((seq, LANES), lambda b, p: (b, p)),
        out_shape=jax.ShapeDtypeStruct((n, d), BF16),
        scratch_shapes=[pltpu.VMEM((rows, 2 * tq), F32), pltpu.VMEM((rows, 2 * tq), BF16),
                        pltpu.VMEM((1, 2 * tq), F32), pltpu.VMEM((LANES, 2 * tq), F32),
                        pltpu.VMEM((seq // tq, 1, 2 * tq), F32)],
        compiler_params=pltpu.CompilerParams(dimension_semantics=("arbitrary",) * 2,
                                             vmem_limit_bytes=VMEM_LIMIT_BYTES),
        name="sb_attention",
    )(qt, k, vt)


def _diff_attn_kernel(qt_ref, k_ref, vt_ref, lq1_ref, lk1_ref, lq2_ref, lk2_ref, sub_ref, o_ref,
                      m_ref, l_ref, acc_ref, alpha_ref, s_ref, p_ref, smax_ref, *, lam_init):
    t = DIFF_TILE
    n_tiles = k_ref.shape[0] // t

    def q2x(i):
        qt = qt_ref[0, :, pl.ds(pl.multiple_of(i * t, t), t)]
        row = lax.broadcasted_iota(jnp.int32, qt.shape, 0)
        zero = jnp.zeros_like(qt)
        return jnp.concatenate([jnp.where(row < DIFF_HEAD_DIM, qt, zero),
                                jnp.where(row >= DIFF_HEAD_DIM, qt, zero)], axis=1)

    def scores(i, j):
        kb = k_ref[pl.ds(pl.multiple_of(j * t, t), t), :]
        return jnp.dot(kb, q2x(i), preferred_element_type=F32)

    def stage_scores(s):
        s_ref[...] = s
        smax_ref[...] = jnp.max(s.reshape(t // 8, 8, 2 * t), axis=0)

    def softmax_first(i):
        s = s_ref[...]
        key_chunk = lax.broadcasted_iota(jnp.int32, s.shape, 0) // CHUNK
        q_chunk = (lax.broadcasted_iota(jnp.int32, s.shape, 1) % t) // CHUNK
        s = jnp.where(key_chunk <= q_chunk, s, NEG_INF)
        m_new = jnp.max(s, axis=0, keepdims=True)
        p = jnp.exp2(s - m_new)
        l_ref[i] = jnp.sum(p, axis=0, keepdims=True)
        p_ref[...] = p.astype(BF16)
        m_ref[i] = m_new

    def softmax_next(i):
        s = s_ref[...]
        m_old = m_ref[i]
        m_new = jnp.maximum(m_old, jnp.max(smax_ref[...], axis=0, keepdims=True))
        p = jnp.exp2(s - m_new)
        alpha = jnp.exp2(m_old - m_new)
        l_ref[i] = alpha * l_ref[i] + jnp.sum(p, axis=0, keepdims=True)
        p_ref[...] = p.astype(BF16)
        alpha_ref[...] = alpha
        m_ref[i] = m_new

    def pv(j):
        vtb = vt_ref[0, :, pl.ds(pl.multiple_of(j * t, t), t)]
        return jnp.dot(vtb, p_ref[...], preferred_element_type=F32)

    s_ref[...] = scores(0, 0)
    softmax_first(0)
    s_ref[...] = scores(1, 1)

    def diag_body(i, carry):
        acc_ref[i - 1] = pv(i - 1)
        nxt = jnp.minimum(i + 1, n_tiles - 1)
        s_next = scores(nxt, nxt)
        softmax_first(i)
        s_ref[...] = s_next
        return carry

    lax.fori_loop(1, n_tiles, diag_body, 0)
    acc_ref[n_tiles - 1] = pv(n_tiles - 1)

    def next_pair(i, j):
        wrap = j + 1 >= i
        return jnp.where(wrap, i + 1, i), jnp.where(wrap, 0, j + 1)

    stage_scores(scores(1, 0))
    softmax_next(1)
    i1, j1 = next_pair(1, 0)
    i1c = jnp.minimum(i1, n_tiles - 1)
    stage_scores(scores(i1c, jnp.minimum(j1, i1c - 1)))
    n_pairs = n_tiles * (n_tiles - 1) // 2

    def pair_body(n, carry):
        ip, jp, i, j = carry
        acc_ref[ip] = alpha_ref[...] * acc_ref[ip] + pv(jp)
        i2, j2 = next_pair(i, j)
        i2c = jnp.minimum(i2, n_tiles - 1)
        s_next = scores(i2c, jnp.minimum(j2, i2c - 1))
        softmax_next(i)
        stage_scores(s_next)
        return i, j, i2, j2

    ip, jp, _, _ = lax.fori_loop(1, n_pairs, pair_body, (1, 0, i1, j1))
    acc_ref[ip] = alpha_ref[...] * acc_ref[ip] + pv(jp)

    lam = (jnp.exp(jnp.sum(lq1_ref[...] * lk1_ref[...], axis=-1, keepdims=True))
           - jnp.exp(jnp.sum(lq2_ref[...] * lk2_ref[...], axis=-1, keepdims=True)) + lam_init)

    def out_body(i, carry):
        on = acc_ref[i] / l_ref[i]
        o = on[:, :t] - lam * on[:, t:]
        ms = jnp.mean(o * o, axis=0, keepdims=True)
        o = o * lax.rsqrt(ms + EPS)
        o = o.T * sub_ref[...] * (1.0 - lam_init)
        o_ref[pl.ds(pl.multiple_of(i * t, t), t), :] = o.astype(o_ref.dtype)
        return carry

    lax.fori_loop(0, n_tiles, out_body, 0)


def _diff_attention(qt, k, vt, lq1, lk1, lq2, lk2, sub_g, batch, seq, lam_init):
    n, d = k.shape
    t = DIFF_TILE
    n_tiles = seq // t
    assert n_tiles >= 2
    small = _const_spec((1, DIFF_HEAD_DIM))
    return pl.pallas_call(
        functools.partial(_diff_attn_kernel, lam_init=lam_init),
        grid=(batch, d // LANES),
        in_specs=[pl.BlockSpec((1, LANES, seq), lambda b, h: (b, h, 0)),
                  pl.BlockSpec((seq, LANES), lambda b, h: (b, h)),
                  pl.BlockSpec((1, LANES, seq), lambda b, h: (b, h, 0)),
                  small, small, small, small, _const_spec((1, LANES))],
        out_specs=pl.BlockSpec((seq, LANES), lambda b, h: (b, h)),
        out_shape=jax.ShapeDtypeStruct((n, d), BF16),
        scratch_shapes=[pltpu.VMEM((n_tiles, 1, 2 * t), F32), pltpu.VMEM((n_tiles, 1, 2 * t), F32),
                        pltpu.VMEM((n_tiles, LANES, 2 * t), F32), pltpu.VMEM((1, 2 * t), F32),
                        pltpu.VMEM((t, 2 * t), F32), pltpu.VMEM((t, 2 * t), BF16),
                        pltpu.VMEM((8, 2 * t), F32)],
        compiler_params=pltpu.CompilerParams(dimension_semantics=("arbitrary",) * 2,
                                             vmem_limit_bytes=VMEM_LIMIT_BYTES),
        name="diff_attention",
    )(qt, k, vt, lq1, lk1, lq2, lk2, sub_g)


def _mix_ffn_kernel(x_ref, o_ref, wo_ref, g_ref, wg_ref, wu_ref, wd_ref, out_ref):
    x1 = x_ref[...] + jnp.dot(o_ref[...], wo_ref[...], preferred_element_type=F32)
    h = _rms(x1, g_ref[...]).astype(BF16)
    acc = x1
    for lo, hi in FFN_CHUNKS:
        g = jnp.dot(h, wg_ref[:, lo:hi], preferred_element_type=F32)
        u = jnp.dot(h, wu_ref[:, lo:hi], preferred_element_type=F32)
        a = (g * (1.0 / (1.0 + jnp.exp(-g))) * u).astype(BF16)
        acc = acc + jnp.dot(a, wd_ref[lo:hi, :], preferred_element_type=F32)
    out_ref[...] = acc


def _mix_ffn(x2, o, wo, gain, wg, wu, wd):
    n, d = x2.shape
    tm = TOKEN_TILE
    assert FFN_CHUNKS[-1][1] == wg.shape[1]
    row_spec = pl.BlockSpec((tm, d), lambda i: (i, 0))
    return pl.pallas_call(
        _mix_ffn_kernel,
        grid=(n // tm,),
        in_specs=[row_spec, row_spec, _const_spec(wo.shape), _const_spec((1, d)),
                  _const_spec(wg.shape), _const_spec(wu.shape), _const_spec(wd.shape)],
        out_specs=row_spec,
        out_shape=jax.ShapeDtypeStruct((n, d), F32),
        compiler_params=pltpu.CompilerParams(dimension_semantics=("arbitrary",),
                                             vmem_limit_bytes=VMEM_LIMIT_BYTES),
        name="mix_ffn",
    )(x2, o, wo, gain, wg, wu, wd)


def _rope_tables(positions):
    inv_freq = ROPE_THETA ** (-jnp.arange(0, ROT_DIM, 2, dtype=F32) / ROT_DIM)
    ang = positions.astype(F32)[:, None, :] * inv_freq[None, :, None]
    return jnp.cos(ang), jnp.sin(ang)


def kernel(x, positions, attn_norm, w_in, w_out, q_norm, k_norm, lambda_q1, lambda_k1, lambda_q2,
           lambda_k2, sub_norm, ffn_norm, w_gate, w_up, w_down):
    batch, seq, d = x.shape
    depth = w_in.shape[0]
    assert seq % TOKEN_TILE == 0 and seq % SB_Q_TILE == 0 and seq % DIFF_TILE == 0
    assert d % LANES == 0 and DIFF_V_DIM == LANES and DIFF_TILE % CHUNK == 0
    x2 = x.reshape(batch * seq, d)
    tables = _rope_tables(positions)
    for i in range(depth):
        gain = attn_norm[i].reshape(1, d)
        if i % N_MIXERS == 0:
            qt, k, vt = _qkv_proj(x2, gain, w_in[i], batch, seq, SB_HEAD_DIM ** -0.5 * LOG2E)
            o = _sb_attention(qt, k, vt, batch, seq)
        else:
            j = i // N_MIXERS
            qg = jnp.broadcast_to(q_norm[j].reshape(-1, 1), (DIFF_HEAD_DIM, LANES))
            kg = jnp.broadcast_to(k_norm[j].reshape(-1, 1), (DIFF_HEAD_DIM, LANES))
            qt, k, vt = _qkv_proj(x2, gain, w_in[i], batch, seq, DIFF_HEAD_DIM ** -0.5 * LOG2E,
                                  (qg, kg) + tables)
            lam_init = 0.8 - 0.6 * math.exp(-0.3 * i)
            o = _diff_attention(qt, k, vt, lambda_q1[j].reshape(1, -1), lambda_k1[j].reshape(1, -1),
                                lambda_q2[j].reshape(1, -1), lambda_k2[j].reshape(1, -1),
                                sub_norm[j].reshape(1, -1), batch, seq, lam_init)
        x2 = _mix_ffn(x2, o, w_out[i].astype(BF16), ffn_norm[i].reshape(1, d),
                      w_gate[i].astype(BF16), w_up[i].astype(BF16), w_down[i].astype(BF16))
    return x2.reshape(batch, seq, d)
```

```python
import functools
import math

import jax
import jax.numpy as jnp
from jax import lax
from jax.experimental import pallas as pl
from jax.experimental.pallas import tpu as pltpu

F32 = jnp.float32
BF16 = jnp.bfloat16

N_MIXERS = 2
SB_HEAD_DIM = 64
DIFF_HEAD_DIM = 64
DIFF_V_DIM = 2 * DIFF_HEAD_DIM
CHUNK = 64
ROPE_THETA = 500000.0
ROT_DIM = DIFF_HEAD_DIM // 4
EPS = 1e-6
NEG_INF = -1e30

LANES = 128
VMEM_LIMIT_BYTES = 56 * 1024 * 1024

TOKEN_TILE = 512
SB_Q_TILE = 256
SB_SUB = 128
SB_PRE_SUBS = 2
SB_NEXT_SUBS = 2
DIFF_TILE = 512
DIFF_OUT_GROUP = 4
FFN_CHUNKS = ((0, 1024), (1024, 2048), (2048, 2816))

LOG2E = 1.4426950408889634
F32_EXP2_UNDERFLOW = -150.0


def _rms(x, gain):
    ms = jnp.mean(x * x, axis=-1, keepdims=True)
    return x * lax.rsqrt(ms + EPS) * gain


def _const_spec(shape):
    nd = len(shape)
    return pl.BlockSpec(shape, lambda *_: (0,) * nd, pipeline_mode=pl.Buffered(1))


def _qkv_kernel(x_ref, g_ref, wk_ref, wqvt_ref, qt_ref, k_ref, vt_ref, *, d_model, scale):
    h = _rms(x_ref[...], g_ref[...]).astype(BF16)
    k_ref[...] = jnp.dot(h, wk_ref[...], preferred_element_type=F32).astype(BF16)
    qvt = lax.dot_general(wqvt_ref[...], h, (((1,), (1,)), ((), ())), preferred_element_type=F32)
    qt_ref[0] = (qvt[:d_model, :] * scale).astype(BF16)
    vt_ref[0] = qvt[d_model:, :].astype(BF16)


def _qkv_diff_kernel(x_ref, g_ref, wt_ref, qg_ref, kg_ref, cost_ref, sint_ref,
                     qt_ref, k_ref, vt_ref, *, d_model, scale):
    h = _rms(x_ref[...], g_ref[...]).astype(BF16)
    tm = h.shape[0]
    cos_t, sin_t = cost_ref[0], sint_ref[0]
    half = ROT_DIM // 2

    def project(part):
        w = wt_ref[part * d_model:(part + 1) * d_model, :]
        return lax.dot_general(w, h, (((1,), (1,)), ((), ())), preferred_element_type=F32)

    def norm_rope(xg, gain):
        ms = jnp.mean(xg * xg, axis=0, keepdims=True)
        y = xg * lax.rsqrt(ms + EPS) * gain
        x1, x2 = y[:half], y[half:ROT_DIM]
        return jnp.concatenate([x1 * cos_t - x2 * sin_t, x2 * cos_t + x1 * sin_t, y[ROT_DIM:]],
                               axis=0)

    def slabs(xt, gain):
        per_slab = LANES // DIFF_HEAD_DIM
        for c in range(d_model // LANES):
            parts = [norm_rope(xt[g * DIFF_HEAD_DIM:(g + 1) * DIFF_HEAD_DIM, :], gain)
                     for g in range(c * per_slab, (c + 1) * per_slab)]
            yield slice(c * LANES, (c + 1) * LANES), jnp.concatenate(parts, axis=0)

    for sl, y in slabs(project(1), jnp.tile(kg_ref[...], (1, tm // LANES))):
        k_ref[:, sl] = y.T.astype(BF16)
    for sl, y in slabs(project(0), jnp.tile(qg_ref[...], (1, tm // LANES))):
        qt_ref[0, sl, :] = (y * scale).astype(BF16)
    vt_ref[0] = project(2).astype(BF16)


def _qkv_proj(x2, gain, w_in, batch, seq, scale, diff_args=None):
    n, d = x2.shape
    tm = TOKEN_TILE
    tiles_per_seq = seq // tm
    row_spec = pl.BlockSpec((tm, d), lambda i: (i, 0))
    t_spec = pl.BlockSpec((1, d, tm), lambda i: (i // tiles_per_seq, 0, i % tiles_per_seq))
    row_shape = jax.ShapeDtypeStruct((n, d), BF16)
    t_shape = jax.ShapeDtypeStruct((batch, d, seq), BF16)
    if diff_args is None:
        wk = w_in[:, d:2 * d].astype(BF16)
        wqvt = jnp.concatenate([w_in[:, :d], w_in[:, 2 * d:]], axis=1).T.astype(BF16)
        body = functools.partial(_qkv_kernel, d_model=d, scale=scale)
        name = "qkv_sb"
        in_specs = [row_spec, _const_spec((1, d)), _const_spec(wk.shape), _const_spec(wqvt.shape)]
        args = [x2, gain, wk, wqvt]
    else:
        qg, kg, cos_t, sin_t = diff_args
        wt = w_in.T.astype(BF16)
        tabt_spec = pl.BlockSpec((1, ROT_DIM // 2, tm),
                                 lambda i: (i // tiles_per_seq, 0, i % tiles_per_seq))
        body = functools.partial(_qkv_diff_kernel, d_model=d, scale=scale)
        name = "qkv_diff"
        in_specs = [row_spec, _const_spec((1, d)), _const_spec(wt.shape),
                    _const_spec(qg.shape), _const_spec(kg.shape), tabt_spec, tabt_spec]
        args = [x2, gain, wt, qg, kg, cos_t, sin_t]
    return pl.pallas_call(
        body,
        grid=(n // tm,),
        in_specs=in_specs,
        out_specs=[t_spec, row_spec, t_spec],
        out_shape=[t_shape, row_shape, t_shape],
        compiler_params=pltpu.CompilerParams(dimension_semantics=("arbitrary",),
                                             vmem_limit_bytes=VMEM_LIMIT_BYTES),
        name=name,
    )(*args)


def _sb_attn_kernel(qt_ref, k_ref, vt_ref, o_ref, z_ref, w_ref, c_ref, accx_ref, call_ref):
    tq, sub = SB_Q_TILE, SB_SUB
    half = tq // 2
    gw = 2 * half
    n_tiles = k_ref.shape[0] // tq
    n_sub = SB_PRE_SUBS + tq // sub
    r_u = lax.broadcasted_iota(jnp.int32, (sub, 2 * sub), 0)
    c_u = lax.broadcasted_iota(jnp.int32, (sub, 2 * sub), 1) % sub
    later2 = jnp.where(c_u > r_u, 1.0, 0.0).astype(BF16)
    tri = (lax.broadcasted_iota(jnp.int32, (sub, gw), 0)
           < lax.broadcasted_iota(jnp.int32, (sub, gw), 1) % half)

    def q4x(t):
        qt = qt_ref[0, :, pl.ds(pl.multiple_of(t * tq, tq), tq)]
        row = lax.broadcasted_iota(jnp.int32, (LANES, half), 0)
        zero = jnp.zeros((LANES, half), qt.dtype)
        parts = []
        for g in range(2):
            qg = qt[:, g * half:(g + 1) * half]
            parts += [jnp.where(row < SB_HEAD_DIM, qg, zero), jnp.where(row >= SB_HEAD_DIM, qg, zero)]
        return jnp.concatenate(parts, axis=1)

    def first_block(t):
        return jnp.maximum(t * (tq // sub) - SB_PRE_SUBS, 0)

    def first_key(t):
        return pl.multiple_of(first_block(t) * sub, sub)

    def scores(t):
        kb = k_ref[pl.ds(first_key(t), n_sub * sub), :]
        return jnp.dot(kb, q4x(t), preferred_element_type=F32)

    def sub_block(z, c, mask):
        if mask is not None:
            z = jnp.where(mask, z, NEG_INF)
        nabs = pltpu.bitcast(pltpu.bitcast(z, jnp.uint32) | jnp.uint32(0x80000000), F32)
        sp = jnp.log(1.0 + jnp.exp2(nabs)) * LOG2E
        ls = jnp.minimum(z, 0.0) - sp
        lk = ls - z
        hi = lk.astype(BF16)
        lo = (lk - hi.astype(F32)).astype(BF16)
        after = jnp.dot(later2, jnp.concatenate([hi, lo], axis=0), preferred_element_type=F32)
        w = jnp.exp2(ls + after + c)
        return w.astype(BF16), c + after[0:1, :] + lk[0:1, :]

    def first_step(first_sb):
        zero_c = jnp.zeros((1, gw), F32)
        lo_rows = slice(first_sb * sub, (first_sb + 1) * sub)
        hi_rows = slice((first_sb + 1) * sub, (first_sb + 2) * sub)
        wb, cb = sub_block(z_ref[hi_rows, gw:], zero_c, tri)
        w_ref[hi_rows, gw:] = wb
        wa, ca = sub_block(z_ref[lo_rows, :gw], zero_c, tri)
        wb, cb = sub_block(z_ref[lo_rows, gw:], cb, None)
        w_ref[lo_rows, :gw] = wa
        w_ref[lo_rows, gw:] = wb
        c = jnp.concatenate([ca, cb], axis=1)
        for sb in reversed(range(first_sb)):
            w, c = sub_block(z_ref[sb * sub:(sb + 1) * sub, :], c, None)
            w_ref[sb * sub:(sb + 1) * sub, :] = w
        c_ref[...] = c
        return jnp.max(c)

    def extra_steps(t, c_max):
        def cond(carry):
            j, c_max = carry
            return (j > 0) & (c_max > F32_EXP2_UNDERFLOW)

        def body(carry):
            j, _ = carry
            j = j - SB_NEXT_SUBS
            start = pl.multiple_of(j * sub, sub)
            kb = k_ref[pl.ds(start, SB_NEXT_SUBS * sub), :]
            z_all = jnp.dot(kb, q4x(t), preferred_element_type=F32)
            c = c_ref[...]
            ws = [None] * SB_NEXT_SUBS
            for sb in reversed(range(SB_NEXT_SUBS)):
                ws[sb], c = sub_block(z_all[sb * sub:(sb + 1) * sub], c, None)
            vtb = vt_ref[0, :, pl.ds(start, SB_NEXT_SUBS * sub)]
            accx_ref[...] += jnp.dot(vtb, jnp.concatenate(ws, axis=0), preferred_element_type=F32)
            c_ref[...] = c
            return j, jnp.max(c)

        lax.while_loop(cond, body, (first_block(t), c_max))

    def finish(t, with_extra):
        vtb = vt_ref[0, :, pl.ds(first_key(t), n_sub * sub)]
        acc = jnp.dot(vtb, w_ref[...], preferred_element_type=F32)
        if with_extra:
            acc = acc + accx_ref[...]
        row_o = lax.broadcasted_iota(jnp.int32, (LANES, half), 0)
        out_t = jnp.concatenate(
            [jnp.where(row_o < SB_HEAD_DIM, acc[:, g * gw:g * gw + half],
                       acc[:, g * gw + half:(g + 1) * gw]) for g in range(2)], axis=1)
        o_ref[pl.ds(pl.multiple_of(t * tq, tq), tq), :] = out_t.T.astype(o_ref.dtype)

    w_ref[...] = jnp.zeros_like(w_ref)
    z_ref[...] = scores(0)
    first_step(0)
    z_ref[...] = scores(1)

    def body(t, carry):
        finish(t - 1, False)
        z_next = scores(jnp.minimum(t + 1, n_tiles - 1))
        first_step(SB_PRE_SUBS)
        call_ref[t] = c_ref[...]
        z_ref[...] = z_next
        return carry

    lax.fori_loop(1, n_tiles, body, 0)
    finish(n_tiles - 1, False)

    @pl.when(jnp.max(call_ref[1:]) > F32_EXP2_UNDERFLOW)
    def _():
        def redo(t, carry):
            @pl.when(jnp.max(call_ref[t]) > F32_EXP2_UNDERFLOW)
            def _():
                z_ref[...] = scores(t)
                accx_ref[...] = jnp.zeros_like(accx_ref)
                extra_steps(t, first_step(SB_PRE_SUBS))
                finish(t, True)
            return carry

        lax.fori_loop(1, n_tiles, redo, 0)


def _sb_attention(qt, k, vt, batch, seq):
    n, d = k.shape
    tq = SB_Q_TILE
    rows = SB_PRE_SUBS * SB_SUB + tq
    assert seq >= rows and seq // tq >= 2
    return pl.pallas_call(
        _sb_attn_kernel,
        grid=(batch, d // LANES),
        in_specs=[pl.BlockSpec((1, LANES, seq), lambda b, p: (b, p, 0)),
                  pl.BlockSpec((seq, LANES), lambda b, p: (b, p)),
                  pl.BlockSpec((1, LANES, seq), lambda b, p: (b, p, 0))],
        out_specs=pl.BlockSpec((seq, LANES), lambda b, p: (b, p)),
        out_shape=jax.ShapeDtypeStruct((n, d), BF16),
        scratch_shapes=[pltpu.VMEM((rows, 2 * tq), F32), pltpu.VMEM((rows, 2 * tq), BF16),
                        pltpu.VMEM((1, 2 * tq), F32), pltpu.VMEM((LANES, 2 * tq), F32),
                        pltpu.VMEM((seq // tq, 1, 2 * tq), F32)],
        compiler_params=pltpu.CompilerParams(dimension_semantics=("arbitrary",) * 2,
                                             vmem_limit_bytes=VMEM_LIMIT_BYTES),
        name="sb_attention",
    )(qt, k, vt)


def _diff_attn_kernel(qt_ref, k_ref, vt_ref, lq1_ref, lk1_ref, lq2_ref, lk2_ref, sub_ref, o_ref,
                      m_ref, l_ref, acc_ref, alpha_ref, s_ref, p_ref, smax_ref, *, lam_init):
    t = DIFF_TILE
    n_tiles = k_ref.shape[0] // t

    def q2x(i):
        qt = qt_ref[0, :, pl.ds(pl.multiple_of(i * t, t), t)]
        row = lax.broadcasted_iota(jnp.int32, qt.shape, 0)
        zero = jnp.zeros_like(qt)
        return jnp.concatenate([jnp.where(row < DIFF_HEAD_DIM, qt, zero),
                                jnp.where(row >= DIFF_HEAD_DIM, qt, zero)], axis=1)

    def scores(i, j):
        kb = k_ref[pl.ds(pl.multiple_of(j * t, t), t), :]
        return jnp.dot(kb, q2x(i), preferred_element_type=F32)

    def stage_scores(s):
        s_ref[...] = s
        smax_ref[...] = jnp.max(s.reshape(t // 8, 8, 2 * t), axis=0)

    def softmax_first(i):
        s = s_ref[...]
        key_chunk = lax.broadcasted_iota(jnp.int32, s.shape, 0) // CHUNK
        q_chunk = (lax.broadcasted_iota(jnp.int32, s.shape, 1) % t) // CHUNK
        s = jnp.where(key_chunk <= q_chunk, s, NEG_INF)
        m_new = jnp.max(s, axis=0, keepdims=True)
        p = jnp.exp2(s - m_new)
        l_ref[i] = jnp.sum(p, axis=0, keepdims=True)
        p_ref[...] = p.astype(BF16)
        m_ref[i] = m_new

    def softmax_next(i):
        s = s_ref[...]
        m_old = m_ref[i]
        m_new = jnp.maximum(m_old, jnp.max(smax_ref[...], axis=0, keepdims=True))
        p = jnp.exp2(s - m_new)
        alpha = jnp.exp2(m_old - m_new)
        l_ref[i] = alpha * l_ref[i] + jnp.sum(p, axis=0, keepdims=True)
        p_ref[...] = p.astype(BF16)
        alpha_ref[...] = alpha
        m_ref[i] = m_new

    def pv(j):
        vtb = vt_ref[0, :, pl.ds(pl.multiple_of(j * t, t), t)]
        return jnp.dot(vtb, p_ref[...], preferred_element_type=F32)

    s_ref[...] = scores(0, 0)
    softmax_first(0)
    s_ref[...] = scores(1, 1)

    def diag_body(i, carry):
        acc_ref[i - 1] = pv(i - 1)
        nxt = jnp.minimum(i + 1, n_tiles - 1)
        s_next = scores(nxt, nxt)
        softmax_first(i)
        s_ref[...] = s_next
        return carry

    lax.fori_loop(1, n_tiles, diag_body, 0)
    acc_ref[n_tiles - 1] = pv(n_tiles - 1)

    def next_pair(i, j):
        wrap = j + 1 >= i
        return jnp.where(wrap, i + 1, i), jnp.where(wrap, 0, j + 1)

    stage_scores(scores(1, 0))
    softmax_next(1)
    i1, j1 = next_pair(1, 0)
    i1c = jnp.minimum(i1, n_tiles - 1)
    stage_scores(scores(i1c, jnp.minimum(j1, i1c - 1)))
    n_pairs = n_tiles * (n_tiles - 1) // 2

    def pair_body(n, carry):
        ip, jp, i, j = carry
        acc_ref[ip] = alpha_ref[...] * acc_ref[ip] + pv(jp)
        i2, j2 = next_pair(i, j)
        i2c = jnp.minimum(i2, n_tiles - 1)
        s_next = scores(i2c, jnp.minimum(j2, i2c - 1))
        softmax_next(i)
        stage_scores(s_next)
        return i, j, i2, j2

    ip, jp, _, _ = lax.fori_loop(1, n_pairs, pair_body, (1, 0, i1, j1))
    acc_ref[ip] = alpha_ref[...] * acc_ref[ip] + pv(jp)

    lam = (jnp.exp(jnp.sum(lq1_ref[...] * lk1_ref[...], axis=-1, keepdims=True))
           - jnp.exp(jnp.sum(lq2_ref[...] * lk2_ref[...], axis=-1, keepdims=True)) + lam_init)

    group = DIFF_OUT_GROUP if n_tiles % DIFF_OUT_GROUP == 0 else 1

    def out_body(ig, carry):
        for u in range(group):
            i = ig * group + u
            on = acc_ref[i] / l_ref[i]
            o = on[:, :t] - lam * on[:, t:]
            ms = jnp.mean(o * o, axis=0, keepdims=True)
            o = o * lax.rsqrt(ms + EPS)
            o = o.T * sub_ref[...] * (1.0 - lam_init)
            o_ref[pl.ds(pl.multiple_of(i * t, t), t), :] = o.astype(o_ref.dtype)
        return carry

    lax.fori_loop(0, n_tiles // group, out_body, 0)


def _diff_attention(qt, k, vt, lq1, lk1, lq2, lk2, sub_g, batch, seq, lam_init):
    n, d = k.shape
    t = DIFF_TILE
    n_tiles = seq // t
    assert n_tiles >= 2
    small = _const_spec((1, DIFF_HEAD_DIM))
    return pl.pallas_call(
        functools.partial(_diff_attn_kernel, lam_init=lam_init),
        grid=(batch, d // LANES),
        in_specs=[pl.BlockSpec((1, LANES, seq), lambda b, h: (b, h, 0)),
                  pl.BlockSpec((seq, LANES), lambda b, h: (b, h)),
                  pl.BlockSpec((1, LANES, seq), lambda b, h: (b, h, 0)),
                  small, small, small, small, _const_spec((1, LANES))],
        out_specs=pl.BlockSpec((seq, LANES), lambda b, h: (b, h)),
        out_shape=jax.ShapeDtypeStruct((n, d), BF16),
        scratch_shapes=[pltpu.VMEM((n_tiles, 1, 2 * t), F32), pltpu.VMEM((n_tiles, 1, 2 * t), F32),
                        pltpu.VMEM((n_tiles, LANES, 2 * t), F32), pltpu.VMEM((1, 2 * t), F32),
                        pltpu.VMEM((t, 2 * t), F32), pltpu.VMEM((t, 2 * t), BF16),
                        pltpu.VMEM((8, 2 * t), F32)],
        compiler_params=pltpu.CompilerParams(dimension_semantics=("arbitrary",) * 2,
                                             vmem_limit_bytes=VMEM_LIMIT_BYTES),
        name="diff_attention",
    )(qt, k, vt, lq1, lk1, lq2, lk2, sub_g)


def _mix_ffn_kernel(x_ref, o_ref, wo_ref, g_ref, wg_ref, wu_ref, wd_ref, out_ref):
    x1 = x_ref[...] + jnp.dot(o_ref[...], wo_ref[...], preferred_element_type=F32)
    h = _rms(x1, g_ref[...]).astype(BF16)
    acc = x1
    for lo, hi in FFN_CHUNKS:
        g = jnp.dot(h, wg_ref[:, lo:hi], preferred_element_type=F32)
        u = jnp.dot(h, wu_ref[:, lo:hi], preferred_element_type=F32)
        a = (g * (1.0 / (1.0 + jnp.exp(-g))) * u).astype(BF16)
        acc = acc + jnp.dot(a, wd_ref[lo:hi, :], preferred_element_type=F32)
    out_ref[...] = acc


def _mix_ffn(x2, o, wo, gain, wg, wu, wd):
    n, d = x2.shape
    tm = TOKEN_TILE
    assert FFN_CHUNKS[-1][1] == wg.shape[1]
    row_spec = pl.BlockSpec((tm, d), lambda i: (i, 0))
    return pl.pallas_call(
        _mix_ffn_kernel,
        grid=(n // tm,),
        in_specs=[row_spec, row_spec, _const_spec(wo.shape), _const_spec((1, d)),
                  _const_spec(wg.shape), _const_spec(wu.shape), _const_spec(wd.shape)],
        out_specs=row_spec,
        out_shape=jax.ShapeDtypeStruct((n, d), F32),
        compiler_params=pltpu.CompilerParams(dimension_semantics=("arbitrary",),
                                             vmem_limit_bytes=VMEM_LIMIT_BYTES),
        name="mix_ffn",
    )(x2, o, wo, gain, wg, wu, wd)


def _rope_tables(positions):
    inv_freq = ROPE_THETA ** (-jnp.arange(0, ROT_DIM, 2, dtype=F32) / ROT_DIM)
    ang = positions.astype(F32)[:, None, :] * inv_freq[None, :, None]
    return jnp.cos(ang), jnp.sin(ang)


def kernel(x, positions, attn_norm, w_in, w_out, q_norm, k_norm, lambda_q1, lambda_k1, lambda_q2,
           lambda_k2, sub_norm, ffn_norm, w_gate, w_up, w_down):
    batch, seq, d = x.shape
    depth = w_in.shape[0]
    assert seq % TOKEN_TILE == 0 and seq % SB_Q_TILE == 0 and seq % DIFF_TILE == 0
    assert d % LANES == 0 and DIFF_V_DIM == LANES and DIFF_TILE % CHUNK == 0
    x2 = x.reshape(batch * seq, d)
    tables = _rope_tables(positions)
    for i in range(depth):
        gain = attn_norm[i].reshape(1, d)
        if i % N_MIXERS == 0:
            qt, k, vt = _qkv_proj(x2, gain, w_in[i], batch, seq, SB_HEAD_DIM ** -0.5 * LOG2E)
            o = _sb_attention(qt, k, vt, batch, seq)
        else:
            j = i // N_MIXERS
            qg = jnp.broadcast_to(q_norm[j].reshape(-1, 1), (DIFF_HEAD_DIM, LANES))
            kg = jnp.broadcast_to(k_norm[j].reshape(-1, 1), (DIFF_HEAD_DIM, LANES))
            qt, k, vt = _qkv_proj(x2, gain, w_in[i], batch, seq, DIFF_HEAD_DIM ** -0.5 * LOG2E,
                                  (qg, kg) + tables)
            lam_init = 0.8 - 0.6 * math.exp(-0.3 * i)
            o = _diff_attention(qt, k, vt, lambda_q1[j].reshape(1, -1), lambda_k1[j].reshape(1, -1),
                                lambda_q2[j].reshape(1, -1), lambda_k2[j].reshape(1, -1),
                                sub_norm[j].reshape(1, -1), batch, seq, lam_init)
        x2 = _mix_ffn(x2, o, w_out[i].astype(BF16), ffn_norm[i].reshape(1, d),
                      w_gate[i].astype(BF16), w_up[i].astype(BF16), w_down[i].astype(BF16))
    return x2.reshape(batch, seq, d)
```

```python
import functools
import math

import jax
import jax.numpy as jnp
from jax import lax
from jax.experimental import pallas as pl
from jax.experimental.pallas import tpu as pltpu

F32 = jnp.float32
BF16 = jnp.bfloat16

N_MIXERS = 2
SB_HEAD_DIM = 64
DIFF_HEAD_DIM = 64
DIFF_V_DIM = 2 * DIFF_HEAD_DIM
CHUNK = 64
ROPE_THETA = 500000.0
ROT_DIM = DIFF_HEAD_DIM // 4
EPS = 1e-6
NEG_INF = -1e30

LANES = 128
VMEM_LIMIT_BYTES = 56 * 1024 * 1024

TOKEN_TILE = 512
SB_Q_TILE = 256
SB_SUB = 128
SB_PRE_SUBS = 2
SB_NEXT_SUBS = 2
DIFF_TILE = 512
DIFF_OUT_GROUP = 4
FFN_CHUNKS = ((0, 1024), (1024, 2048), (2048, 2816))

LOG2E = 1.4426950408889634
F32_EXP2_UNDERFLOW = -150.0


def _rms(x, gain):
    ms = jnp.mean(x * x, axis=-1, keepdims=True)
    return x * lax.rsqrt(ms + EPS) * gain


def _const_spec(shape):
    nd = len(shape)
    return pl.BlockSpec(shape, lambda *_: (0,) * nd, pipeline_mode=pl.Buffered(1))


def _qkv_kernel(x_ref, g_ref, wk_ref, wqvt_ref, qt_ref, k_ref, vt_ref, *, d_model, scale):
    h = _rms(x_ref[...], g_ref[...]).astype(BF16)
    k_ref[...] = jnp.dot(h, wk_ref[...], preferred_element_type=F32).astype(BF16)
    qvt = lax.dot_general(wqvt_ref[...], h, (((1,), (1,)), ((), ())), preferred_element_type=F32)
    qt_ref[0] = (qvt[:d_model, :] * scale).astype(BF16)
    vt_ref[0] = qvt[d_model:, :].astype(BF16)


def _qkv_diff_kernel(x_ref, g_ref, wt_ref, qg_ref, kg_ref, cost_ref, sint_ref,
                     qt_ref, k_ref, vt_ref, *, d_model, scale):
    h = _rms(x_ref[...], g_ref[...]).astype(BF16)
    tm = h.shape[0]
    cos_t, sin_t = cost_ref[0], sint_ref[0]
    half = ROT_DIM // 2

    def project(part):
        w = wt_ref[part * d_model:(part + 1) * d_model, :]
        return lax.dot_general(w, h, (((1,), (1,)), ((), ())), preferred_element_type=F32)

    def norm_rope(xg, gain):
        ms = jnp.mean(xg * xg, axis=0, keepdims=True)
        y = xg * lax.rsqrt(ms + EPS) * gain
        x1, x2 = y[:half], y[half:ROT_DIM]
        return jnp.concatenate([x1 * cos_t - x2 * sin_t, x2 * cos_t + x1 * sin_t, y[ROT_DIM:]],
                               axis=0)

    def slabs(xt, gain):
        per_slab = LANES // DIFF_HEAD_DIM
        for c in range(d_model // LANES):
            parts = [norm_rope(xt[g * DIFF_HEAD_DIM:(g + 1) * DIFF_HEAD_DIM, :], gain)
                     for g in range(c * per_slab, (c + 1) * per_slab)]
            yield slice(c * LANES, (c + 1) * LANES), jnp.concatenate(parts, axis=0)

    for sl, y in slabs(project(1), jnp.tile(kg_ref[...], (1, tm // LANES))):
        k_ref[:, sl] = y.T.astype(BF16)
    for sl, y in slabs(project(0), jnp.tile(qg_ref[...], (1, tm // LANES))):
        qt_ref[0, sl, :] = (y * scale).astype(BF16)
    vt_ref[0] = project(2).astype(BF16)


def _qkv_proj(x2, gain, w_in, batch, seq, scale, diff_args=None):
    n, d = x2.shape
    tm = TOKEN_TILE
    tiles_per_seq = seq // tm
    row_spec = pl.BlockSpec((tm, d), lambda i: (i, 0))
    t_spec = pl.BlockSpec((1, d, tm), lambda i: (i // tiles_per_seq, 0, i % tiles_per_seq))
    row_shape = jax.ShapeDtypeStruct((n, d), BF16)
    t_shape = jax.ShapeDtypeStruct((batch, d, seq), BF16)
    if diff_args is None:
        wk = w_in[:, d:2 * d].astype(BF16)
        wqvt = jnp.concatenate([w_in[:, :d], w_in[:, 2 * d:]], axis=1).T.astype(BF16)
        body = functools.partial(_qkv_kernel, d_model=d, scale=scale)
        name = "qkv_sb"
        in_specs = [row_spec, _const_spec((1, d)), _const_spec(wk.shape), _const_spec(wqvt.shape)]
        args = [x2, gain, wk, wqvt]
    else:
        qg, kg, cos_t, sin_t = diff_args
        wt = w_in.T.astype(BF16)
        tabt_spec = pl.BlockSpec((1, ROT_DIM // 2, tm),
                                 lambda i: (i // tiles_per_seq, 0, i % tiles_per_seq))
        body = functools.partial(_qkv_diff_kernel, d_model=d, scale=scale)
        name = "qkv_diff"
        in_specs = [row_spec, _const_spec((1, d)), _const_spec(wt.shape),
                    _const_spec(qg.shape), _const_spec(kg.shape), tabt_spec, tabt_spec]
        args = [x2, gain, wt, qg, kg, cos_t, sin_t]
    return pl.pallas_call(
        body,
        grid=(n // tm,),
        in_specs=in_specs,
        out_specs=[t_spec, row_spec, t_spec],
        out_shape=[t_shape, row_shape, t_shape],
        compiler_params=pltpu.CompilerParams(dimension_semantics=("arbitrary",),
                                             vmem_limit_bytes=VMEM_LIMIT_BYTES),
        name=name,
    )(*args)


def _sb_attn_kernel(qt_ref, k_ref, vt_ref, o_ref, z_ref, w_ref, w0_ref, c_ref, accx_ref, call_ref):
    tq, sub = SB_Q_TILE, SB_SUB
    half = tq // 2
    gw = 2 * half
    n_tiles = k_ref.shape[0] // tq
    n_sub = SB_PRE_SUBS + tq // sub
    r_u = lax.broadcasted_iota(jnp.int32, (sub, 2 * sub), 0)
    c_u = lax.broadcasted_iota(jnp.int32, (sub, 2 * sub), 1) % sub
    later2 = jnp.where(c_u > r_u, 1.0, 0.0).astype(BF16)
    tri = (lax.broadcasted_iota(jnp.int32, (sub, gw), 0)
           < lax.broadcasted_iota(jnp.int32, (sub, gw), 1) % half)

    def q4x(t):
        qt = qt_ref[0, :, pl.ds(pl.multiple_of(t * tq, tq), tq)]
        row = lax.broadcasted_iota(jnp.int32, (LANES, half), 0)
        zero = jnp.zeros((LANES, half), qt.dtype)
        parts = []
        for g in range(2):
            qg = qt[:, g * half:(g + 1) * half]
            parts += [jnp.where(row < SB_HEAD_DIM, qg, zero), jnp.where(row >= SB_HEAD_DIM, qg, zero)]
        return jnp.concatenate(parts, axis=1)

    def first_block(t):
        return jnp.maximum(t * (tq // sub) - SB_PRE_SUBS, 0)

    def first_key(t):
        return pl.multiple_of(first_block(t) * sub, sub)

    def scores(t):
        kb = k_ref[pl.ds(first_key(t), n_sub * sub), :]
        return jnp.dot(kb, q4x(t), preferred_element_type=F32)

    def sub_block(z, c, mask):
        if mask is not None:
            z = jnp.where(mask, z, NEG_INF)
        nabs = pltpu.bitcast(pltpu.bitcast(z, jnp.uint32) | jnp.uint32(0x80000000), F32)
        sp = jnp.log(1.0 + jnp.exp2(nabs)) * LOG2E
        ls = jnp.minimum(z, 0.0) - sp
        lk = ls - z
        hi = lk.astype(BF16)
        lo = (lk - hi.astype(F32)).astype(BF16)
        after = jnp.dot(later2, jnp.concatenate([hi, lo], axis=0), preferred_element_type=F32)
        w = jnp.exp2(ls + after + c)
        return w.astype(BF16), c + after[0:1, :] + lk[0:1, :]

    def first_step(first_sb, dst_ref, skip_first):
        zero_c = jnp.zeros((1, gw), F32)
        lo_rows = slice(first_sb * sub, (first_sb + 1) * sub)
        hi_rows = slice((first_sb + 1) * sub, (first_sb + 2) * sub)
        wb, cb = sub_block(z_ref[hi_rows, gw:], zero_c, tri)
        dst_ref[hi_rows, gw:] = wb
        wa, ca = sub_block(z_ref[lo_rows, :gw], zero_c, tri)
        wb, cb = sub_block(z_ref[lo_rows, gw:], cb, None)
        dst_ref[lo_rows, :gw] = wa
        dst_ref[lo_rows, gw:] = wb
        c = jnp.concatenate([ca, cb], axis=1)
        for sb in reversed(range(1 if skip_first else 0, first_sb)):
            w, c = sub_block(z_ref[sb * sub:(sb + 1) * sub, :], c, None)
            dst_ref[sb * sub:(sb + 1) * sub, :] = w
        if skip_first:
            wa, ca = sub_block(z_ref[0:sub, :gw], c[:, :gw], None)
            dst_ref[0:sub, :gw] = wa
            c = jnp.concatenate([ca, c[:, gw:]], axis=1)
        c_ref[...] = c
        return jnp.max(c)

    def extra_steps(t, c_max):
        def cond(carry):
            j, c_max = carry
            return (j > 0) & (c_max > F32_EXP2_UNDERFLOW)

        def body(carry):
            j, _ = carry
            j = j - SB_NEXT_SUBS
            start = pl.multiple_of(j * sub, sub)
            kb = k_ref[pl.ds(start, SB_NEXT_SUBS * sub), :]
            z_all = jnp.dot(kb, q4x(t), preferred_element_type=F32)
            c = c_ref[...]
            ws = [None] * SB_NEXT_SUBS
            for sb in reversed(range(SB_NEXT_SUBS)):
                ws[sb], c = sub_block(z_all[sb * sub:(sb + 1) * sub], c, None)
            vtb = vt_ref[0, :, pl.ds(start, SB_NEXT_SUBS * sub)]
            accx_ref[...] += jnp.dot(vtb, jnp.concatenate(ws, axis=0), preferred_element_type=F32)
            c_ref[...] = c
            return j, jnp.max(c)

        lax.while_loop(cond, body, (first_block(t), c_max))

    def write_out(t, acc):
        row_o = lax.broadcasted_iota(jnp.int32, (LANES, half), 0)
        out_t = jnp.concatenate(
            [jnp.where(row_o < SB_HEAD_DIM, acc[:, g * gw:g * gw + half],
                       acc[:, g * gw + half:(g + 1) * gw]) for g in range(2)], axis=1)
        o_ref[pl.ds(pl.multiple_of(t * tq, tq), tq), :] = out_t.T.astype(o_ref.dtype)

    def finish(t, with_extra):
        vtb = vt_ref[0, :, pl.ds(first_key(t), n_sub * sub)]
        acc = jnp.dot(vtb, w_ref[...], preferred_element_type=F32)
        if with_extra:
            acc = acc + accx_ref[...]
        write_out(t, acc)

    w_ref[...] = jnp.zeros_like(w_ref)
    w0_ref[...] = jnp.zeros_like(w0_ref)
    z_ref[...] = scores(0)
    first_step(0, w0_ref, False)
    z_ref[...] = scores(1)
    write_out(0, jnp.dot(vt_ref[0, :, 0:tq], w0_ref[...], preferred_element_type=F32))

    def body(t, carry):
        finish(jnp.maximum(t - 1, 1), False)
        z_next = scores(jnp.minimum(t + 1, n_tiles - 1))
        first_step(SB_PRE_SUBS, w_ref, True)
        call_ref[t] = c_ref[...]
        z_ref[...] = z_next
        return carry

    lax.fori_loop(1, n_tiles, body, 0)
    finish(n_tiles - 1, False)

    @pl.when(jnp.max(call_ref[1:]) > F32_EXP2_UNDERFLOW)
    def _():
        def redo(t, carry):
            @pl.when(jnp.max(call_ref[t]) > F32_EXP2_UNDERFLOW)
            def _():
                z_ref[...] = scores(t)
                accx_ref[...] = jnp.zeros_like(accx_ref)
                extra_steps(t, first_step(SB_PRE_SUBS, w_ref, False))
                finish(t, True)
            return carry

        lax.fori_loop(1, n_tiles, redo, 0)


def _sb_attention(qt, k, vt, batch, seq):
    n, d = k.shape
    tq = SB_Q_TILE
    rows = SB_PRE_SUBS * SB_SUB + tq
    assert seq >= rows and seq // tq >= 2
    return pl.pallas_call(
        _sb_attn_kernel,
        grid=(batch, d // LANES),
        in_specs=[pl.BlockSpec((1, LANES, seq), lambda b, p: (b, p, 0)),
                  pl.BlockSpec((seq, LANES), lambda b, p: (b, p)),
                  pl.BlockSpec((1, LANES, seq), lambda b, p: (b, p, 0))],
        out_specs=pl.BlockSpec((seq, LANES), lambda b, p: (b, p)),
        out_shape=jax.ShapeDtypeStruct((n, d), BF16),
        scratch_shapes=[pltpu.VMEM((rows, 2 * tq), F32), pltpu.VMEM((rows, 2 * tq), BF16),
                        pltpu.VMEM((tq, 2 * tq), BF16),
                        pltpu.VMEM((1, 2 * tq), F32), pltpu.VMEM((LANES, 2 * tq), F32),
                        pltpu.VMEM((seq // tq, 1, 2 * tq), F32)],
        compiler_params=pltpu.CompilerParams(dimension_semantics=("arbitrary",) * 2,
                                             vmem_limit_bytes=VMEM_LIMIT_BYTES),
        name="sb_attention",
    )(qt, k, vt)


def _diff_attn_kernel(qt_ref, k_ref, vt_ref, lq1_ref, lk1_ref, lq2_ref, lk2_ref, sub_ref, o_ref,
                      m_ref, l_ref, acc_ref, alpha_ref, s_ref, p_ref, smax_ref, *, lam_init):
    t = DIFF_TILE
    n_tiles = k_ref.shape[0] // t

    def q2x(i):
        qt = qt_ref[0, :, pl.ds(pl.multiple_of(i * t, t), t)]
        row = lax.broadcasted_iota(jnp.int32, qt.shape, 0)
        zero = jnp.zeros_like(qt)
        return jnp.concatenate([jnp.where(row < DIFF_HEAD_DIM, qt, zero),
                                jnp.where(row >= DIFF_HEAD_DIM, qt, zero)], axis=1)

    def scores(i, j):
        kb = k_ref[pl.ds(pl.multiple_of(j * t, t), t), :]
        return jnp.dot(kb, q2x(i), preferred_element_type=F32)

    def stage_scores(s):
        s_ref[...] = s
        smax_ref[...] = jnp.max(s.reshape(t // 8, 8, 2 * t), axis=0)

    def softmax_first(i):
        s = s_ref[...]
        key_chunk = lax.broadcasted_iota(jnp.int32, s.shape, 0) // CHUNK
        q_chunk = (lax.broadcasted_iota(jnp.int32, s.shape, 1) % t) // CHUNK
        s = jnp.where(key_chunk <= q_chunk, s, NEG_INF)
        m_new = jnp.max(s, axis=0, keepdims=True)
        p = jnp.exp2(s - m_new)
        l_ref[i] = jnp.sum(p, axis=0, keepdims=True)
        p_ref[...] = p.astype(BF16)
        m_ref[i] = m_new

    def softmax_next(i):
        s = s_ref[...]
        m_old = m_ref[i]
        m_new = jnp.maximum(m_old, jnp.max(smax_ref[...], axis=0, keepdims=True))
        p = jnp.exp2(s - m_new)
        alpha = jnp.exp2(m_old - m_new)
        l_ref[i] = alpha * l_ref[i] + jnp.sum(p, axis=0, keepdims=True)
        p_ref[...] = p.astype(BF16)
        alpha_ref[...] = alpha
        m_ref[i] = m_new

    def pv(j):
        vtb = vt_ref[0, :, pl.ds(pl.multiple_of(j * t, t), t)]
        return jnp.dot(vtb, p_ref[...], preferred_element_type=F32)

    s_ref[...] = scores(0, 0)
    softmax_first(0)
    s_ref[...] = scores(1, 1)

    def diag_body(i, carry):
        acc_ref[i - 1] = pv(i - 1)
        nxt = jnp.minimum(i + 1, n_tiles - 1)
        s_next = scores(nxt, nxt)
        softmax_first(i)
        s_ref[...] = s_next
        return carry

    lax.fori_loop(1, n_tiles, diag_body, 0)
    acc_ref[n_tiles - 1] = pv(n_tiles - 1)

    def next_pair(i, j):
        wrap = j + 1 >= i
        return jnp.where(wrap, i + 1, i), jnp.where(wrap, 0, j + 1)

    stage_scores(scores(1, 0))
    softmax_next(1)
    i1, j1 = next_pair(1, 0)
    i1c = jnp.minimum(i1, n_tiles - 1)
    stage_scores(scores(i1c, jnp.minimum(j1, i1c - 1)))
    n_pairs = n_tiles * (n_tiles - 1) // 2

    def pair_body(n, carry):
        ip, jp, i, j = carry
        acc_ref[ip] = alpha_ref[...] * acc_ref[ip] + pv(jp)
        i2, j2 = next_pair(i, j)
        i2c = jnp.minimum(i2, n_tiles - 1)
        s_next = scores(i2c, jnp.minimum(j2, i2c - 1))
        softmax_next(i)
        stage_scores(s_next)
        return i, j, i2, j2

    ip, jp, _, _ = lax.fori_loop(1, n_pairs, pair_body, (1, 0, i1, j1))
    acc_ref[ip] = alpha_ref[...] * acc_ref[ip] + pv(jp)

    lam = (jnp.exp(jnp.sum(lq1_ref[...] * lk1_ref[...], axis=-1, keepdims=True))
           - jnp.exp(jnp.sum(lq2_ref[...] * lk2_ref[...], axis=-1, keepdims=True)) + lam_init)

    group = DIFF_OUT_GROUP if n_tiles % DIFF_OUT_GROUP == 0 else 1

    def out_body(ig, carry):
        for u in range(group):
            i = ig * group + u
            on = acc_ref[i] / l_ref[i]
            o = on[:, :t] - lam * on[:, t:]
            ms = jnp.mean(o * o, axis=0, keepdims=True)
            o = o * lax.rsqrt(ms + EPS)
            o = o.T * sub_ref[...] * (1.0 - lam_init)
            o_ref[pl.ds(pl.multiple_of(i * t, t), t), :] = o.astype(o_ref.dtype)
        return carry

    lax.fori_loop(0, n_tiles // group, out_body, 0)


def _diff_attention(qt, k, vt, lq1, lk1, lq2, lk2, sub_g, batch, seq, lam_init):
    n, d = k.shape
    t = DIFF_TILE
    n_tiles = seq // t
    assert n_tiles >= 2
    small = _const_spec((1, DIFF_HEAD_DIM))
    return pl.pallas_call(
        functools.partial(_diff_attn_kernel, lam_init=lam_init),
        grid=(batch, d // LANES),
        in_specs=[pl.BlockSpec((1, LANES, seq), lambda b, h: (b, h, 0)),
                  pl.BlockSpec((seq, LANES), lambda b, h: (b, h)),
                  pl.BlockSpec((1, LANES, seq), lambda b, h: (b, h, 0)),
                  small, small, small, small, _const_spec((1, LANES))],
        out_specs=pl.BlockSpec((seq, LANES), lambda b, h: (b, h)),
        out_shape=jax.ShapeDtypeStruct((n, d), BF16),
        scratch_shapes=[pltpu.VMEM((n_tiles, 1, 2 * t), F32), pltpu.VMEM((n_tiles, 1, 2 * t), F32),
                        pltpu.VMEM((n_tiles, LANES, 2 * t), F32), pltpu.VMEM((1, 2 * t), F32),
                        pltpu.VMEM((t, 2 * t), F32), pltpu.VMEM((t, 2 * t), BF16),
                        pltpu.VMEM((8, 2 * t), F32)],
        compiler_params=pltpu.CompilerParams(dimension_semantics=("arbitrary",) * 2,
                                             vmem_limit_bytes=VMEM_LIMIT_BYTES),
        name="diff_attention",
    )(qt, k, vt, lq1, lk1, lq2, lk2, sub_g)


def _mix_ffn_kernel(x_ref, o_ref, wo_ref, g_ref, wg_ref, wu_ref, wd_ref, out_ref):
    x1 = x_ref[...] + jnp.dot(o_ref[...], wo_ref[...], preferred_element_type=F32)
    h = _rms(x1, g_ref[...]).astype(BF16)
    acc = x1
    for lo, hi in FFN_CHUNKS:
        g = jnp.dot(h, wg_ref[:, lo:hi], preferred_element_type=F32)
        u = jnp.dot(h, wu_ref[:, lo:hi], preferred_element_type=F32)
        a = (g * (1.0 / (1.0 + jnp.exp(-g))) * u).astype(BF16)
        acc = acc + jnp.dot(a, wd_ref[lo:hi, :], preferred_element_type=F32)
    out_ref[...] = acc


def _mix_ffn(x2, o, wo, gain, wg, wu, wd):
    n, d = x2.shape
    tm = TOKEN_TILE
    assert FFN_CHUNKS[-1][1] == wg.shape[1]
    row_spec = pl.BlockSpec((tm, d), lambda i: (i, 0))
    return pl.pallas_call(
        _mix_ffn_kernel,
        grid=(n // tm,),
        in_specs=[row_spec, row_spec, _const_spec(wo.shape), _const_spec((1, d)),
                  _const_spec(wg.shape), _const_spec(wu.shape), _const_spec(wd.shape)],
        out_specs=row_spec,
        out_shape=jax.ShapeDtypeStruct((n, d), F32),
        compiler_params=pltpu.CompilerParams(dimension_semantics=("arbitrary",),
                                             vmem_limit_bytes=VMEM_LIMIT_BYTES),
        name="mix_ffn",
    )(x2, o, wo, gain, wg, wu, wd)


def _rope_tables(positions):
    inv_freq = ROPE_THETA ** (-jnp.arange(0, ROT_DIM, 2, dtype=F32) / ROT_DIM)
    ang = positions.astype(F32)[:, None, :] * inv_freq[None, :, None]
    return jnp.cos(ang), jnp.sin(ang)


def kernel(x, positions, attn_norm, w_in, w_out, q_norm, k_norm, lambda_q1, lambda_k1, lambda_q2,
           lambda_k2, sub_norm, ffn_norm, w_gate, w_up, w_down):
    batch, seq, d = x.shape
    depth = w_in.shape[0]
    assert seq % TOKEN_TILE == 0 and seq % SB_Q_TILE == 0 and seq % DIFF_TILE == 0
    assert d % LANES == 0 and DIFF_V_DIM == LANES and DIFF_TILE % CHUNK == 0
    x2 = x.reshape(batch * seq, d)
    tables = _rope_tables(positions)
    for i in range(depth):
        gain = attn_norm[i].reshape(1, d)
        if i % N_MIXERS == 0:
            qt, k, vt = _qkv_proj(x2, gain, w_in[i], batch, seq, SB_HEAD_DIM ** -0.5 * LOG2E)
            o = _sb_attention(qt, k, vt, batch, seq)
        else:
            j = i // N_MIXERS
            qg = jnp.broadcast_to(q_norm[j].reshape(-1, 1), (DIFF_HEAD_DIM, LANES))
            kg = jnp.broadcast_to(k_norm[j].reshape(-1, 1), (DIFF_HEAD_DIM, LANES))
            qt, k, vt = _qkv_proj(x2, gain, w_in[i], batch, seq, DIFF_HEAD_DIM ** -0.5 * LOG2E,
                                  (qg, kg) + tables)
            lam_init = 0.8 - 0.6 * math.exp(-0.3 * i)
            o = _diff_attention(qt, k, vt, lambda_q1[j].reshape(1, -1), lambda_k1[j].reshape(1, -1),
                                lambda_q2[j].reshape(1, -1), lambda_k2[j].reshape(1, -1),
                                sub_norm[j].reshape(1, -1), batch, seq, lam_init)
        x2 = _mix_ffn(x2, o, w_out[i].astype(BF16), ffn_norm[i].reshape(1, d),
                      w_gate[i].astype(BF16), w_up[i].astype(BF16), w_down[i].astype(BF16))
    return x2.reshape(batch, seq, d)
```

```python
import functools
import math

import jax
import jax.numpy as jnp
from jax import lax
from jax.experimental import pallas as pl
from jax.experimental.pallas import tpu as pltpu

F32 = jnp.float32
BF16 = jnp.bfloat16

N_MIXERS = 2
SB_HEAD_DIM = 64
DIFF_HEAD_DIM = 64
DIFF_V_DIM = 2 * DIFF_HEAD_DIM
CHUNK = 64
ROPE_THETA = 500000.0
ROT_DIM = DIFF_HEAD_DIM // 4
EPS = 1e-6
NEG_INF = -1e30

LANES = 128
VMEM_LIMIT_BYTES = 56 * 1024 * 1024

TOKEN_TILE = 512
SB_Q_TILE = 256
SB_SUB = 128
SB_PRE_SUBS = 2
SB_NEXT_SUBS = 2
DIFF_TILE = 512
DIFF_OUT_GROUP = 4
FFN_CHUNKS = ((0, 1024), (1024, 2048), (2048, 2816))

LOG2E = 1.4426950408889634
F32_EXP2_UNDERFLOW = -150.0


def _rms(x, gain):
    ms = jnp.mean(x * x, axis=-1, keepdims=True)
    return x * lax.rsqrt(ms + EPS) * gain


def _const_spec(shape):
    nd = len(shape)
    return pl.BlockSpec(shape, lambda *_: (0,) * nd, pipeline_mode=pl.Buffered(1))


def _qkv_kernel(x_ref, g_ref, wk_ref, wqvt_ref, qt_ref, k_ref, vt_ref, *, d_model, scale):
    h = _rms(x_ref[...], g_ref[...]).astype(BF16)
    k_ref[...] = jnp.dot(h, wk_ref[...], preferred_element_type=F32).astype(BF16)
    qvt = lax.dot_general(wqvt_ref[...], h, (((1,), (1,)), ((), ())), preferred_element_type=F32)
    qt_ref[0] = (qvt[:d_model, :] * scale).astype(BF16)
    vt_ref[0] = qvt[d_model:, :].astype(BF16)


def _qkv_diff_kernel(x_ref, g_ref, wt_ref, qg_ref, kg_ref, cost_ref, sint_ref,
                     qt_ref, k_ref, vt_ref, *, d_model, scale):
    h = _rms(x_ref[...], g_ref[...]).astype(BF16)
    tm = h.shape[0]
    cos_t, sin_t = cost_ref[0], sint_ref[0]
    half = ROT_DIM // 2

    def project(part):
        w = wt_ref[part * d_model:(part + 1) * d_model, :]
        return lax.dot_general(w, h, (((1,), (1,)), ((), ())), preferred_element_type=F32)

    def norm_rope(xg, gain):
        ms = jnp.mean(xg * xg, axis=0, keepdims=True)
        y = xg * lax.rsqrt(ms + EPS) * gain
        x1, x2 = y[:half], y[half:ROT_DIM]
        return jnp.concatenate([x1 * cos_t - x2 * sin_t, x2 * cos_t + x1 * sin_t, y[ROT_DIM:]],
                               axis=0)

    def slabs(xt, gain):
        per_slab = LANES // DIFF_HEAD_DIM
        for c in range(d_model // LANES):
            parts = [norm_rope(xt[g * DIFF_HEAD_DIM:(g + 1) * DIFF_HEAD_DIM, :], gain)
                     for g in range(c * per_slab, (c + 1) * per_slab)]
            yield slice(c * LANES, (c + 1) * LANES), jnp.concatenate(parts, axis=0)

    for sl, y in slabs(project(1), jnp.tile(kg_ref[...], (1, tm // LANES))):
        k_ref[:, sl] = y.T.astype(BF16)
    for sl, y in slabs(project(0), jnp.tile(qg_ref[...], (1, tm // LANES))):
        qt_ref[0, sl, :] = (y * scale).astype(BF16)
    vt_ref[0] = project(2).astype(BF16)


def _qkv_proj(x2, gain, w_in, batch, seq, scale, diff_args=None):
    n, d = x2.shape
    tm = TOKEN_TILE
    tiles_per_seq = seq // tm
    row_spec = pl.BlockSpec((tm, d), lambda i: (i, 0))
    t_spec = pl.BlockSpec((1, d, tm), lambda i: (i // tiles_per_seq, 0, i % tiles_per_seq))
    row_shape = jax.ShapeDtypeStruct((n, d), BF16)
    t_shape = jax.ShapeDtypeStruct((batch, d, seq), BF16)
    if diff_args is None:
        wk = w_in[:, d:2 * d].astype(BF16)
        wqvt = jnp.concatenate([w_in[:, :d], w_in[:, 2 * d:]], axis=1).T.astype(BF16)
        body = functools.partial(_qkv_kernel, d_model=d, scale=scale)
        name = "qkv_sb"
        in_specs = [row_spec, _const_spec((1, d)), _const_spec(wk.shape), _const_spec(wqvt.shape)]
        args = [x2, gain, wk, wqvt]
    else:
        qg, kg, cos_t, sin_t = diff_args
        wt = w_in.T.astype(BF16)
        tabt_spec = pl.BlockSpec((1, ROT_DIM // 2, tm),
                                 lambda i: (i // tiles_per_seq, 0, i % tiles_per_seq))
        body = functools.partial(_qkv_diff_kernel, d_model=d, scale=scale)
        name = "qkv_diff"
        in_specs = [row_spec, _const_spec((1, d)), _const_spec(wt.shape),
                    _const_spec(qg.shape), _const_spec(kg.shape), tabt_spec, tabt_spec]
        args = [x2, gain, wt, qg, kg, cos_t, sin_t]
    return pl.pallas_call(
        body,
        grid=(n // tm,),
        in_specs=in_specs,
        out_specs=[t_spec, row_spec, t_spec],
        out_shape=[t_shape, row_shape, t_shape],
        compiler_params=pltpu.CompilerParams(dimension_semantics=("arbitrary",),
                                             vmem_limit_bytes=VMEM_LIMIT_BYTES),
        name=name,
    )(*args)


def _sb_attn_kernel(qt_ref, k_ref, vt_ref, o_ref, z_ref, w_ref, w0_ref, c_ref, accx_ref, call_ref):
    tq, sub = SB_Q_TILE, SB_SUB
    half = tq // 2
    gw = 2 * half
    n_tiles = k_ref.shape[0] // tq
    n_sub = SB_PRE_SUBS + tq // sub
    hs = sub // 2

    def later_matrix(rows):
        r = lax.broadcasted_iota(jnp.int32, (rows, 2 * rows), 0)
        c = lax.broadcasted_iota(jnp.int32, (rows, 2 * rows), 1) % rows
        return jnp.where(c > r, 1.0, 0.0).astype(BF16)

    later2, later2_half = later_matrix(sub), later_matrix(hs)
    tri = (lax.broadcasted_iota(jnp.int32, (sub, gw), 0)
           < lax.broadcasted_iota(jnp.int32, (sub, gw), 1) % half)

    def q4x(t):
        qt = qt_ref[0, :, pl.ds(pl.multiple_of(t * tq, tq), tq)]
        row = lax.broadcasted_iota(jnp.int32, (LANES, half), 0)
        zero = jnp.zeros((LANES, half), qt.dtype)
        parts = []
        for g in range(2):
            qg = qt[:, g * half:(g + 1) * half]
            parts += [jnp.where(row < SB_HEAD_DIM, qg, zero), jnp.where(row >= SB_HEAD_DIM, qg, zero)]
        return jnp.concatenate(parts, axis=1)

    def first_block(t):
        return jnp.maximum(t * (tq // sub) - SB_PRE_SUBS, 0)

    def first_key(t):
        return pl.multiple_of(first_block(t) * sub, sub)

    def scores(t):
        kb = k_ref[pl.ds(first_key(t), n_sub * sub), :]
        return jnp.dot(kb, q4x(t), preferred_element_type=F32)

    def sub_block(z, c, mask, later2=later2):
        if mask is not None:
            z = jnp.where(mask, z, NEG_INF)
        nabs = pltpu.bitcast(pltpu.bitcast(z, jnp.uint32) | jnp.uint32(0x80000000), F32)
        sp = jnp.log(1.0 + jnp.exp2(nabs)) * LOG2E
        ls = jnp.minimum(z, 0.0) - sp
        lk = ls - z
        hi = lk.astype(BF16)
        lo = (lk - hi.astype(F32)).astype(BF16)
        after = jnp.dot(later2, jnp.concatenate([hi, lo], axis=0), preferred_element_type=F32)
        w = jnp.exp2(ls + after + c)
        return w.astype(BF16), c + after[0:1, :] + lk[0:1, :]

    def first_step(first_sb, dst_ref, skip_first):
        zero_c = jnp.zeros((1, gw), F32)
        lo_rows = slice(first_sb * sub, (first_sb + 1) * sub)
        hi_rows = slice((first_sb + 1) * sub, (first_sb + 2) * sub)
        wb, cb = sub_block(z_ref[hi_rows, gw:], zero_c, tri)
        dst_ref[hi_rows, gw:] = wb
        wa, ca = sub_block(z_ref[lo_rows, :gw], zero_c, tri)
        wb, cb = sub_block(z_ref[lo_rows, gw:], cb, None)
        dst_ref[lo_rows, :gw] = wa
        dst_ref[lo_rows, gw:] = wb
        c = jnp.concatenate([ca, cb], axis=1)
        for sb in reversed(range(1 if skip_first else 0, first_sb)):
            w, c = sub_block(z_ref[sb * sub:(sb + 1) * sub, :], c, None)
            dst_ref[sb * sub:(sb + 1) * sub, :] = w
        if skip_first:
            wa, ca = sub_block(z_ref[hs:sub, :gw], c[:, :gw], None, later2_half)
            dst_ref[hs:sub, :gw] = wa
            c = jnp.concatenate([ca, c[:, gw:]], axis=1)
        c_ref[...] = c
        return jnp.max(c)

    def extra_steps(t, c_max):
        def cond(carry):
            j, c_max = carry
            return (j > 0) & (c_max > F32_EXP2_UNDERFLOW)

        def body(carry):
            j, _ = carry
            j = j - SB_NEXT_SUBS
            start = pl.multiple_of(j * sub, sub)
            kb = k_ref[pl.ds(start, SB_NEXT_SUBS * sub), :]
            z_all = jnp.dot(kb, q4x(t), preferred_element_type=F32)
            c = c_ref[...]
            ws = [None] * SB_NEXT_SUBS
            for sb in reversed(range(SB_NEXT_SUBS)):
                ws[sb], c = sub_block(z_all[sb * sub:(sb + 1) * sub], c, None)
            vtb = vt_ref[0, :, pl.ds(start, SB_NEXT_SUBS * sub)]
            accx_ref[...] += jnp.dot(vtb, jnp.concatenate(ws, axis=0), preferred_element_type=F32)
            c_ref[...] = c
            return j, jnp.max(c)

        lax.while_loop(cond, body, (first_block(t), c_max))

    def write_out(t, acc):
        row_o = lax.broadcasted_iota(jnp.int32, (LANES, half), 0)
        out_t = jnp.concatenate(
            [jnp.where(row_o < SB_HEAD_DIM, acc[:, g * gw:g * gw + half],
                       acc[:, g * gw + half:(g + 1) * gw]) for g in range(2)], axis=1)
        o_ref[pl.ds(pl.multiple_of(t * tq, tq), tq), :] = out_t.T.astype(o_ref.dtype)

    def finish(t, with_extra):
        vtb = vt_ref[0, :, pl.ds(first_key(t), n_sub * sub)]
        acc = jnp.dot(vtb, w_ref[...], preferred_element_type=F32)
        if with_extra:
            acc = acc + accx_ref[...]
        write_out(t, acc)

    w_ref[...] = jnp.zeros_like(w_ref)
    w0_ref[...] = jnp.zeros_like(w0_ref)
    z_ref[...] = scores(0)
    first_step(0, w0_ref, False)
    z_ref[...] = scores(1)
    write_out(0, jnp.dot(vt_ref[0, :, 0:tq], w0_ref[...], preferred_element_type=F32))

    def body(t, carry):
        finish(jnp.maximum(t - 1, 1), False)
        z_next = scores(jnp.minimum(t + 1, n_tiles - 1))
        first_step(SB_PRE_SUBS, w_ref, True)
        call_ref[t] = c_ref[...]
        z_ref[...] = z_next
        return carry

    lax.fori_loop(1, n_tiles, body, 0)
    finish(n_tiles - 1, False)

    @pl.when(jnp.max(call_ref[1:]) > F32_EXP2_UNDERFLOW)
    def _():
        def redo(t, carry):
            @pl.when(jnp.max(call_ref[t]) > F32_EXP2_UNDERFLOW)
            def _():
                z_ref[...] = scores(t)
                accx_ref[...] = jnp.zeros_like(accx_ref)
                extra_steps(t, first_step(SB_PRE_SUBS, w_ref, False))
                finish(t, True)
            return carry

        lax.fori_loop(1, n_tiles, redo, 0)


def _sb_attention(qt, k, vt, batch, seq):
    n, d = k.shape
    tq = SB_Q_TILE
    rows = SB_PRE_SUBS * SB_SUB + tq
    assert seq >= rows and seq // tq >= 2
    return pl.pallas_call(
        _sb_attn_kernel,
        grid=(batch, d // LANES),
        in_specs=[pl.BlockSpec((1, LANES, seq), lambda b, p: (b, p, 0)),
                  pl.BlockSpec((seq, LANES), lambda b, p: (b, p)),
                  pl.BlockSpec((1, LANES, seq), lambda b, p: (b, p, 0))],
        out_specs=pl.BlockSpec((seq, LANES), lambda b, p: (b, p)),
        out_shape=jax.ShapeDtypeStruct((n, d), BF16),
        scratch_shapes=[pltpu.VMEM((rows, 2 * tq), F32), pltpu.VMEM((rows, 2 * tq), BF16),
                        pltpu.VMEM((tq, 2 * tq), BF16),
                        pltpu.VMEM((1, 2 * tq), F32), pltpu.VMEM((LANES, 2 * tq), F32),
                        pltpu.VMEM((seq // tq, 1, 2 * tq), F32)],
        compiler_params=pltpu.CompilerParams(dimension_semantics=("arbitrary",) * 2,
                                             vmem_limit_bytes=VMEM_LIMIT_BYTES),
        name="sb_attention",
    )(qt, k, vt)


def _diff_attn_kernel(qt_ref, k_ref, vt_ref, lq1_ref, lk1_ref, lq2_ref, lk2_ref, sub_ref, o_ref,
                      m_ref, l_ref, acc_ref, alpha_ref, s_ref, p_ref, smax_ref, *, lam_init):
    t = DIFF_TILE
    n_tiles = k_ref.shape[0] // t

    def q2x(i):
        qt = qt_ref[0, :, pl.ds(pl.multiple_of(i * t, t), t)]
        row = lax.broadcasted_iota(jnp.int32, qt.shape, 0)
        zero = jnp.zeros_like(qt)
        return jnp.concatenate([jnp.where(row < DIFF_HEAD_DIM, qt, zero),
                                jnp.where(row >= DIFF_HEAD_DIM, qt, zero)], axis=1)

    def scores(i, j):
        kb = k_ref[pl.ds(pl.multiple_of(j * t, t), t), :]
        return jnp.dot(kb, q2x(i), preferred_element_type=F32)

    def stage_scores(s):
        s_ref[...] = s
        smax_ref[...] = jnp.max(s.reshape(t // 8, 8, 2 * t), axis=0)

    def softmax_first(i):
        s = s_ref[...]
        key_chunk = lax.broadcasted_iota(jnp.int32, s.shape, 0) // CHUNK
        q_chunk = (lax.broadcasted_iota(jnp.int32, s.shape, 1) % t) // CHUNK
        s = jnp.where(key_chunk <= q_chunk, s, NEG_INF)
        m_new = jnp.max(s, axis=0, keepdims=True)
        p = jnp.exp2(s - m_new)
        l_ref[i] = jnp.sum(p, axis=0, keepdims=True)
        p_ref[...] = p.astype(BF16)
        m_ref[i] = m_new

    def softmax_next(i):
        s = s_ref[...]
        m_old = m_ref[i]
        m_new = jnp.maximum(m_old, jnp.max(smax_ref[...], axis=0, keepdims=True))
        p = jnp.exp2(s - m_new)
        alpha = jnp.exp2(m_old - m_new)
        l_ref[i] = alpha * l_ref[i] + jnp.sum(p, axis=0, keepdims=True)
        p_ref[...] = p.astype(BF16)
        alpha_ref[...] = alpha
        m_ref[i] = m_new

    def pv(j):
        vtb = vt_ref[0, :, pl.ds(pl.multiple_of(j * t, t), t)]
        return jnp.dot(vtb, p_ref[...], preferred_element_type=F32)

    s_ref[...] = scores(0, 0)
    softmax_first(0)
    s_ref[...] = scores(1, 1)

    def diag_body(i, carry):
        acc_ref[i - 1] = pv(i - 1)
        nxt = jnp.minimum(i + 1, n_tiles - 1)
        s_next = scores(nxt, nxt)
        softmax_first(i)
        s_ref[...] = s_next
        return carry

    lax.fori_loop(1, n_tiles, diag_body, 0)
    acc_ref[n_tiles - 1] = pv(n_tiles - 1)

    def next_pair(i, j):
        wrap = j + 1 >= i
        return jnp.where(wrap, i + 1, i), jnp.where(wrap, 0, j + 1)

    stage_scores(scores(1, 0))
    softmax_next(1)
    i1, j1 = next_pair(1, 0)
    i1c = jnp.minimum(i1, n_tiles - 1)
    stage_scores(scores(i1c, jnp.minimum(j1, i1c - 1)))
    n_pairs = n_tiles * (n_tiles - 1) // 2

    def pair_body(n, carry):
        ip, jp, i, j = carry
        acc_ref[ip] = alpha_ref[...] * acc_ref[ip] + pv(jp)
        i2, j2 = next_pair(i, j)
        i2c = jnp.minimum(i2, n_tiles - 1)
        s_next = scores(i2c, jnp.minimum(j2, i2c - 1))
        softmax_next(i)
        stage_scores(s_next)
        return i, j, i2, j2

    ip, jp, _, _ = lax.fori_loop(1, n_pairs, pair_body, (1, 0, i1, j1))
    acc_ref[ip] = alpha_ref[...] * acc_ref[ip] + pv(jp)

    lam = (jnp.exp(jnp.sum(lq1_ref[...] * lk1_ref[...], axis=-1, keepdims=True))
           - jnp.exp(jnp.sum(lq2_ref[...] * lk2_ref[...], axis=-1, keepdims=True)) + lam_init)

    group = DIFF_OUT_GROUP if n_tiles % DIFF_OUT_GROUP == 0 else 1

    def out_body(ig, carry):
        for u in range(group):
            i = ig * group + u
            on = acc_ref[i] / l_ref[i]
            o = on[:, :t] - lam * on[:, t:]
            ms = jnp.mean(o * o, axis=0, keepdims=True)
            o = o * lax.rsqrt(ms + EPS)
            o = o.T * sub_ref[...] * (1.0 - lam_init)
            o_ref[pl.ds(pl.multiple_of(i * t, t), t), :] = o.astype(o_ref.dtype)
        return carry

    lax.fori_loop(0, n_tiles // group, out_body, 0)


def _diff_attention(qt, k, vt, lq1, lk1, lq2, lk2, sub_g, batch, seq, lam_init):
    n, d = k.shape
    t = DIFF_TILE
    n_tiles = seq // t
    assert n_tiles >= 2
    small = _const_spec((1, DIFF_HEAD_DIM))
    return pl.pallas_call(
        functools.partial(_diff_attn_kernel, lam_init=lam_init),
        grid=(batch, d // LANES),
        in_specs=[pl.BlockSpec((1, LANES, seq), lambda b, h: (b, h, 0)),
                  pl.BlockSpec((seq, LANES), lambda b, h: (b, h)),
                  pl.BlockSpec((1, LANES, seq), lambda b, h: (b, h, 0)),
                  small, small, small, small, _const_spec((1, LANES))],
        out_specs=pl.BlockSpec((seq, LANES), lambda b, h: (b, h)),
        out_shape=jax.ShapeDtypeStruct((n, d), BF16),
        scratch_shapes=[pltpu.VMEM((n_tiles, 1, 2 * t), F32), pltpu.VMEM((n_tiles, 1, 2 * t), F32),
                        pltpu.VMEM((n_tiles, LANES, 2 * t), F32), pltpu.VMEM((1, 2 * t), F32),
                        pltpu.VMEM((t, 2 * t), F32), pltpu.VMEM((t, 2 * t), BF16),
                        pltpu.VMEM((8, 2 * t), F32)],
        compiler_params=pltpu.CompilerParams(dimension_semantics=("arbitrary",) * 2,
                                             vmem_limit_bytes=VMEM_LIMIT_BYTES),
        name="diff_attention",
    )(qt, k, vt, lq1, lk1, lq2, lk2, sub_g)


def _mix_ffn_kernel(x_ref, o_ref, wo_ref, g_ref, wg_ref, wu_ref, wd_ref, out_ref):
    x1 = x_ref[...] + jnp.dot(o_ref[...], wo_ref[...], preferred_element_type=F32)
    h = _rms(x1, g_ref[...]).astype(BF16)
    acc = x1
    for lo, hi in FFN_CHUNKS:
        g = jnp.dot(h, wg_ref[:, lo:hi], preferred_element_type=F32)
        u = jnp.dot(h, wu_ref[:, lo:hi], preferred_element_type=F32)
        a = (g * (1.0 / (1.0 + jnp.exp(-g))) * u).astype(BF16)
        acc = acc + jnp.dot(a, wd_ref[lo:hi, :], preferred_element_type=F32)
    out_ref[...] = acc


def _mix_ffn(x2, o, wo, gain, wg, wu, wd):
    n, d = x2.shape
    tm = TOKEN_TILE
    assert FFN_CHUNKS[-1][1] == wg.shape[1]
    row_spec = pl.BlockSpec((tm, d), lambda i: (i, 0))
    return pl.pallas_call(
        _mix_ffn_kernel,
        grid=(n // tm,),
        in_specs=[row_spec, row_spec, _const_spec(wo.shape), _const_spec((1, d)),
                  _const_spec(wg.shape), _const_spec(wu.shape), _const_spec(wd.shape)],
        out_specs=row_spec,
        out_shape=jax.ShapeDtypeStruct((n, d), F32),
        compiler_params=pltpu.CompilerParams(dimension_semantics=("arbitrary",),
                                             vmem_limit_bytes=VMEM_LIMIT_BYTES),
        name="mix_ffn",
    )(x2, o, wo, gain, wg, wu, wd)


def _rope_tables(positions):
    inv_freq = ROPE_THETA ** (-jnp.arange(0, ROT_DIM, 2, dtype=F32) / ROT_DIM)
    ang = positions.astype(F32)[:, None, :] * inv_freq[None, :, None]
    return jnp.cos(ang), jnp.sin(ang)


def kernel(x, positions, attn_norm, w_in, w_out, q_norm, k_norm, lambda_q1, lambda_k1, lambda_q2,
           lambda_k2, sub_norm, ffn_norm, w_gate, w_up, w_down):
    batch, seq, d = x.shape
    depth = w_in.shape[0]
    assert seq % TOKEN_TILE == 0 and seq % SB_Q_TILE == 0 and seq % DIFF_TILE == 0
    assert d % LANES == 0 and DIFF_V_DIM == LANES and DIFF_TILE % CHUNK == 0
    x2 = x.reshape(batch * seq, d)
    tables = _rope_tables(positions)
    for i in range(depth):
        gain = attn_norm[i].reshape(1, d)
        if i % N_MIXERS == 0:
            qt, k, vt = _qkv_proj(x2, gain, w_in[i], batch, seq, SB_HEAD_DIM ** -0.5 * LOG2E)
            o = _sb_attention(qt, k, vt, batch, seq)
        else:
            j = i // N_MIXERS
            qg = jnp.broadcast_to(q_norm[j].reshape(-1, 1), (DIFF_HEAD_DIM, LANES))
            kg = jnp.broadcast_to(k_norm[j].reshape(-1, 1), (DIFF_HEAD_DIM, LANES))
            qt, k, vt = _qkv_proj(x2, gain, w_in[i], batch, seq, DIFF_HEAD_DIM ** -0.5 * LOG2E,
                                  (qg, kg) + tables)
            lam_init = 0.8 - 0.6 * math.exp(-0.3 * i)
            o = _diff_attention(qt, k, vt, lambda_q1[j].reshape(1, -1), lambda_k1[j].reshape(1, -1),
                                lambda_q2[j].reshape(1, -1), lambda_k2[j].reshape(1, -1),
                                sub_norm[j].reshape(1, -1), batch, seq, lam_init)
        x2 = _mix_ffn(x2, o, w_out[i].astype(BF16), ffn_norm[i].reshape(1, d),
                      w_gate[i].astype(BF16), w_up[i].astype(BF16), w_down[i].astype(BF16))
    return x2.reshape(batch, seq, d)
```

```python
import functools
import math

import jax
import jax.numpy as jnp
from jax import lax
from jax.experimental import pallas as pl
from jax.experimental.pallas import tpu as pltpu

F32 = jnp.float32
BF16 = jnp.bfloat16

N_MIXERS = 2
SB_HEAD_DIM = 64
DIFF_HEAD_DIM = 64
DIFF_V_DIM = 2 * DIFF_HEAD_DIM
CHUNK = 64
ROPE_THETA = 500000.0
ROT_DIM = DIFF_HEAD_DIM // 4
EPS = 1e-6
NEG_INF = -1e30

LANES = 128
VMEM_LIMIT_BYTES = 56 * 1024 * 1024

TOKEN_TILE = 512
SB_Q_TILE = 256
SB_SUB = 128
SB_PRE_SUBS = 2
SB_NEXT_SUBS = 2
DIFF_TILE = 512
DIFF_OUT_GROUP = 4
FFN_CHUNKS = ((0, 1024), (1024, 2048), (2048, 2816))

LOG2E = 1.4426950408889634
F32_EXP2_UNDERFLOW = -150.0


def _rms(x, gain):
    ms = jnp.mean(x * x, axis=-1, keepdims=True)
    return x * lax.rsqrt(ms + EPS) * gain


def _const_spec(shape):
    nd = len(shape)
    return pl.BlockSpec(shape, lambda *_: (0,) * nd, pipeline_mode=pl.Buffered(1))


def _project_t(wt_ref, h, part, d_model):
    w = wt_ref[part * d_model:(part + 1) * d_model, :]
    return lax.dot_general(w, h, (((1,), (1,)), ((), ())), preferred_element_type=F32)


def _qkv_kernel(x_ref, g_ref, wt_ref, qt_ref, k_ref, vt_ref, *, d_model, scale):
    h = _rms(x_ref[...], g_ref[...]).astype(BF16)
    kt = _project_t(wt_ref, h, 1, d_model)
    for c in range(d_model // LANES):
        sl = slice(c * LANES, (c + 1) * LANES)
        k_ref[:, sl] = kt[sl, :].T.astype(BF16)
    qt_ref[0] = (_project_t(wt_ref, h, 0, d_model) * scale).astype(BF16)
    vt_ref[0] = _project_t(wt_ref, h, 2, d_model).astype(BF16)


def _qkv_diff_kernel(x_ref, g_ref, wt_ref, qg_ref, kg_ref, cost_ref, sint_ref,
                     qt_ref, k_ref, vt_ref, *, d_model, scale):
    h = _rms(x_ref[...], g_ref[...]).astype(BF16)
    tm = h.shape[0]
    cos_t, sin_t = cost_ref[0], sint_ref[0]
    half = ROT_DIM // 2

    project = functools.partial(_project_t, wt_ref, h, d_model=d_model)

    def norm_rope(xg, gain):
        ms = jnp.mean(xg * xg, axis=0, keepdims=True)
        y = xg * lax.rsqrt(ms + EPS) * gain
        x1, x2 = y[:half], y[half:ROT_DIM]
        return jnp.concatenate([x1 * cos_t - x2 * sin_t, x2 * cos_t + x1 * sin_t, y[ROT_DIM:]],
                               axis=0)

    def slabs(xt, gain):
        per_slab = LANES // DIFF_HEAD_DIM
        for c in range(d_model // LANES):
            parts = [norm_rope(xt[g * DIFF_HEAD_DIM:(g + 1) * DIFF_HEAD_DIM, :], gain)
                     for g in range(c * per_slab, (c + 1) * per_slab)]
            yield slice(c * LANES, (c + 1) * LANES), jnp.concatenate(parts, axis=0)

    for sl, y in slabs(project(1), jnp.tile(kg_ref[...], (1, tm // LANES))):
        k_ref[:, sl] = y.T.astype(BF16)
    for sl, y in slabs(project(0), jnp.tile(qg_ref[...], (1, tm // LANES))):
        qt_ref[0, sl, :] = (y * scale).astype(BF16)
    vt_ref[0] = project(2).astype(BF16)


def _qkv_proj(x2, gain, wt, batch, seq, scale, diff_args=None):
    n, d = x2.shape
    tm = TOKEN_TILE
    tiles_per_seq = seq // tm
    row_spec = pl.BlockSpec((tm, d), lambda i: (i, 0))
    t_spec = pl.BlockSpec((1, d, tm), lambda i: (i // tiles_per_seq, 0, i % tiles_per_seq))
    row_shape = jax.ShapeDtypeStruct((n, d), BF16)
    t_shape = jax.ShapeDtypeStruct((batch, d, seq), BF16)
    if diff_args is None:
        body = functools.partial(_qkv_kernel, d_model=d, scale=scale)
        name = "qkv_sb"
        in_specs = [row_spec, _const_spec((1, d)), _const_spec(wt.shape)]
        args = [x2, gain, wt]
    else:
        qg, kg, cos_t, sin_t = diff_args
        tabt_spec = pl.BlockSpec((1, ROT_DIM // 2, tm),
                                 lambda i: (i // tiles_per_seq, 0, i % tiles_per_seq))
        body = functools.partial(_qkv_diff_kernel, d_model=d, scale=scale)
        name = "qkv_diff"
        in_specs = [row_spec, _const_spec((1, d)), _const_spec(wt.shape),
                    _const_spec(qg.shape), _const_spec(kg.shape), tabt_spec, tabt_spec]
        args = [x2, gain, wt, qg, kg, cos_t, sin_t]
    return pl.pallas_call(
        body,
        grid=(n // tm,),
        in_specs=in_specs,
        out_specs=[t_spec, row_spec, t_spec],
        out_shape=[t_shape, row_shape, t_shape],
        compiler_params=pltpu.CompilerParams(dimension_semantics=("arbitrary",),
                                             vmem_limit_bytes=VMEM_LIMIT_BYTES),
        name=name,
    )(*args)


def _sb_attn_kernel(qt_ref, k_ref, vt_ref, o_ref, z_ref, w_ref, w0_ref, c_ref, accx_ref, call_ref):
    tq, sub = SB_Q_TILE, SB_SUB
    half = tq // 2
    gw = 2 * half
    n_tiles = k_ref.shape[0] // tq
    n_sub = SB_PRE_SUBS + tq // sub
    r_u = lax.broadcasted_iota(jnp.int32, (sub, 2 * sub), 0)
    c_u = lax.broadcasted_iota(jnp.int32, (sub, 2 * sub), 1) % sub
    later2 = jnp.where(c_u > r_u, 1.0, 0.0).astype(BF16)
    tri = (lax.broadcasted_iota(jnp.int32, (sub, gw), 0)
           < lax.broadcasted_iota(jnp.int32, (sub, gw), 1) % half)

    def q4x(t):
        qt = qt_ref[0, :, pl.ds(pl.multiple_of(t * tq, tq), tq)]
        row = lax.broadcasted_iota(jnp.int32, (LANES, half), 0)
        zero = jnp.zeros((LANES, half), qt.dtype)
        parts = []
        for g in range(2):
            qg = qt[:, g * half:(g + 1) * half]
            parts += [jnp.where(row < SB_HEAD_DIM, qg, zero), jnp.where(row >= SB_HEAD_DIM, qg, zero)]
        return jnp.concatenate(parts, axis=1)

    def first_block(t):
        return jnp.maximum(t * (tq // sub) - SB_PRE_SUBS, 0)

    def first_key(t):
        return pl.multiple_of(first_block(t) * sub, sub)

    def scores(t):
        kb = k_ref[pl.ds(first_key(t), n_sub * sub), :]
        return jnp.dot(kb, q4x(t), preferred_element_type=F32)

    def sub_block(z, c, mask):
        if mask is not None:
            z = jnp.where(mask, z, NEG_INF)
        nabs = pltpu.bitcast(pltpu.bitcast(z, jnp.uint32) | jnp.uint32(0x80000000), F32)
        sp = jnp.log(1.0 + jnp.exp2(nabs)) * LOG2E
        ls = jnp.minimum(z, 0.0) - sp
        lk = ls - z
        hi = lk.astype(BF16)
        lo = (lk - hi.astype(F32)).astype(BF16)
        after = jnp.dot(later2, jnp.concatenate([hi, lo], axis=0), preferred_element_type=F32)
        w = jnp.exp2(ls + after + c)
        return w.astype(BF16), c + after[0:1, :] + lk[0:1, :]

    def first_step(first_sb, dst_ref, skip_first):
        zero_c = jnp.zeros((1, gw), F32)
        lo_rows = slice(first_sb * sub, (first_sb + 1) * sub)
        hi_rows = slice((first_sb + 1) * sub, (first_sb + 2) * sub)
        wb, cb = sub_block(z_ref[hi_rows, gw:], zero_c, tri)
        dst_ref[hi_rows, gw:] = wb
        wa, ca = sub_block(z_ref[lo_rows, :gw], zero_c, tri)
        wb, cb = sub_block(z_ref[lo_rows, gw:], cb, None)
        dst_ref[lo_rows, :gw] = wa
        dst_ref[lo_rows, gw:] = wb
        c = jnp.concatenate([ca, cb], axis=1)
        for sb in reversed(range(1 if skip_first else 0, first_sb)):
            w, c = sub_block(z_ref[sb * sub:(sb + 1) * sub, :], c, None)
            dst_ref[sb * sub:(sb + 1) * sub, :] = w
        if skip_first:
            wa, ca = sub_block(z_ref[0:sub, :gw], c[:, :gw], None)
            dst_ref[0:sub, :gw] = wa
            c = jnp.concatenate([ca, c[:, gw:]], axis=1)
        c_ref[...] = c
        return jnp.max(c)

    def extra_steps(t, c_max):
        def cond(carry):
            j, c_max = carry
            return (j > 0) & (c_max > F32_EXP2_UNDERFLOW)

        def body(carry):
            j, _ = carry
            j = j - SB_NEXT_SUBS
            start = pl.multiple_of(j * sub, sub)
            kb = k_ref[pl.ds(start, SB_NEXT_SUBS * sub), :]
            z_all = jnp.dot(kb, q4x(t), preferred_element_type=F32)
            c = c_ref[...]
            ws = [None] * SB_NEXT_SUBS
            for sb in reversed(range(SB_NEXT_SUBS)):
                ws[sb], c = sub_block(z_all[sb * sub:(sb + 1) * sub], c, None)
            vtb = vt_ref[0, :, pl.ds(start, SB_NEXT_SUBS * sub)]
            accx_ref[...] += jnp.dot(vtb, jnp.concatenate(ws, axis=0), preferred_element_type=F32)
            c_ref[...] = c
            return j, jnp.max(c)

        lax.while_loop(cond, body, (first_block(t), c_max))

    def write_out(t, acc):
        row_o = lax.broadcasted_iota(jnp.int32, (LANES, half), 0)
        out_t = jnp.concatenate(
            [jnp.where(row_o < SB_HEAD_DIM, acc[:, g * gw:g * gw + half],
                       acc[:, g * gw + half:(g + 1) * gw]) for g in range(2)], axis=1)
        o_ref[pl.ds(pl.multiple_of(t * tq, tq), tq), :] = out_t.T.astype(o_ref.dtype)

    def finish(t, with_extra):
        vtb = vt_ref[0, :, pl.ds(first_key(t), n_sub * sub)]
        acc = jnp.dot(vtb, w_ref[...], preferred_element_type=F32)
        if with_extra:
            acc = acc + accx_ref[...]
        write_out(t, acc)

    w_ref[...] = jnp.zeros_like(w_ref)
    w0_ref[...] = jnp.zeros_like(w0_ref)
    z_ref[...] = scores(0)
    first_step(0, w0_ref, False)
    z_ref[...] = scores(1)
    write_out(0, jnp.dot(vt_ref[0, :, 0:tq], w0_ref[...], preferred_element_type=F32))

    def body(t, carry):
        finish(jnp.maximum(t - 1, 1), False)
        z_next = scores(jnp.minimum(t + 1, n_tiles - 1))
        first_step(SB_PRE_SUBS, w_ref, True)
        call_ref[t] = c_ref[...]
        z_ref[...] = z_next
        return carry

    lax.fori_loop(1, n_tiles, body, 0)
    finish(n_tiles - 1, False)

    @pl.when(jnp.max(call_ref[1:]) > F32_EXP2_UNDERFLOW)
    def _():
        def redo(t, carry):
            @pl.when(jnp.max(call_ref[t]) > F32_EXP2_UNDERFLOW)
            def _():
                z_ref[...] = scores(t)
                accx_ref[...] = jnp.zeros_like(accx_ref)
                extra_steps(t, first_step(SB_PRE_SUBS, w_ref, False))
                finish(t, True)
            return carry

        lax.fori_loop(1, n_tiles, redo, 0)


def _sb_attention(qt, k, vt, batch, seq):
    n, d = k.shape
    tq = SB_Q_TILE
    rows = SB_PRE_SUBS * SB_SUB + tq
    assert seq >= rows and seq // tq >= 2
    return pl.pallas_call(
        _sb_attn_kernel,
        grid=(batch, d // LANES),
        in_specs=[pl.BlockSpec((1, LANES, seq), lambda b, p: (b, p, 0)),
                  pl.BlockSpec((seq, LANES), lambda b, p: (b, p)),
                  pl.BlockSpec((1, LANES, seq), lambda b, p: (b, p, 0))],
        out_specs=pl.BlockSpec((seq, LANES), lambda b, p: (b, p)),
        out_shape=jax.ShapeDtypeStruct((n, d), BF16),
        scratch_shapes=[pltpu.VMEM((rows, 2 * tq), F32), pltpu.VMEM((rows, 2 * tq), BF16),
                        pltpu.VMEM((tq, 2 * tq), BF16),
                        pltpu.VMEM((1, 2 * tq), F32), pltpu.VMEM((LANES, 2 * tq), F32),
                        pltpu.VMEM((seq // tq, 1, 2 * tq), F32)],
        compiler_params=pltpu.CompilerParams(dimension_semantics=("arbitrary",) * 2,
                                             vmem_limit_bytes=VMEM_LIMIT_BYTES),
        name="sb_attention",
    )(qt, k, vt)


def _diff_attn_kernel(qt_ref, k_ref, vt_ref, lq1_ref, lk1_ref, lq2_ref, lk2_ref, sub_ref, o_ref,
                      m_ref, l_ref, acc_ref, alpha_ref, s_ref, p_ref, smax_ref, *, lam_init):
    t = DIFF_TILE
    n_tiles = k_ref.shape[0] // t

    def q2x(i):
        qt = qt_ref[0, :, pl.ds(pl.multiple_of(i * t, t), t)]
        row = lax.broadcasted_iota(jnp.int32, qt.shape, 0)
        zero = jnp.zeros_like(qt)
        return jnp.concatenate([jnp.where(row < DIFF_HEAD_DIM, qt, zero),
                                jnp.where(row >= DIFF_HEAD_DIM, qt, zero)], axis=1)

    def scores(i, j):
        kb = k_ref[pl.ds(pl.multiple_of(j * t, t), t), :]
        return jnp.dot(kb, q2x(i), preferred_element_type=F32)

    def stage_scores(s):
        s_ref[...] = s
        smax_ref[...] = jnp.max(s.reshape(t // 8, 8, 2 * t), axis=0)

    def softmax_first(i):
        s = s_ref[...]
        key_chunk = lax.broadcasted_iota(jnp.int32, s.shape, 0) // CHUNK
        q_chunk = (lax.broadcasted_iota(jnp.int32, s.shape, 1) % t) // CHUNK
        s = jnp.where(key_chunk <= q_chunk, s, NEG_INF)
        m_new = jnp.max(s, axis=0, keepdims=True)
        p = jnp.exp2(s - m_new)
        l_ref[i] = jnp.sum(p, axis=0, keepdims=True)
        p_ref[...] = p.astype(BF16)
        m_ref[i] = m_new

    def softmax_next(i):
        s = s_ref[...]
        m_old = m_ref[i]
        m_new = jnp.maximum(m_old, jnp.max(smax_ref[...], axis=0, keepdims=True))
        p = jnp.exp2(s - m_new)
        alpha = jnp.exp2(m_old - m_new)
        l_ref[i] = alpha * l_ref[i] + jnp.sum(p, axis=0, keepdims=True)
        p_ref[...] = p.astype(BF16)
        alpha_ref[...] = alpha
        m_ref[i] = m_new

    def pv(j):
        vtb = vt_ref[0, :, pl.ds(pl.multiple_of(j * t, t), t)]
        return jnp.dot(vtb, p_ref[...], preferred_element_type=F32)

    s_ref[...] = scores(0, 0)
    softmax_first(0)
    s_ref[...] = scores(1, 1)

    def diag_body(i, carry):
        acc_ref[i - 1] = pv(i - 1)
        nxt = jnp.minimum(i + 1, n_tiles - 1)
        s_next = scores(nxt, nxt)
        softmax_first(i)
        s_ref[...] = s_next
        return carry

    lax.fori_loop(1, n_tiles, diag_body, 0)
    acc_ref[n_tiles - 1] = pv(n_tiles - 1)

    def next_pair(i, j):
        wrap = j + 1 >= i
        return jnp.where(wrap, i + 1, i), jnp.where(wrap, 0, j + 1)

    stage_scores(scores(1, 0))
    softmax_next(1)
    i1, j1 = next_pair(1, 0)
    i1c = jnp.minimum(i1, n_tiles - 1)
    stage_scores(scores(i1c, jnp.minimum(j1, i1c - 1)))
    n_pairs = n_tiles * (n_tiles - 1) // 2

    def pair_body(n, carry):
        ip, jp, i, j = carry
        acc_ref[ip] = alpha_ref[...] * acc_ref[ip] + pv(jp)
        i2, j2 = next_pair(i, j)
        i2c = jnp.minimum(i2, n_tiles - 1)
        s_next = scores(i2c, jnp.minimum(j2, i2c - 1))
        softmax_next(i)
        stage_scores(s_next)
        return i, j, i2, j2

    ip, jp, _, _ = lax.fori_loop(1, n_pairs, pair_body, (1, 0, i1, j1))
    acc_ref[ip] = alpha_ref[...] * acc_ref[ip] + pv(jp)

    lam = (jnp.exp(jnp.sum(lq1_ref[...] * lk1_ref[...], axis=-1, keepdims=True))
           - jnp.exp(jnp.sum(lq2_ref[...] * lk2_ref[...], axis=-1, keepdims=True)) + lam_init)

    group = DIFF_OUT_GROUP if n_tiles % DIFF_OUT_GROUP == 0 else 1

    def out_body(ig, carry):
        for u in range(group):
            i = ig * group + u
            on = acc_ref[i] / l_ref[i]
            o = on[:, :t] - lam * on[:, t:]
            ms = jnp.mean(o * o, axis=0, keepdims=True)
            o = o * lax.rsqrt(ms + EPS)
            o = o.T * sub_ref[...] * (1.0 - lam_init)
            o_ref[pl.ds(pl.multiple_of(i * t, t), t), :] = o.astype(o_ref.dtype)
        return carry

    lax.fori_loop(0, n_tiles // group, out_body, 0)


def _diff_attention(qt, k, vt, lq1, lk1, lq2, lk2, sub_g, batch, seq, lam_init):
    n, d = k.shape
    t = DIFF_TILE
    n_tiles = seq // t
    assert n_tiles >= 2
    small = _const_spec((1, DIFF_HEAD_DIM))
    return pl.pallas_call(
        functools.partial(_diff_attn_kernel, lam_init=lam_init),
        grid=(batch, d // LANES),
        in_specs=[pl.BlockSpec((1, LANES, seq), lambda b, h: (b, h, 0)),
                  pl.BlockSpec((seq, LANES), lambda b, h: (b, h)),
                  pl.BlockSpec((1, LANES, seq), lambda b, h: (b, h, 0)),
                  small, small, small, small, _const_spec((1, LANES))],
        out_specs=pl.BlockSpec((seq, LANES), lambda b, h: (b, h)),
        out_shape=jax.ShapeDtypeStruct((n, d), BF16),
        scratch_shapes=[pltpu.VMEM((n_tiles, 1, 2 * t), F32), pltpu.VMEM((n_tiles, 1, 2 * t), F32),
                        pltpu.VMEM((n_tiles, LANES, 2 * t), F32), pltpu.VMEM((1, 2 * t), F32),
                        pltpu.VMEM((t, 2 * t), F32), pltpu.VMEM((t, 2 * t), BF16),
                        pltpu.VMEM((8, 2 * t), F32)],
        compiler_params=pltpu.CompilerParams(dimension_semantics=("arbitrary",) * 2,
                                             vmem_limit_bytes=VMEM_LIMIT_BYTES),
        name="diff_attention",
    )(qt, k, vt, lq1, lk1, lq2, lk2, sub_g)


def _mix_ffn_kernel(x_ref, o_ref, wo_ref, g_ref, wg_ref, wu_ref, wd_ref, out_ref):
    x1 = x_ref[...] + jnp.dot(o_ref[...], wo_ref[...], preferred_element_type=F32)
    h = _rms(x1, g_ref[...]).astype(BF16)
    acc = x1
    for lo, hi in FFN_CHUNKS:
        g = jnp.dot(h, wg_ref[:, lo:hi], preferred_element_type=F32)
        u = jnp.dot(h, wu_ref[:, lo:hi], preferred_element_type=F32)
        a = (g * (1.0 / (1.0 + jnp.exp(-g))) * u).astype(BF16)
        acc = acc + jnp.dot(a, wd_ref[lo:hi, :], preferred_element_type=F32)
    out_ref[...] = acc


def _mix_ffn(x2, o, wo, gain, wg, wu, wd):
    n, d = x2.shape
    tm = TOKEN_TILE
    assert FFN_CHUNKS[-1][1] == wg.shape[1]
    row_spec = pl.BlockSpec((tm, d), lambda i: (i, 0))
    return pl.pallas_call(
        _mix_ffn_kernel,
        grid=(n // tm,),
        in_specs=[row_spec, row_spec, _const_spec(wo.shape), _const_spec((1, d)),
                  _const_spec(wg.shape), _const_spec(wu.shape), _const_spec(wd.shape)],
        out_specs=row_spec,
        out_shape=jax.ShapeDtypeStruct((n, d), F32),
        compiler_params=pltpu.CompilerParams(dimension_semantics=("arbitrary",),
                                             vmem_limit_bytes=VMEM_LIMIT_BYTES),
        name="mix_ffn",
    )(x2, o, wo, gain, wg, wu, wd)


def _rope_tables(positions):
    inv_freq = ROPE_THETA ** (-jnp.arange(0, ROT_DIM, 2, dtype=F32) / ROT_DIM)
    ang = positions.astype(F32)[:, None, :] * inv_freq[None, :, None]
    return jnp.cos(ang), jnp.sin(ang)


def kernel(x, positions, attn_norm, w_in, w_out, q_norm, k_norm, lambda_q1, lambda_k1, lambda_q2,
           lambda_k2, sub_norm, ffn_norm, w_gate, w_up, w_down):
    batch, seq, d = x.shape
    depth = w_in.shape[0]
    assert seq % TOKEN_TILE == 0 and seq % SB_Q_TILE == 0 and seq % DIFF_TILE == 0
    assert d % LANES == 0 and DIFF_V_DIM == LANES and DIFF_TILE % CHUNK == 0
    x2 = x.reshape(batch * seq, d)
    tables = _rope_tables(positions)
    w_in_t = jnp.swapaxes(w_in, 1, 2).astype(BF16)
    for i in range(depth):
        gain = attn_norm[i].reshape(1, d)
        if i % N_MIXERS == 0:
            qt, k, vt = _qkv_proj(x2, gain, w_in_t[i], batch, seq, SB_HEAD_DIM ** -0.5 * LOG2E)
            o = _sb_attention(qt, k, vt, batch, seq)
        else:
            j = i // N_MIXERS
            qg = jnp.broadcast_to(q_norm[j].reshape(-1, 1), (DIFF_HEAD_DIM, LANES))
            kg = jnp.broadcast_to(k_norm[j].reshape(-1, 1), (DIFF_HEAD_DIM, LANES))
            qt, k, vt = _qkv_proj(x2, gain, w_in_t[i], batch, seq, DIFF_HEAD_DIM ** -0.5 * LOG2E,
                                  (qg, kg) + tables)
            lam_init = 0.8 - 0.6 * math.exp(-0.3 * i)
            o = _diff_attention(qt, k, vt, lambda_q1[j].reshape(1, -1), lambda_k1[j].reshape(1, -1),
                                lambda_q2[j].reshape(1, -1), lambda_k2[j].reshape(1, -1),
                                sub_norm[j].reshape(1, -1), batch, seq, lam_init)
        x2 = _mix_ffn(x2, o, w_out[i].astype(BF16), ffn_norm[i].reshape(1, d),
                      w_gate[i].astype(BF16), w_up[i].astype(BF16), w_down[i].astype(BF16))
    return x2.reshape(batch, seq, d)
```

```python
import functools
import math

import jax
import jax.numpy as jnp
from jax import lax
from jax.experimental import pallas as pl
from jax.experimental.pallas import tpu as pltpu

F32 = jnp.float32
BF16 = jnp.bfloat16

N_MIXERS = 2
SB_HEAD_DIM = 64
DIFF_HEAD_DIM = 64
DIFF_V_DIM = 2 * DIFF_HEAD_DIM
CHUNK = 64
ROPE_THETA = 500000.0
ROT_DIM = DIFF_HEAD_DIM // 4
EPS = 1e-6
NEG_INF = -1e30

LANES = 128
VMEM_LIMIT_BYTES = 56 * 1024 * 1024

TOKEN_TILE = 512
SB_Q_TILE = 256
SB_SUB = 128
SB_PRE_SUBS = 2
SB_NEXT_SUBS = 2
DIFF_TILE = 512
DIFF_OUT_GROUP = 4
FFN_CHUNKS = ((0, 1024), (1024, 2048), (2048, 2816))

LOG2E = 1.4426950408889634
F32_EXP2_UNDERFLOW = -150.0


def _rms(x, gain):
    ms = jnp.mean(x * x, axis=-1, keepdims=True)
    return x * lax.rsqrt(ms + EPS) * gain


def _const_spec(shape):
    nd = len(shape)
    return pl.BlockSpec(shape, lambda *_: (0,) * nd, pipeline_mode=pl.Buffered(1))


def _layer_spec(stacked, layer):
    return pl.BlockSpec((1,) + stacked.shape[1:], lambda *_: (layer, 0, 0),
                        pipeline_mode=pl.Buffered(1))


def _project_t(wt_ref, h, part, d_model):
    w = wt_ref[0, part * d_model:(part + 1) * d_model, :]
    return lax.dot_general(w, h, (((1,), (1,)), ((), ())), preferred_element_type=F32)


def _qkv_kernel(x_ref, g_ref, wt_ref, qt_ref, k_ref, vt_ref, *, d_model, scale):
    h = _rms(x_ref[...], g_ref[...]).astype(BF16)
    kt = _project_t(wt_ref, h, 1, d_model)
    for c in range(d_model // LANES):
        sl = slice(c * LANES, (c + 1) * LANES)
        k_ref[:, sl] = kt[sl, :].T.astype(BF16)
    qt_ref[0] = (_project_t(wt_ref, h, 0, d_model) * scale).astype(BF16)
    vt_ref[0] = _project_t(wt_ref, h, 2, d_model).astype(BF16)


def _qkv_diff_kernel(x_ref, g_ref, wt_ref, qg_ref, kg_ref, cost_ref, sint_ref,
                     qt_ref, k_ref, vt_ref, *, d_model, scale):
    h = _rms(x_ref[...], g_ref[...]).astype(BF16)
    tm = h.shape[0]
    cos_t, sin_t = cost_ref[0], sint_ref[0]
    half = ROT_DIM // 2

    project = functools.partial(_project_t, wt_ref, h, d_model=d_model)

    def norm_rope(xg, gain):
        ms = jnp.mean(xg * xg, axis=0, keepdims=True)
        y = xg * lax.rsqrt(ms + EPS) * gain
        x1, x2 = y[:half], y[half:ROT_DIM]
        return jnp.concatenate([x1 * cos_t - x2 * sin_t, x2 * cos_t + x1 * sin_t, y[ROT_DIM:]],
                               axis=0)

    def slabs(xt, gain):
        per_slab = LANES // DIFF_HEAD_DIM
        for c in range(d_model // LANES):
            parts = [norm_rope(xt[g * DIFF_HEAD_DIM:(g + 1) * DIFF_HEAD_DIM, :], gain)
                     for g in range(c * per_slab, (c + 1) * per_slab)]
            yield slice(c * LANES, (c + 1) * LANES), jnp.concatenate(parts, axis=0)

    for sl, y in slabs(project(1), jnp.tile(kg_ref[...], (1, tm // LANES))):
        k_ref[:, sl] = y.T.astype(BF16)
    for sl, y in slabs(project(0), jnp.tile(qg_ref[...], (1, tm // LANES))):
        qt_ref[0, sl, :] = (y * scale).astype(BF16)
    vt_ref[0] = project(2).astype(BF16)


def _qkv_proj(x2, gain, wt, layer, batch, seq, scale, diff_args=None):
    n, d = x2.shape
    tm = TOKEN_TILE
    tiles_per_seq = seq // tm
    row_spec = pl.BlockSpec((tm, d), lambda i: (i, 0))
    t_spec = pl.BlockSpec((1, d, tm), lambda i: (i // tiles_per_seq, 0, i % tiles_per_seq))
    row_shape = jax.ShapeDtypeStruct((n, d), BF16)
    t_shape = jax.ShapeDtypeStruct((batch, d, seq), BF16)
    if diff_args is None:
        body = functools.partial(_qkv_kernel, d_model=d, scale=scale)
        name = "qkv_sb"
        in_specs = [row_spec, _const_spec((1, d)), _layer_spec(wt, layer)]
        args = [x2, gain, wt]
    else:
        qg, kg, cos_t, sin_t = diff_args
        tabt_spec = pl.BlockSpec((1, ROT_DIM // 2, tm),
                                 lambda i: (i // tiles_per_seq, 0, i % tiles_per_seq))
        body = functools.partial(_qkv_diff_kernel, d_model=d, scale=scale)
        name = "qkv_diff"
        in_specs = [row_spec, _const_spec((1, d)), _layer_spec(wt, layer),
                    _const_spec(qg.shape), _const_spec(kg.shape), tabt_spec, tabt_spec]
        args = [x2, gain, wt, qg, kg, cos_t, sin_t]
    return pl.pallas_call(
        body,
        grid=(n // tm,),
        in_specs=in_specs,
        out_specs=[t_spec, row_spec, t_spec],
        out_shape=[t_shape, row_shape, t_shape],
        compiler_params=pltpu.CompilerParams(dimension_semantics=("arbitrary",),
                                             vmem_limit_bytes=VMEM_LIMIT_BYTES),
        name=name,
    )(*args)


def _sb_attn_kernel(qt_ref, k_ref, vt_ref, o_ref, z_ref, w_ref, w0_ref, c_ref, accx_ref, call_ref):
    tq, sub = SB_Q_TILE, SB_SUB
    half = tq // 2
    gw = 2 * half
    n_tiles = k_ref.shape[0] // tq
    n_sub = SB_PRE_SUBS + tq // sub
    r_u = lax.broadcasted_iota(jnp.int32, (sub, 2 * sub), 0)
    c_u = lax.broadcasted_iota(jnp.int32, (sub, 2 * sub), 1) % sub
    later2 = jnp.where(c_u > r_u, 1.0, 0.0).astype(BF16)
    tri = (lax.broadcasted_iota(jnp.int32, (sub, gw), 0)
           < lax.broadcasted_iota(jnp.int32, (sub, gw), 1) % half)

    def q4x(t):
        qt = qt_ref[0, :, pl.ds(pl.multiple_of(t * tq, tq), tq)]
        row = lax.broadcasted_iota(jnp.int32, (LANES, half), 0)
        zero = jnp.zeros((LANES, half), qt.dtype)
        parts = []
        for g in range(2):
            qg = qt[:, g * half:(g + 1) * half]
            parts += [jnp.where(row < SB_HEAD_DIM, qg, zero), jnp.where(row >= SB_HEAD_DIM, qg, zero)]
        return jnp.concatenate(parts, axis=1)

    def first_block(t):
        return jnp.maximum(t * (tq // sub) - SB_PRE_SUBS, 0)

    def first_key(t):
        return pl.multiple_of(first_block(t) * sub, sub)

    def scores(t):
        kb = k_ref[pl.ds(first_key(t), n_sub * sub), :]
        return jnp.dot(kb, q4x(t), preferred_element_type=F32)

    def sub_block(z, c, mask):
        if mask is not None:
            z = jnp.where(mask, z, NEG_INF)
        nabs = pltpu.bitcast(pltpu.bitcast(z, jnp.uint32) | jnp.uint32(0x80000000), F32)
        sp = jnp.log(1.0 + jnp.exp2(nabs)) * LOG2E
        ls = jnp.minimum(z, 0.0) - sp
        lk = ls - z
        hi = lk.astype(BF16)
        lo = (lk - hi.astype(F32)).astype(BF16)
        after = jnp.dot(later2, jnp.concatenate([hi, lo], axis=0), preferred_element_type=F32)
        w = jnp.exp2(ls + after + c)
        return w.astype(BF16), c + after[0:1, :] + lk[0:1, :]

    def first_step(first_sb, dst_ref, skip_first):
        zero_c = jnp.zeros((1, gw), F32)
        lo_rows = slice(first_sb * sub, (first_sb + 1) * sub)
        hi_rows = slice((first_sb + 1) * sub, (first_sb + 2) * sub)
        wb, cb = sub_block(z_ref[hi_rows, gw:], zero_c, tri)
        dst_ref[hi_rows, gw:] = wb
        wa, ca = sub_block(z_ref[lo_rows, :gw], zero_c, tri)
        wb, cb = sub_block(z_ref[lo_rows, gw:], cb, None)
        dst_ref[lo_rows, :gw] = wa
        dst_ref[lo_rows, gw:] = wb
        c = jnp.concatenate([ca, cb], axis=1)
        for sb in reversed(range(1 if skip_first else 0, first_sb)):
            w, c = sub_block(z_ref[sb * sub:(sb + 1) * sub, :], c, None)
            dst_ref[sb * sub:(sb + 1) * sub, :] = w
        if skip_first:
            wa, ca = sub_block(z_ref[0:sub, :gw], c[:, :gw], None)
            dst_ref[0:sub, :gw] = wa
            c = jnp.concatenate([ca, c[:, gw:]], axis=1)
        c_ref[...] = c
        return jnp.max(c)

    def extra_steps(t, c_max):
        def cond(carry):
            j, c_max = carry
            return (j > 0) & (c_max > F32_EXP2_UNDERFLOW)

        def body(carry):
            j, _ = carry
            j = j - SB_NEXT_SUBS
            start = pl.multiple_of(j * sub, sub)
            kb = k_ref[pl.ds(start, SB_NEXT_SUBS * sub), :]
            z_all = jnp.dot(kb, q4x(t), preferred_element_type=F32)
            c = c_ref[...]
            ws = [None] * SB_NEXT_SUBS
            for sb in reversed(range(SB_NEXT_SUBS)):
                ws[sb], c = sub_block(z_all[sb * sub:(sb + 1) * sub], c, None)
            vtb = vt_ref[0, :, pl.ds(start, SB_NEXT_SUBS * sub)]
            accx_ref[...] += jnp.dot(vtb, jnp.concatenate(ws, axis=0), preferred_element_type=F32)
            c_ref[...] = c
            return j, jnp.max(c)

        lax.while_loop(cond, body, (first_block(t), c_max))

    def write_out(t, acc):
        row_o = lax.broadcasted_iota(jnp.int32, (LANES, half), 0)
        out_t = jnp.concatenate(
            [jnp.where(row_o < SB_HEAD_DIM, acc[:, g * gw:g * gw + half],
                       acc[:, g * gw + half:(g + 1) * gw]) for g in range(2)], axis=1)
        o_ref[pl.ds(pl.multiple_of(t * tq, tq), tq), :] = out_t.T.astype(o_ref.dtype)

    def finish(t, with_extra):
        vtb = vt_ref[0, :, pl.ds(first_key(t), n_sub * sub)]
        acc = jnp.dot(vtb, w_ref[...], preferred_element_type=F32)
        if with_extra:
            acc = acc + accx_ref[...]
        write_out(t, acc)

    w_ref[...] = jnp.zeros_like(w_ref)
    w0_ref[...] = jnp.zeros_like(w0_ref)
    z_ref[...] = scores(0)
    first_step(0, w0_ref, False)
    z_ref[...] = scores(1)
    write_out(0, jnp.dot(vt_ref[0, :, 0:tq], w0_ref[...], preferred_element_type=F32))

    def body(t, carry):
        finish(jnp.maximum(t - 1, 1), False)
        z_next = scores(jnp.minimum(t + 1, n_tiles - 1))
        first_step(SB_PRE_SUBS, w_ref, True)
        call_ref[t] = c_ref[...]
        z_ref[...] = z_next
        return carry

    lax.fori_loop(1, n_tiles, body, 0)
    finish(n_tiles - 1, False)

    @pl.when(jnp.max(call_ref[1:]) > F32_EXP2_UNDERFLOW)
    def _():
        def redo(t, carry):
            @pl.when(jnp.max(call_ref[t]) > F32_EXP2_UNDERFLOW)
            def _():
                z_ref[...] = scores(t)
                accx_ref[...] = jnp.zeros_like(accx_ref)
                extra_steps(t, first_step(SB_PRE_SUBS, w_ref, False))
                finish(t, True)
            return carry

        lax.fori_loop(1, n_tiles, redo, 0)


def _sb_attention(qt, k, vt, batch, seq):
    n, d = k.shape
    tq = SB_Q_TILE
    rows = SB_PRE_SUBS * SB_SUB + tq
    assert seq >= rows and seq // tq >= 2
    return pl.pallas_call(
        _sb_attn_kernel,
        grid=(batch, d // LANES),
        in_specs=[pl.BlockSpec((1, LANES, seq), lambda b, p: (b, p, 0)),
                  pl.BlockSpec((seq, LANES), lambda b, p: (b, p)),
                  pl.BlockSpec((1, LANES, seq), lambda b, p: (b, p, 0))],
        out_specs=pl.BlockSpec((seq, LANES), lambda b, p: (b, p)),
        out_shape=jax.ShapeDtypeStruct((n, d), BF16),
        scratch_shapes=[pltpu.VMEM((rows, 2 * tq), F32), pltpu.VMEM((rows, 2 * tq), BF16),
                        pltpu.VMEM((tq, 2 * tq), BF16),
                        pltpu.VMEM((1, 2 * tq), F32), pltpu.VMEM((LANES, 2 * tq), F32),
                        pltpu.VMEM((seq // tq, 1, 2 * tq), F32)],
        compiler_params=pltpu.CompilerParams(dimension_semantics=("arbitrary",) * 2,
                                             vmem_limit_bytes=VMEM_LIMIT_BYTES),
        name="sb_attention",
    )(qt, k, vt)


def _diff_attn_kernel(qt_ref, k_ref, vt_ref, lq1_ref, lk1_ref, lq2_ref, lk2_ref, sub_ref, o_ref,
                      m_ref, l_ref, acc_ref, alpha_ref, s_ref, p_ref, smax_ref, *, lam_init):
    t = DIFF_TILE
    n_tiles = k_ref.shape[0] // t

    def q2x(i):
        qt = qt_ref[0, :, pl.ds(pl.multiple_of(i * t, t), t)]
        row = lax.broadcasted_iota(jnp.int32, qt.shape, 0)
        zero = jnp.zeros_like(qt)
        return jnp.concatenate([jnp.where(row < DIFF_HEAD_DIM, qt, zero),
                                jnp.where(row >= DIFF_HEAD_DIM, qt, zero)], axis=1)

    def scores(i, j):
        kb = k_ref[pl.ds(pl.multiple_of(j * t, t), t), :]
        return jnp.dot(kb, q2x(i), preferred_element_type=F32)

    def stage_scores(s):
        s_ref[...] = s
        smax_ref[...] = jnp.max(s.reshape(t // 8, 8, 2 * t), axis=0)

    def softmax_first(i):
        s = s_ref[...]
        key_chunk = lax.broadcasted_iota(jnp.int32, s.shape, 0) // CHUNK
        q_chunk = (lax.broadcasted_iota(jnp.int32, s.shape, 1) % t) // CHUNK
        s = jnp.where(key_chunk <= q_chunk, s, NEG_INF)
        m_new = jnp.max(s, axis=0, keepdims=True)
        p = jnp.exp2(s - m_new)
        l_ref[i] = jnp.sum(p, axis=0, keepdims=True)
        p_ref[...] = p.astype(BF16)
        m_ref[i] = m_new

    def softmax_next(i):
        s = s_ref[...]
        m_old = m_ref[i]
        m_new = jnp.maximum(m_old, jnp.max(smax_ref[...], axis=0, keepdims=True))
        p = jnp.exp2(s - m_new)
        alpha = jnp.exp2(m_old - m_new)
        l_ref[i] = alpha * l_ref[i] + jnp.sum(p, axis=0, keepdims=True)
        p_ref[...] = p.astype(BF16)
        alpha_ref[...] = alpha
        m_ref[i] = m_new

    def pv(j):
        vtb = vt_ref[0, :, pl.ds(pl.multiple_of(j * t, t), t)]
        return jnp.dot(vtb, p_ref[...], preferred_element_type=F32)

    s_ref[...] = scores(0, 0)
    softmax_first(0)
    s_ref[...] = scores(1, 1)

    def diag_body(i, carry):
        acc_ref[i - 1] = pv(i - 1)
        nxt = jnp.minimum(i + 1, n_tiles - 1)
        s_next = scores(nxt, nxt)
        softmax_first(i)
        s_ref[...] = s_next
        return carry

    lax.fori_loop(1, n_tiles, diag_body, 0)
    acc_ref[n_tiles - 1] = pv(n_tiles - 1)

    def next_pair(i, j):
        wrap = j + 1 >= i
        return jnp.where(wrap, i + 1, i), jnp.where(wrap, 0, j + 1)

    stage_scores(scores(1, 0))
    softmax_next(1)
    i1, j1 = next_pair(1, 0)
    i1c = jnp.minimum(i1, n_tiles - 1)
    stage_scores(scores(i1c, jnp.minimum(j1, i1c - 1)))
    n_pairs = n_tiles * (n_tiles - 1) // 2

    def pair_body(n, carry):
        ip, jp, i, j = carry
        acc_ref[ip] = alpha_ref[...] * acc_ref[ip] + pv(jp)
        i2, j2 = next_pair(i, j)
        i2c = jnp.minimum(i2, n_tiles - 1)
        s_next = scores(i2c, jnp.minimum(j2, i2c - 1))
        softmax_next(i)
        stage_scores(s_next)
        return i, j, i2, j2

    ip, jp, _, _ = lax.fori_loop(1, n_pairs, pair_body, (1, 0, i1, j1))
    acc_ref[ip] = alpha_ref[...] * acc_ref[ip] + pv(jp)

    lam = (jnp.exp(jnp.sum(lq1_ref[...] * lk1_ref[...], axis=-1, keepdims=True))
           - jnp.exp(jnp.sum(lq2_ref[...] * lk2_ref[...], axis=-1, keepdims=True)) + lam_init)

    group = DIFF_OUT_GROUP if n_tiles % DIFF_OUT_GROUP == 0 else 1

    def out_body(ig, carry):
        for u in range(group):
            i = ig * group + u
            on = acc_ref[i] / l_ref[i]
            o = on[:, :t] - lam * on[:, t:]
            ms = jnp.mean(o * o, axis=0, keepdims=True)
            o = o * lax.rsqrt(ms + EPS)
            o = o.T * sub_ref[...] * (1.0 - lam_init)
            o_ref[pl.ds(pl.multiple_of(i * t, t), t), :] = o.astype(o_ref.dtype)
        return carry

    lax.fori_loop(0, n_tiles // group, out_body, 0)


def _diff_attention(qt, k, vt, lq1, lk1, lq2, lk2, sub_g, batch, seq, lam_init):
    n, d = k.shape
    t = DIFF_TILE
    n_tiles = seq // t
    assert n_tiles >= 2
    small = _const_spec((1, DIFF_HEAD_DIM))
    return pl.pallas_call(
        functools.partial(_diff_attn_kernel, lam_init=lam_init),
        grid=(batch, d // LANES),
        in_specs=[pl.BlockSpec((1, LANES, seq), lambda b, h: (b, h, 0)),
                  pl.BlockSpec((seq, LANES), lambda b, h: (b, h)),
                  pl.BlockSpec((1, LANES, seq), lambda b, h: (b, h, 0)),
                  small, small, small, small, _const_spec((1, LANES))],
        out_specs=pl.BlockSpec((seq, LANES), lambda b, h: (b, h)),
        out_shape=jax.ShapeDtypeStruct((n, d), BF16),
        scratch_shapes=[pltpu.VMEM((n_tiles, 1, 2 * t), F32), pltpu.VMEM((n_tiles, 1, 2 * t), F32),
                        pltpu.VMEM((n_tiles, LANES, 2 * t), F32), pltpu.VMEM((1, 2 * t), F32),
                        pltpu.VMEM((t, 2 * t), F32), pltpu.VMEM((t, 2 * t), BF16),
                        pltpu.VMEM((8, 2 * t), F32)],
        compiler_params=pltpu.CompilerParams(dimension_semantics=("arbitrary",) * 2,
                                             vmem_limit_bytes=VMEM_LIMIT_BYTES),
        name="diff_attention",
    )(qt, k, vt, lq1, lk1, lq2, lk2, sub_g)


def _mix_ffn_kernel(x_ref, o_ref, wo_ref, g_ref, wg_ref, wu_ref, wd_ref, out_ref):
    x1 = x_ref[...] + jnp.dot(o_ref[...], wo_ref[0], preferred_element_type=F32)
    h = _rms(x1, g_ref[...]).astype(BF16)
    acc = x1
    for lo, hi in FFN_CHUNKS:
        g = jnp.dot(h, wg_ref[0, :, lo:hi], preferred_element_type=F32)
        u = jnp.dot(h, wu_ref[0, :, lo:hi], preferred_element_type=F32)
        a = (g * (1.0 / (1.0 + jnp.exp(-g))) * u).astype(BF16)
        acc = acc + jnp.dot(a, wd_ref[0, lo:hi, :], preferred_element_type=F32)
    out_ref[...] = acc


def _mix_ffn(x2, o, gain, wo, wg, wu, wd, layer):
    n, d = x2.shape
    tm = TOKEN_TILE
    assert FFN_CHUNKS[-1][1] == wg.shape[2]
    row_spec = pl.BlockSpec((tm, d), lambda i: (i, 0))
    return pl.pallas_call(
        _mix_ffn_kernel,
        grid=(n // tm,),
        in_specs=[row_spec, row_spec, _layer_spec(wo, layer), _const_spec((1, d)),
                  _layer_spec(wg, layer), _layer_spec(wu, layer), _layer_spec(wd, layer)],
        out_specs=row_spec,
        out_shape=jax.ShapeDtypeStruct((n, d), F32),
        compiler_params=pltpu.CompilerParams(dimension_semantics=("arbitrary",),
                                             vmem_limit_bytes=VMEM_LIMIT_BYTES),
        name="mix_ffn",
    )(x2, o, wo, gain, wg, wu, wd)


def _rope_tables(positions):
    inv_freq = ROPE_THETA ** (-jnp.arange(0, ROT_DIM, 2, dtype=F32) / ROT_DIM)
    ang = positions.astype(F32)[:, None, :] * inv_freq[None, :, None]
    return jnp.cos(ang), jnp.sin(ang)


def kernel(x, positions, attn_norm, w_in, w_out, q_norm, k_norm, lambda_q1, lambda_k1, lambda_q2,
           lambda_k2, sub_norm, ffn_norm, w_gate, w_up, w_down):
    batch, seq, d = x.shape
    depth = w_in.shape[0]
    assert seq % TOKEN_TILE == 0 and seq % SB_Q_TILE == 0 and seq % DIFF_TILE == 0
    assert d % LANES == 0 and DIFF_V_DIM == LANES and DIFF_TILE % CHUNK == 0
    x2 = x.reshape(batch * seq, d)
    tables = _rope_tables(positions)
    w_in_t = jnp.swapaxes(w_in, 1, 2).astype(BF16)
    wo, wg, wu, wd = (w.astype(BF16) for w in (w_out, w_gate, w_up, w_down))
    for i in range(depth):
        gain = attn_norm[i].reshape(1, d)
        if i % N_MIXERS == 0:
            qt, k, vt = _qkv_proj(x2, gain, w_in_t, i, batch, seq, SB_HEAD_DIM ** -0.5 * LOG2E)
            o = _sb_attention(qt, k, vt, batch, seq)
        else:
            j = i // N_MIXERS
            qg = jnp.broadcast_to(q_norm[j].reshape(-1, 1), (DIFF_HEAD_DIM, LANES))
            kg = jnp.broadcast_to(k_norm[j].reshape(-1, 1), (DIFF_HEAD_DIM, LANES))
            qt, k, vt = _qkv_proj(x2, gain, w_in_t, i, batch, seq, DIFF_HEAD_DIM ** -0.5 * LOG2E,
                                  (qg, kg) + tables)
            lam_init = 0.8 - 0.6 * math.exp(-0.3 * i)
            o = _diff_attention(qt, k, vt, lambda_q1[j].reshape(1, -1), lambda_k1[j].reshape(1, -1),
                                lambda_q2[j].reshape(1, -1), lambda_k2[j].reshape(1, -1),
                                sub_norm[j].reshape(1, -1), batch, seq, lam_init)
        x2 = _mix_ffn(x2, o, ffn_norm[i].reshape(1, d), wo, wg, wu, wd, i)
    return x2.reshape(batch, seq, d)
```

```python
import functools
import math

import jax
import jax.numpy as jnp
from jax import lax
from jax.experimental import pallas as pl
from jax.experimental.pallas import tpu as pltpu

F32 = jnp.float32
BF16 = jnp.bfloat16

N_MIXERS = 2
SB_HEAD_DIM = 64
DIFF_HEAD_DIM = 64
DIFF_V_DIM = 2 * DIFF_HEAD_DIM
CHUNK = 64
ROPE_THETA = 500000.0
ROT_DIM = DIFF_HEAD_DIM // 4
EPS = 1e-6
NEG_INF = -1e30

LANES = 128
VMEM_LIMIT_BYTES = 56 * 1024 * 1024

QKV_TILE = 1024
TOKEN_TILE = 512
SB_Q_TILE = 256
SB_SUB = 128
SB_PRE_SUBS = 2
SB_NEXT_SUBS = 2
DIFF_TILE = 512
DIFF_OUT_GROUP = 4
FFN_CHUNKS = ((0, 1024), (1024, 2048), (2048, 2816))

LOG2E = 1.4426950408889634
F32_EXP2_UNDERFLOW = -150.0


def _rms(x, gain):
    ms = jnp.mean(x * x, axis=-1, keepdims=True)
    return x * lax.rsqrt(ms + EPS) * gain


def _const_spec(shape):
    nd = len(shape)
    return pl.BlockSpec(shape, lambda *_: (0,) * nd, pipeline_mode=pl.Buffered(1))


def _layer_spec(stacked, layer):
    return pl.BlockSpec((1,) + stacked.shape[1:], lambda *_: (layer, 0, 0),
                        pipeline_mode=pl.Buffered(1))


def _project_t(wt_ref, h, part, d_model):
    w = wt_ref[0, part * d_model:(part + 1) * d_model, :]
    return lax.dot_general(w, h, (((1,), (1,)), ((), ())), preferred_element_type=F32)


def _qkv_kernel(x_ref, g_ref, wt_ref, qt_ref, k_ref, vt_ref, *, d_model, scale):
    h = _rms(x_ref[...], g_ref[...]).astype(BF16)
    kt = _project_t(wt_ref, h, 1, d_model)
    for c in range(d_model // LANES):
        sl = slice(c * LANES, (c + 1) * LANES)
        k_ref[:, sl] = kt[sl, :].T.astype(BF16)
    qt_ref[0] = (_project_t(wt_ref, h, 0, d_model) * scale).astype(BF16)
    vt_ref[0] = _project_t(wt_ref, h, 2, d_model).astype(BF16)


def _qkv_diff_kernel(x_ref, g_ref, wt_ref, qg_ref, kg_ref, cost_ref, sint_ref,
                     qt_ref, k_ref, vt_ref, *, d_model, scale):
    h = _rms(x_ref[...], g_ref[...]).astype(BF16)
    tm = h.shape[0]
    cos_t, sin_t = cost_ref[0], sint_ref[0]
    half = ROT_DIM // 2

    project = functools.partial(_project_t, wt_ref, h, d_model=d_model)

    def norm_rope(xg, gain):
        ms = jnp.mean(xg * xg, axis=0, keepdims=True)
        y = xg * lax.rsqrt(ms + EPS) * gain
        x1, x2 = y[:half], y[half:ROT_DIM]
        return jnp.concatenate([x1 * cos_t - x2 * sin_t, x2 * cos_t + x1 * sin_t, y[ROT_DIM:]],
                               axis=0)

    def slabs(xt, gain):
        per_slab = LANES // DIFF_HEAD_DIM
        for c in range(d_model // LANES):
            parts = [norm_rope(xt[g * DIFF_HEAD_DIM:(g + 1) * DIFF_HEAD_DIM, :], gain)
                     for g in range(c * per_slab, (c + 1) * per_slab)]
            yield slice(c * LANES, (c + 1) * LANES), jnp.concatenate(parts, axis=0)

    for sl, y in slabs(project(1), jnp.tile(kg_ref[...], (1, tm // LANES))):
        k_ref[:, sl] = y.T.astype(BF16)
    for sl, y in slabs(project(0), jnp.tile(qg_ref[...], (1, tm // LANES))):
        qt_ref[0, sl, :] = (y * scale).astype(BF16)
    vt_ref[0] = project(2).astype(BF16)


def _qkv_proj(x2, gain, wt, layer, batch, seq, scale, diff_args=None):
    n, d = x2.shape
    tm = QKV_TILE
    tiles_per_seq = seq // tm
    row_spec = pl.BlockSpec((tm, d), lambda i: (i, 0))
    t_spec = pl.BlockSpec((1, d, tm), lambda i: (i // tiles_per_seq, 0, i % tiles_per_seq))
    row_shape = jax.ShapeDtypeStruct((n, d), BF16)
    t_shape = jax.ShapeDtypeStruct((batch, d, seq), BF16)
    if diff_args is None:
        body = functools.partial(_qkv_kernel, d_model=d, scale=scale)
        name = "qkv_sb"
        in_specs = [row_spec, _const_spec((1, d)), _layer_spec(wt, layer)]
        args = [x2, gain, wt]
    else:
        qg, kg, cos_t, sin_t = diff_args
        tabt_spec = pl.BlockSpec((1, ROT_DIM // 2, tm),
                                 lambda i: (i // tiles_per_seq, 0, i % tiles_per_seq))
        body = functools.partial(_qkv_diff_kernel, d_model=d, scale=scale)
        name = "qkv_diff"
        in_specs = [row_spec, _const_spec((1, d)), _layer_spec(wt, layer),
                    _const_spec(qg.shape), _const_spec(kg.shape), tabt_spec, tabt_spec]
        args = [x2, gain, wt, qg, kg, cos_t, sin_t]
    return pl.pallas_call(
        body,
        grid=(n // tm,),
        in_specs=in_specs,
        out_specs=[t_spec, row_spec, t_spec],
        out_shape=[t_shape, row_shape, t_shape],
        compiler_params=pltpu.CompilerParams(dimension_semantics=("arbitrary",),
                                             vmem_limit_bytes=VMEM_LIMIT_BYTES),
        name=name,
    )(*args)


def _sb_attn_kernel(qt_ref, k_ref, vt_ref, o_ref, z_ref, w_ref, w0_ref, c_ref, accx_ref, call_ref):
    tq, sub = SB_Q_TILE, SB_SUB
    half = tq // 2
    gw = 2 * half
    n_tiles = k_ref.shape[0] // tq
    n_sub = SB_PRE_SUBS + tq // sub
    r_u = lax.broadcasted_iota(jnp.int32, (sub, 2 * sub), 0)
    c_u = lax.broadcasted_iota(jnp.int32, (sub, 2 * sub), 1) % sub
    later2 = jnp.where(c_u > r_u, 1.0, 0.0).astype(BF16)
    tri = (lax.broadcasted_iota(jnp.int32, (sub, gw), 0)
           < lax.broadcasted_iota(jnp.int32, (sub, gw), 1) % half)

    def q4x(t):
        qt = qt_ref[0, :, pl.ds(pl.multiple_of(t * tq, tq), tq)]
        row = lax.broadcasted_iota(jnp.int32, (LANES, half), 0)
        zero = jnp.zeros((LANES, half), qt.dtype)
        parts = []
        for g in range(2):
            qg = qt[:, g * half:(g + 1) * half]
            parts += [jnp.where(row < SB_HEAD_DIM, qg, zero), jnp.where(row >= SB_HEAD_DIM, qg, zero)]
        return jnp.concatenate(parts, axis=1)

    def first_block(t):
        return jnp.maximum(t * (tq // sub) - SB_PRE_SUBS, 0)

    def first_key(t):
        return pl.multiple_of(first_block(t) * sub, sub)

    def scores(t):
        kb = k_ref[pl.ds(first_key(t), n_sub * sub), :]
        return jnp.dot(kb, q4x(t), preferred_element_type=F32)

    def sub_block(z, c, mask):
        if mask is not None:
            z = jnp.where(mask, z, NEG_INF)
        nabs = pltpu.bitcast(pltpu.bitcast(z, jnp.uint32) | jnp.uint32(0x80000000), F32)
        sp = jnp.log(1.0 + jnp.exp2(nabs)) * LOG2E
        ls = jnp.minimum(z, 0.0) - sp
        lk = ls - z
        hi = lk.astype(BF16)
        lo = (lk - hi.astype(F32)).astype(BF16)
        after = jnp.dot(later2, jnp.concatenate([hi, lo], axis=0), preferred_element_type=F32)
        w = jnp.exp2(ls + after + c)
        return w.astype(BF16), c + after[0:1, :] + lk[0:1, :]

    def first_step(first_sb, dst_ref, skip_first):
        zero_c = jnp.zeros((1, gw), F32)
        lo_rows = slice(first_sb * sub, (first_sb + 1) * sub)
        hi_rows = slice((first_sb + 1) * sub, (first_sb + 2) * sub)
        wb, cb = sub_block(z_ref[hi_rows, gw:], zero_c, tri)
        dst_ref[hi_rows, gw:] = wb
        wa, ca = sub_block(z_ref[lo_rows, :gw], zero_c, tri)
        wb, cb = sub_block(z_ref[lo_rows, gw:], cb, None)
        dst_ref[lo_rows, :gw] = wa
        dst_ref[lo_rows, gw:] = wb
        c = jnp.concatenate([ca, cb], axis=1)
        for sb in reversed(range(1 if skip_first else 0, first_sb)):
            w, c = sub_block(z_ref[sb * sub:(sb + 1) * sub, :], c, None)
            dst_ref[sb * sub:(sb + 1) * sub, :] = w
        if skip_first:
            wa, ca = sub_block(z_ref[0:sub, :gw], c[:, :gw], None)
            dst_ref[0:sub, :gw] = wa
            c = jnp.concatenate([ca, c[:, gw:]], axis=1)
        c_ref[...] = c
        return jnp.max(c)

    def extra_steps(t, c_max):
        def cond(carry):
            j, c_max = carry
            return (j > 0) & (c_max > F32_EXP2_UNDERFLOW)

        def body(carry):
            j, _ = carry
            j = j - SB_NEXT_SUBS
            start = pl.multiple_of(j * sub, sub)
            kb = k_ref[pl.ds(start, SB_NEXT_SUBS * sub), :]
            z_all = jnp.dot(kb, q4x(t), preferred_element_type=F32)
            c = c_ref[...]
            ws = [None] * SB_NEXT_SUBS
            for sb in reversed(range(SB_NEXT_SUBS)):
                ws[sb], c = sub_block(z_all[sb * sub:(sb + 1) * sub], c, None)
            vtb = vt_ref[0, :, pl.ds(start, SB_NEXT_SUBS * sub)]
            accx_ref[...] += jnp.dot(vtb, jnp.concatenate(ws, axis=0), preferred_element_type=F32)
            c_ref[...] = c
            return j, jnp.max(c)

        lax.while_loop(cond, body, (first_block(t), c_max))

    def write_out(t, acc):
        row_o = lax.broadcasted_iota(jnp.int32, (LANES, half), 0)
        out_t = jnp.concatenate(
            [jnp.where(row_o < SB_HEAD_DIM, acc[:, g * gw:g * gw + half],
                       acc[:, g * gw + half:(g + 1) * gw]) for g in range(2)], axis=1)
        o_ref[pl.ds(pl.multiple_of(t * tq, tq), tq), :] = out_t.T.astype(o_ref.dtype)

    def finish(t, with_extra):
        vtb = vt_ref[0, :, pl.ds(first_key(t), n_sub * sub)]
        acc = jnp.dot(vtb, w_ref[...], preferred_element_type=F32)
        if with_extra:
            acc = acc + accx_ref[...]
        write_out(t, acc)

    w_ref[...] = jnp.zeros_like(w_ref)
    w0_ref[...] = jnp.zeros_like(w0_ref)
    z_ref[...] = scores(0)
    first_step(0, w0_ref, False)
    z_ref[...] = scores(1)
    write_out(0, jnp.dot(vt_ref[0, :, 0:tq], w0_ref[...], preferred_element_type=F32))

    def body(t, carry):
        finish(jnp.maximum(t - 1, 1), False)
        z_next = scores(jnp.minimum(t + 1, n_tiles - 1))
        first_step(SB_PRE_SUBS, w_ref, True)
        call_ref[t] = c_ref[...]
        z_ref[...] = z_next
        return carry

    lax.fori_loop(1, n_tiles, body, 0)
    finish(n_tiles - 1, False)

    @pl.when(jnp.max(call_ref[1:]) > F32_EXP2_UNDERFLOW)
    def _():
        def redo(t, carry):
            @pl.when(jnp.max(call_ref[t]) > F32_EXP2_UNDERFLOW)
            def _():
                z_ref[...] = scores(t)
                accx_ref[...] = jnp.zeros_like(accx_ref)
                extra_steps(t, first_step(SB_PRE_SUBS, w_ref, False))
                finish(t, True)
            return carry

        lax.fori_loop(1, n_tiles, redo, 0)


def _sb_attention(qt, k, vt, batch, seq):
    n, d = k.shape
    tq = SB_Q_TILE
    rows = SB_PRE_SUBS * SB_SUB + tq
    assert seq >= rows and seq // tq >= 2
    return pl.pallas_call(
        _sb_attn_kernel,
        grid=(batch, d // LANES),
        in_specs=[pl.BlockSpec((1, LANES, seq), lambda b, p: (b, p, 0)),
                  pl.BlockSpec((seq, LANES), lambda b, p: (b, p)),
                  pl.BlockSpec((1, LANES, seq), lambda b, p: (b, p, 0))],
        out_specs=pl.BlockSpec((seq, LANES), lambda b, p: (b, p)),
        out_shape=jax.ShapeDtypeStruct((n, d), BF16),
        scratch_shapes=[pltpu.VMEM((rows, 2 * tq), F32), pltpu.VMEM((rows, 2 * tq), BF16),
                        pltpu.VMEM((tq, 2 * tq), BF16),
                        pltpu.VMEM((1, 2 * tq), F32), pltpu.VMEM((LANES, 2 * tq), F32),
                        pltpu.VMEM((seq // tq, 1, 2 * tq), F32)],
        compiler_params=pltpu.CompilerParams(dimension_semantics=("arbitrary",) * 2,
                                             vmem_limit_bytes=VMEM_LIMIT_BYTES),
        name="sb_attention",
    )(qt, k, vt)


def _diff_attn_kernel(qt_ref, k_ref, vt_ref, lq1_ref, lk1_ref, lq2_ref, lk2_ref, sub_ref, o_ref,
                      m_ref, l_ref, acc_ref, alpha_ref, s_ref, p_ref, smax_ref, *, lam_init):
    t = DIFF_TILE
    n_tiles = k_ref.shape[0] // t

    def q2x(i):
        qt = qt_ref[0, :, pl.ds(pl.multiple_of(i * t, t), t)]
        row = lax.broadcasted_iota(jnp.int32, qt.shape, 0)
        zero = jnp.zeros_like(qt)
        return jnp.concatenate([jnp.where(row < DIFF_HEAD_DIM, qt, zero),
                                jnp.where(row >= DIFF_HEAD_DIM, qt, zero)], axis=1)

    def scores(i, j):
        kb = k_ref[pl.ds(pl.multiple_of(j * t, t), t), :]
        return jnp.dot(kb, q2x(i), preferred_element_type=F32)

    def stage_scores(s):
        s_ref[...] = s
        smax_ref[...] = jnp.max(s.reshape(t // 8, 8, 2 * t), axis=0)

    def softmax_first(i):
        s = s_ref[...]
        key_chunk = lax.broadcasted_iota(jnp.int32, s.shape, 0) // CHUNK
        q_chunk = (lax.broadcasted_iota(jnp.int32, s.shape, 1) % t) // CHUNK
        s = jnp.where(key_chunk <= q_chunk, s, NEG_INF)
        m_new = jnp.max(s, axis=0, keepdims=True)
        p = jnp.exp2(s - m_new)
        l_ref[i] = jnp.sum(p, axis=0, keepdims=True)
        p_ref[...] = p.astype(BF16)
        m_ref[i] = m_new

    def softmax_next(i):
        s = s_ref[...]
        m_old = m_ref[i]
        m_new = jnp.maximum(m_old, jnp.max(smax_ref[...], axis=0, keepdims=True))
        p = jnp.exp2(s - m_new)
        alpha = jnp.exp2(m_old - m_new)
        l_ref[i] = alpha * l_ref[i] + jnp.sum(p, axis=0, keepdims=True)
        p_ref[...] = p.astype(BF16)
        alpha_ref[...] = alpha
        m_ref[i] = m_new

    def pv(j):
        vtb = vt_ref[0, :, pl.ds(pl.multiple_of(j * t, t), t)]
        return jnp.dot(vtb, p_ref[...], preferred_element_type=F32)

    s_ref[...] = scores(0, 0)
    softmax_first(0)
    s_ref[...] = scores(1, 1)

    def diag_body(i, carry):
        acc_ref[i - 1] = pv(i - 1)
        nxt = jnp.minimum(i + 1, n_tiles - 1)
        s_next = scores(nxt, nxt)
        softmax_first(i)
        s_ref[...] = s_next
        return carry

    lax.fori_loop(1, n_tiles, diag_body, 0)
    acc_ref[n_tiles - 1] = pv(n_tiles - 1)

    def next_pair(i, j):
        wrap = j + 1 >= i
        return jnp.where(wrap, i + 1, i), jnp.where(wrap, 0, j + 1)

    stage_scores(scores(1, 0))
    softmax_next(1)
    i1, j1 = next_pair(1, 0)
    i1c = jnp.minimum(i1, n_tiles - 1)
    stage_scores(scores(i1c, jnp.minimum(j1, i1c - 1)))
    n_pairs = n_tiles * (n_tiles - 1) // 2

    def pair_body(n, carry):
        ip, jp, i, j = carry
        acc_ref[ip] = alpha_ref[...] * acc_ref[ip] + pv(jp)
        i2, j2 = next_pair(i, j)
        i2c = jnp.minimum(i2, n_tiles - 1)
        s_next = scores(i2c, jnp.minimum(j2, i2c - 1))
        softmax_next(i)
        stage_scores(s_next)
        return i, j, i2, j2

    ip, jp, _, _ = lax.fori_loop(1, n_pairs, pair_body, (1, 0, i1, j1))
    acc_ref[ip] = alpha_ref[...] * acc_ref[ip] + pv(jp)

    lam = (jnp.exp(jnp.sum(lq1_ref[...] * lk1_ref[...], axis=-1, keepdims=True))
           - jnp.exp(jnp.sum(lq2_ref[...] * lk2_ref[...], axis=-1, keepdims=True)) + lam_init)

    group = DIFF_OUT_GROUP if n_tiles % DIFF_OUT_GROUP == 0 else 1

    def out_body(ig, carry):
        for u in range(group):
            i = ig * group + u
            on = acc_ref[i] / l_ref[i]
            o = on[:, :t] - lam * on[:, t:]
            ms = jnp.mean(o * o, axis=0, keepdims=True)
            o = o * lax.rsqrt(ms + EPS)
            o = o.T * sub_ref[...] * (1.0 - lam_init)
            o_ref[pl.ds(pl.multiple_of(i * t, t), t), :] = o.astype(o_ref.dtype)
        return carry

    lax.fori_loop(0, n_tiles // group, out_body, 0)


def _diff_attention(qt, k, vt, lq1, lk1, lq2, lk2, sub_g, batch, seq, lam_init):
    n, d = k.shape
    t = DIFF_TILE
    n_tiles = seq // t
    assert n_tiles >= 2
    small = _const_spec((1, DIFF_HEAD_DIM))
    return pl.pallas_call(
        functools.partial(_diff_attn_kernel, lam_init=lam_init),
        grid=(batch, d // LANES),
        in_specs=[pl.BlockSpec((1, LANES, seq), lambda b, h: (b, h, 0)),
                  pl.BlockSpec((seq, LANES), lambda b, h: (b, h)),
                  pl.BlockSpec((1, LANES, seq), lambda b, h: (b, h, 0)),
                  small, small, small, small, _const_spec((1, LANES))],
        out_specs=pl.BlockSpec((seq, LANES), lambda b, h: (b, h)),
        out_shape=jax.ShapeDtypeStruct((n, d), BF16),
        scratch_shapes=[pltpu.VMEM((n_tiles, 1, 2 * t), F32), pltpu.VMEM((n_tiles, 1, 2 * t), F32),
                        pltpu.VMEM((n_tiles, LANES, 2 * t), F32), pltpu.VMEM((1, 2 * t), F32),
                        pltpu.VMEM((t, 2 * t), F32), pltpu.VMEM((t, 2 * t), BF16),
                        pltpu.VMEM((8, 2 * t), F32)],
        compiler_params=pltpu.CompilerParams(dimension_semantics=("arbitrary",) * 2,
                                             vmem_limit_bytes=VMEM_LIMIT_BYTES),
        name="diff_attention",
    )(qt, k, vt, lq1, lk1, lq2, lk2, sub_g)


def _mix_ffn_kernel(x_ref, o_ref, wo_ref, g_ref, wg_ref, wu_ref, wd_ref, out_ref):
    x1 = x_ref[...] + jnp.dot(o_ref[...], wo_ref[0], preferred_element_type=F32)
    h = _rms(x1, g_ref[...]).astype(BF16)
    acc = x1
    for lo, hi in FFN_CHUNKS:
        g = jnp.dot(h, wg_ref[0, :, lo:hi], preferred_element_type=F32)
        u = jnp.dot(h, wu_ref[0, :, lo:hi], preferred_element_type=F32)
        a = (g * (1.0 / (1.0 + jnp.exp(-g))) * u).astype(BF16)
        acc = acc + jnp.dot(a, wd_ref[0, lo:hi, :], preferred_element_type=F32)
    out_ref[...] = acc


def _mix_ffn(x2, o, gain, wo, wg, wu, wd, layer):
    n, d = x2.shape
    tm = TOKEN_TILE
    assert FFN_CHUNKS[-1][1] == wg.shape[2]
    row_spec = pl.BlockSpec((tm, d), lambda i: (i, 0))
    return pl.pallas_call(
        _mix_ffn_kernel,
        grid=(n // tm,),
        in_specs=[row_spec, row_spec, _layer_spec(wo, layer), _const_spec((1, d)),
                  _layer_spec(wg, layer), _layer_spec(wu, layer), _layer_spec(wd, layer)],
        out_specs=row_spec,
        out_shape=jax.ShapeDtypeStruct((n, d), F32),
        compiler_params=pltpu.CompilerParams(dimension_semantics=("arbitrary",),
                                             vmem_limit_bytes=VMEM_LIMIT_BYTES),
        name="mix_ffn",
    )(x2, o, wo, gain, wg, wu, wd)


def _rope_tables(positions):
    inv_freq = ROPE_THETA ** (-jnp.arange(0, ROT_DIM, 2, dtype=F32) / ROT_DIM)
    ang = positions.astype(F32)[:, None, :] * inv_freq[None, :, None]
    return jnp.cos(ang), jnp.sin(ang)


def kernel(x, positions, attn_norm, w_in, w_out, q_norm, k_norm, lambda_q1, lambda_k1, lambda_q2,
           lambda_k2, sub_norm, ffn_norm, w_gate, w_up, w_down):
    batch, seq, d = x.shape
    depth = w_in.shape[0]
    assert seq % QKV_TILE == 0 and seq % TOKEN_TILE == 0
    assert seq % SB_Q_TILE == 0 and seq % DIFF_TILE == 0
    assert d % LANES == 0 and DIFF_V_DIM == LANES and DIFF_TILE % CHUNK == 0
    x2 = x.reshape(batch * seq, d)
    tables = _rope_tables(positions)
    w_in_t = jnp.swapaxes(w_in, 1, 2).astype(BF16)
    wo, wg, wu, wd = (w.astype(BF16) for w in (w_out, w_gate, w_up, w_down))
    for i in range(depth):
        gain = attn_norm[i].reshape(1, d)
        if i % N_MIXERS == 0:
            qt, k, vt = _qkv_proj(x2, gain, w_in_t, i, batch, seq, SB_HEAD_DIM ** -0.5 * LOG2E)
            o = _sb_attention(qt, k, vt, batch, seq)
        else:
            j = i // N_MIXERS
            qg = jnp.broadcast_to(q_norm[j].reshape(-1, 1), (DIFF_HEAD_DIM, LANES))
            kg = jnp.broadcast_to(k_norm[j].reshape(-1, 1), (DIFF_HEAD_DIM, LANES))
            qt, k, vt = _qkv_proj(x2, gain, w_in_t, i, batch, seq, DIFF_HEAD_DIM ** -0.5 * LOG2E,
                                  (qg, kg) + tables)
            lam_init = 0.8 - 0.6 * math.exp(-0.3 * i)
            o = _diff_attention(qt, k, vt, lambda_q1[j].reshape(1, -1), lambda_k1[j].reshape(1, -1),
                                lambda_q2[j].reshape(1, -1), lambda_k2[j].reshape(1, -1),
                                sub_norm[j].reshape(1, -1), batch, seq, lam_init)
        x2 = _mix_ffn(x2, o, ffn_norm[i].reshape(1, d), wo, wg, wu, wd, i)
    return x2.reshape(batch, seq, d)
```

```python
import functools
import math

import jax
import jax.numpy as jnp
from jax import lax
from jax.experimental import pallas as pl
from jax.experimental.pallas import tpu as pltpu

F32 = jnp.float32
BF16 = jnp.bfloat16

N_MIXERS = 2
SB_HEAD_DIM = 64
DIFF_HEAD_DIM = 64
DIFF_V_DIM = 2 * DIFF_HEAD_DIM
CHUNK = 64
ROPE_THETA = 500000.0
ROT_DIM = DIFF_HEAD_DIM // 4
EPS = 1e-6
NEG_INF = -1e30

LANES = 128
VMEM_LIMIT_BYTES = 56 * 1024 * 1024

QKV_TILE = 1024
TOKEN_TILE = 512
SB_Q_TILE = 256
SB_SUB = 128
SB_PRE_SUBS = 2
SB_NEXT_SUBS = 2
DIFF_TILE = 512
DIFF_OUT_GROUP = 4
FFN_CHUNKS = ((0, 1024), (1024, 2048), (2048, 2816))

LOG2E = 1.4426950408889634
F32_EXP2_UNDERFLOW = -150.0


def _rms(x, gain):
    ms = jnp.mean(x * x, axis=-1, keepdims=True)
    return x * lax.rsqrt(ms + EPS) * gain


def _const_spec(shape):
    nd = len(shape)
    return pl.BlockSpec(shape, lambda *_: (0,) * nd, pipeline_mode=pl.Buffered(1))


def _layer_spec(stacked, layer):
    return pl.BlockSpec((1,) + stacked.shape[1:], lambda *_: (layer, 0, 0),
                        pipeline_mode=pl.Buffered(1))


def _project_t(wt_ref, h, part, d_model):
    w = wt_ref[0, part * d_model:(part + 1) * d_model, :]
    return lax.dot_general(w, h, (((1,), (1,)), ((), ())), preferred_element_type=F32)


def _qkv_kernel(x_ref, g_ref, wt_ref, qt_ref, k_ref, vt_ref, *, d_model, scale):
    h = _rms(x_ref[...], g_ref[...]).astype(BF16)
    kt = _project_t(wt_ref, h, 1, d_model)
    for c in range(d_model // LANES):
        sl = slice(c * LANES, (c + 1) * LANES)
        k_ref[:, sl] = kt[sl, :].T.astype(BF16)
    qt_ref[0] = (_project_t(wt_ref, h, 0, d_model) * scale).astype(BF16)
    vt_ref[0] = _project_t(wt_ref, h, 2, d_model).astype(BF16)


def _qkv_diff_kernel(x_ref, g_ref, wt_ref, qg_ref, kg_ref, cost_ref, sint_ref,
                     qt_ref, k_ref, vt_ref, *, d_model, scale):
    h = _rms(x_ref[...], g_ref[...]).astype(BF16)
    tm = h.shape[0]
    cos_t, sin_t = cost_ref[0], sint_ref[0]
    half = ROT_DIM // 2

    project = functools.partial(_project_t, wt_ref, h, d_model=d_model)

    def norm_rope(xg, gain):
        ms = jnp.mean(xg * xg, axis=0, keepdims=True)
        y = xg * lax.rsqrt(ms + EPS) * gain
        x1, x2 = y[:half], y[half:ROT_DIM]
        return jnp.concatenate([x1 * cos_t - x2 * sin_t, x2 * cos_t + x1 * sin_t, y[ROT_DIM:]],
                               axis=0)

    def slabs(xt, gain):
        per_slab = LANES // DIFF_HEAD_DIM
        for c in range(d_model // LANES):
            parts = [norm_rope(xt[g * DIFF_HEAD_DIM:(g + 1) * DIFF_HEAD_DIM, :], gain)
                     for g in range(c * per_slab, (c + 1) * per_slab)]
            yield slice(c * LANES, (c + 1) * LANES), jnp.concatenate(parts, axis=0)

    for sl, y in slabs(project(1), jnp.tile(kg_ref[...], (1, tm // LANES))):
        k_ref[:, sl] = y.T.astype(BF16)
    for sl, y in slabs(project(0), jnp.tile(qg_ref[...], (1, tm // LANES))):
        qt_ref[0, sl, :] = (y * scale).astype(BF16)
    vt_ref[0] = project(2).astype(BF16)


def _qkv_proj(x2, gain, wt, layer, batch, seq, scale, diff_args=None):
    n, d = x2.shape
    tm = QKV_TILE
    tiles_per_seq = seq // tm
    row_spec = pl.BlockSpec((tm, d), lambda i: (i, 0))
    t_spec = pl.BlockSpec((1, d, tm), lambda i: (i // tiles_per_seq, 0, i % tiles_per_seq))
    row_shape = jax.ShapeDtypeStruct((n, d), BF16)
    t_shape = jax.ShapeDtypeStruct((batch, d, seq), BF16)
    if diff_args is None:
        body = functools.partial(_qkv_kernel, d_model=d, scale=scale)
        name = "qkv_sb"
        in_specs = [row_spec, _const_spec((1, d)), _layer_spec(wt, layer)]
        args = [x2, gain, wt]
    else:
        qg, kg, cos_t, sin_t = diff_args
        tabt_spec = pl.BlockSpec((1, ROT_DIM // 2, tm),
                                 lambda i: (i // tiles_per_seq, 0, i % tiles_per_seq))
        body = functools.partial(_qkv_diff_kernel, d_model=d, scale=scale)
        name = "qkv_diff"
        in_specs = [row_spec, _const_spec((1, d)), _layer_spec(wt, layer),
                    _const_spec(qg.shape), _const_spec(kg.shape), tabt_spec, tabt_spec]
        args = [x2, gain, wt, qg, kg, cos_t, sin_t]
    return pl.pallas_call(
        body,
        grid=(n // tm,),
        in_specs=in_specs,
        out_specs=[t_spec, row_spec, t_spec],
        out_shape=[t_shape, row_shape, t_shape],
        compiler_params=pltpu.CompilerParams(dimension_semantics=("arbitrary",),
                                             vmem_limit_bytes=VMEM_LIMIT_BYTES),
        name=name,
    )(*args)


def _sb_attn_kernel(qt_ref, k_ref, vt_ref, o_ref, z_ref, w_ref, w0_ref, c_ref, accx_ref, call_ref):
    tq, sub = SB_Q_TILE, SB_SUB
    half = tq // 2
    gw = 2 * half
    n_tiles = k_ref.shape[0] // tq
    n_sub = SB_PRE_SUBS + tq // sub
    r_u = lax.broadcasted_iota(jnp.int32, (sub, 2 * sub), 0)
    c_u = lax.broadcasted_iota(jnp.int32, (sub, 2 * sub), 1) % sub
    later2 = jnp.where(c_u > r_u, 1.0, 0.0).astype(BF16)
    tri = (lax.broadcasted_iota(jnp.int32, (sub, gw), 0)
           < lax.broadcasted_iota(jnp.int32, (sub, gw), 1) % half)

    def q4x(t):
        qt = qt_ref[0, :, pl.ds(pl.multiple_of(t * tq, tq), tq)]
        row = lax.broadcasted_iota(jnp.int32, (LANES, half), 0)
        zero = jnp.zeros((LANES, half), qt.dtype)
        parts = []
        for g in range(2):
            qg = qt[:, g * half:(g + 1) * half]
            parts += [jnp.where(row < SB_HEAD_DIM, qg, zero), jnp.where(row >= SB_HEAD_DIM, qg, zero)]
        return jnp.concatenate(parts, axis=1)

    def first_block(t):
        return jnp.maximum(t * (tq // sub) - SB_PRE_SUBS, 0)

    def first_key(t):
        return pl.multiple_of(first_block(t) * sub, sub)

    def scores(t):
        kb = k_ref[pl.ds(first_key(t), n_sub * sub), :]
        return jnp.dot(kb, q4x(t), preferred_element_type=F32)

    def sub_block(z, c, mask):
        if mask is not None:
            z = jnp.where(mask, z, NEG_INF)
        nabs = pltpu.bitcast(pltpu.bitcast(z, jnp.uint32) | jnp.uint32(0x80000000), F32)
        sp = jnp.log(1.0 + jnp.exp2(nabs)) * LOG2E
        ls = jnp.minimum(z, 0.0) - sp
        lk = ls - z
        hi = lk.astype(BF16)
        lo = (lk - hi.astype(F32)).astype(BF16)
        after = jnp.dot(later2, jnp.concatenate([hi, lo], axis=0), preferred_element_type=F32)
        w = jnp.exp2(ls + after + c)
        return w.astype(BF16), c + after[0:1, :] + lk[0:1, :]

    def first_step(first_sb, dst_ref, skip_first):
        zero_c = jnp.zeros((1, gw), F32)
        lo_rows = slice(first_sb * sub, (first_sb + 1) * sub)
        hi_rows = slice((first_sb + 1) * sub, (first_sb + 2) * sub)
        wb, cb = sub_block(z_ref[hi_rows, gw:], zero_c, tri)
        dst_ref[hi_rows, gw:] = wb
        wa, ca = sub_block(z_ref[lo_rows, :gw], zero_c, tri)
        wb, cb = sub_block(z_ref[lo_rows, gw:], cb, None)
        dst_ref[lo_rows, :gw] = wa
        dst_ref[lo_rows, gw:] = wb
        c = jnp.concatenate([ca, cb], axis=1)
        for sb in reversed(range(1 if skip_first else 0, first_sb)):
            w, c = sub_block(z_ref[sb * sub:(sb + 1) * sub, :], c, None)
            dst_ref[sb * sub:(sb + 1) * sub, :] = w
        if skip_first:
            wa, ca = sub_block(z_ref[0:sub, :gw], c[:, :gw], None)
            dst_ref[0:sub, :gw] = wa
            c = jnp.concatenate([ca, c[:, gw:]], axis=1)
        c_ref[...] = c
        return jnp.max(c)

    def extra_steps(t, c_max):
        def cond(carry):
            j, c_max = carry
            return (j > 0) & (c_max > F32_EXP2_UNDERFLOW)

        def body(carry):
            j, _ = carry
            j = j - SB_NEXT_SUBS
            start = pl.multiple_of(j * sub, sub)
            kb = k_ref[pl.ds(start, SB_NEXT_SUBS * sub), :]
            z_all = jnp.dot(kb, q4x(t), preferred_element_type=F32)
            c = c_ref[...]
            ws = [None] * SB_NEXT_SUBS
            for sb in reversed(range(SB_NEXT_SUBS)):
                ws[sb], c = sub_block(z_all[sb * sub:(sb + 1) * sub], c, None)
            vtb = vt_ref[0, :, pl.ds(start, SB_NEXT_SUBS * sub)]
            accx_ref[...] += jnp.dot(vtb, jnp.concatenate(ws, axis=0), preferred_element_type=F32)
            c_ref[...] = c
            return j, jnp.max(c)

        lax.while_loop(cond, body, (first_block(t), c_max))

    def write_out(t, acc):
        row_o = lax.broadcasted_iota(jnp.int32, (LANES, half), 0)
        out_t = jnp.concatenate(
            [jnp.where(row_o < SB_HEAD_DIM, acc[:, g * gw:g * gw + half],
                       acc[:, g * gw + half:(g + 1) * gw]) for g in range(2)], axis=1)
        o_ref[pl.ds(pl.multiple_of(t * tq, tq), tq), :] = out_t.T.astype(o_ref.dtype)

    def finish(t, with_extra):
        vtb = vt_ref[0, :, pl.ds(first_key(t), n_sub * sub)]
        acc = jnp.dot(vtb, w_ref[...], preferred_element_type=F32)
        if with_extra:
            acc = acc + accx_ref[...]
        write_out(t, acc)

    w_ref[...] = jnp.zeros_like(w_ref)
    w0_ref[...] = jnp.zeros_like(w0_ref)
    z_ref[...] = scores(0)
    first_step(0, w0_ref, False)
    z_ref[...] = scores(1)
    write_out(0, jnp.dot(vt_ref[0, :, 0:tq], w0_ref[...], preferred_element_type=F32))

    def body(t, carry):
        finish(jnp.maximum(t - 1, 1), False)
        z_next = scores(jnp.minimum(t + 1, n_tiles - 1))
        first_step(SB_PRE_SUBS, w_ref, True)
        call_ref[t] = c_ref[...]
        z_ref[...] = z_next
        return carry

    lax.fori_loop(1, n_tiles, body, 0)
    finish(n_tiles - 1, False)

    @pl.when(jnp.max(call_ref[1:]) > F32_EXP2_UNDERFLOW)
    def _():
        def redo(t, carry):
            @pl.when(jnp.max(call_ref[t]) > F32_EXP2_UNDERFLOW)
            def _():
                z_ref[...] = scores(t)
                accx_ref[...] = jnp.zeros_like(accx_ref)
                extra_steps(t, first_step(SB_PRE_SUBS, w_ref, False))
                finish(t, True)
            return carry

        lax.fori_loop(1, n_tiles, redo, 0)


def _sb_attention(qt, k, vt, batch, seq):
    n, d = k.shape
    tq = SB_Q_TILE
    rows = SB_PRE_SUBS * SB_SUB + tq
    assert seq >= rows and seq // tq >= 2
    return pl.pallas_call(
        _sb_attn_kernel,
        grid=(batch, d // LANES),
        in_specs=[pl.BlockSpec((1, LANES, seq), lambda b, p: (b, p, 0)),
                  pl.BlockSpec((seq, LANES), lambda b, p: (b, p)),
                  pl.BlockSpec((1, LANES, seq), lambda b, p: (b, p, 0))],
        out_specs=pl.BlockSpec((seq, LANES), lambda b, p: (b, p)),
        out_shape=jax.ShapeDtypeStruct((n, d), BF16),
        scratch_shapes=[pltpu.VMEM((rows, 2 * tq), F32), pltpu.VMEM((rows, 2 * tq), BF16),
                        pltpu.VMEM((tq, 2 * tq), BF16),
                        pltpu.VMEM((1, 2 * tq), F32), pltpu.VMEM((LANES, 2 * tq), F32),
                        pltpu.VMEM((seq // tq, 1, 2 * tq), F32)],
        compiler_params=pltpu.CompilerParams(dimension_semantics=("arbitrary",) * 2,
                                             vmem_limit_bytes=VMEM_LIMIT_BYTES),
        name="sb_attention",
    )(qt, k, vt)


def _diff_attn_kernel(qt_ref, k_ref, vt_ref, lq1_ref, lk1_ref, lq2_ref, lk2_ref, sub_ref, o_ref,
                      m_ref, l_ref, acc_ref, alpha_ref, s_ref, p_ref, smax_ref, *, lam_init):
    t = DIFF_TILE
    n_tiles = k_ref.shape[0] // t

    def q2x(i):
        qt = qt_ref[0, :, pl.ds(pl.multiple_of(i * t, t), t)]
        row = lax.broadcasted_iota(jnp.int32, qt.shape, 0)
        zero = jnp.zeros_like(qt)
        return jnp.concatenate([jnp.where(row < DIFF_HEAD_DIM, qt, zero),
                                jnp.where(row >= DIFF_HEAD_DIM, qt, zero)], axis=1)

    def scores(i, j):
        kb = k_ref[pl.ds(pl.multiple_of(j * t, t), t), :]
        return jnp.dot(kb, q2x(i), preferred_element_type=F32)

    def stage_scores(s):
        s_ref[...] = s
        smax_ref[...] = jnp.max(s.reshape(t // 8, 8, 2 * t), axis=0)

    def softmax_next(i):
        s = s_ref[...]
        m_old = m_ref[i]
        m_new = jnp.maximum(m_old, jnp.max(smax_ref[...], axis=0, keepdims=True))
        p = jnp.exp2(s - m_new)
        alpha = jnp.exp2(m_old - m_new)
        l_ref[i] = alpha * l_ref[i] + jnp.sum(p, axis=0, keepdims=True)
        p_ref[...] = p.astype(BF16)
        alpha_ref[...] = alpha
        m_ref[i] = m_new

    def pv(j):
        vtb = vt_ref[0, :, pl.ds(pl.multiple_of(j * t, t), t)]
        return jnp.dot(vtb, p_ref[...], preferred_element_type=F32)

    h = t // 2

    def late(x):
        return jnp.concatenate([x[:, h:t], x[:, t + h:]], axis=1)

    def merge_late(full, late_part):
        return jnp.concatenate([full[:, 0:h], late_part[:, 0:h], full[:, t:t + h], late_part[:, h:]],
                               axis=1)

    def scores_first(i):
        start = pl.multiple_of(i * t, t)
        q = q2x(i)
        top = jnp.dot(k_ref[pl.ds(start, h), :], q, preferred_element_type=F32)
        bot = jnp.dot(k_ref[pl.ds(start + h, h), :], late(q), preferred_element_type=F32)
        return top, bot

    def stage_first(top_bot):
        s_ref[0:h, :], s_ref[h:t, 0:t] = top_bot

    def softmax_first(i):
        top, bot = s_ref[0:h, :], s_ref[h:t, 0:t]
        key_chunk = lax.broadcasted_iota(jnp.int32, (h, 2 * t), 0) // CHUNK
        q_pos = lax.broadcasted_iota(jnp.int32, (h, 2 * t), 1) % t
        top = jnp.where((q_pos >= h) | (key_chunk <= q_pos // CHUNK), top, NEG_INF)
        key_chunk_b = lax.broadcasted_iota(jnp.int32, (h, t), 0) // CHUNK
        q_chunk_b = (lax.broadcasted_iota(jnp.int32, (h, t), 1) % h) // CHUNK
        bot = jnp.where(key_chunk_b <= q_chunk_b, bot, NEG_INF)
        m_top = jnp.max(top, axis=0, keepdims=True)
        m_late = jnp.maximum(late(m_top), jnp.max(bot, axis=0, keepdims=True))
        m_new = merge_late(m_top, m_late)
        p_top = jnp.exp2(top - m_new)
        p_bot = jnp.exp2(bot - m_late)
        l_top = jnp.sum(p_top, axis=0, keepdims=True)
        l_ref[i] = merge_late(l_top, late(l_top) + jnp.sum(p_bot, axis=0, keepdims=True))
        p_ref[0:h, :] = p_top.astype(BF16)
        p_ref[h:t, 0:t] = p_bot.astype(BF16)
        m_ref[i] = m_new

    def pv_first(j):
        start = pl.multiple_of(j * t, t)
        d_top = jnp.dot(vt_ref[0, :, pl.ds(start, h)], p_ref[0:h, :], preferred_element_type=F32)
        d_bot = jnp.dot(vt_ref[0, :, pl.ds(start + h, h)], p_ref[h:t, 0:t],
                        preferred_element_type=F32)
        return merge_late(d_top, late(d_top) + d_bot)

    stage_first(scores_first(0))
    softmax_first(0)
    stage_first(scores_first(1))

    def diag_body(i, carry):
        acc_ref[i - 1] = pv_first(i - 1)
        s_next = scores_first(jnp.minimum(i + 1, n_tiles - 1))
        softmax_first(i)
        stage_first(s_next)
        return carry

    lax.fori_loop(1, n_tiles, diag_body, 0)
    acc_ref[n_tiles - 1] = pv_first(n_tiles - 1)

    def next_pair(i, j):
        wrap = j + 1 >= i
        return jnp.where(wrap, i + 1, i), jnp.where(wrap, 0, j + 1)

    stage_scores(scores(1, 0))
    softmax_next(1)
    i1, j1 = next_pair(1, 0)
    i1c = jnp.minimum(i1, n_tiles - 1)
    stage_scores(scores(i1c, jnp.minimum(j1, i1c - 1)))
    n_pairs = n_tiles * (n_tiles - 1) // 2

    def pair_body(n, carry):
        ip, jp, i, j = carry
        acc_ref[ip] = alpha_ref[...] * acc_ref[ip] + pv(jp)
        i2, j2 = next_pair(i, j)
        i2c = jnp.minimum(i2, n_tiles - 1)
        s_next = scores(i2c, jnp.minimum(j2, i2c - 1))
        softmax_next(i)
        stage_scores(s_next)
        return i, j, i2, j2

    ip, jp, _, _ = lax.fori_loop(1, n_pairs, pair_body, (1, 0, i1, j1))
    acc_ref[ip] = alpha_ref[...] * acc_ref[ip] + pv(jp)

    lam = (jnp.exp(jnp.sum(lq1_ref[...] * lk1_ref[...], axis=-1, keepdims=True))
           - jnp.exp(jnp.sum(lq2_ref[...] * lk2_ref[...], axis=-1, keepdims=True)) + lam_init)

    group = DIFF_OUT_GROUP if n_tiles % DIFF_OUT_GROUP == 0 else 1

    def out_body(ig, carry):
        for u in range(group):
            i = ig * group + u
            on = acc_ref[i] / l_ref[i]
            o = on[:, :t] - lam * on[:, t:]
            ms = jnp.mean(o * o, axis=0, keepdims=True)
            o = o * lax.rsqrt(ms + EPS)
            o = o.T * sub_ref[...] * (1.0 - lam_init)
            o_ref[pl.ds(pl.multiple_of(i * t, t), t), :] = o.astype(o_ref.dtype)
        return carry

    lax.fori_loop(0, n_tiles // group, out_body, 0)


def _diff_attention(qt, k, vt, lq1, lk1, lq2, lk2, sub_g, batch, seq, lam_init):
    n, d = k.shape
    t = DIFF_TILE
    n_tiles = seq // t
    assert n_tiles >= 2
    small = _const_spec((1, DIFF_HEAD_DIM))
    return pl.pallas_call(
        functools.partial(_diff_attn_kernel, lam_init=lam_init),
        grid=(batch, d // LANES),
        in_specs=[pl.BlockSpec((1, LANES, seq), lambda b, h: (b, h, 0)),
                  pl.BlockSpec((seq, LANES), lambda b, h: (b, h)),
                  pl.BlockSpec((1, LANES, seq), lambda b, h: (b, h, 0)),
                  small, small, small, small, _const_spec((1, LANES))],
        out_specs=pl.BlockSpec((seq, LANES), lambda b, h: (b, h)),
        out_shape=jax.ShapeDtypeStruct((n, d), BF16),
        scratch_shapes=[pltpu.VMEM((n_tiles, 1, 2 * t), F32), pltpu.VMEM((n_tiles, 1, 2 * t), F32),
                        pltpu.VMEM((n_tiles, LANES, 2 * t), F32), pltpu.VMEM((1, 2 * t), F32),
                        pltpu.VMEM((t, 2 * t), F32), pltpu.VMEM((t, 2 * t), BF16),
                        pltpu.VMEM((8, 2 * t), F32)],
        compiler_params=pltpu.CompilerParams(dimension_semantics=("arbitrary",) * 2,
                                             vmem_limit_bytes=VMEM_LIMIT_BYTES),
        name="diff_attention",
    )(qt, k, vt, lq1, lk1, lq2, lk2, sub_g)


def _mix_ffn_kernel(x_ref, o_ref, wo_ref, g_ref, wg_ref, wu_ref, wd_ref, out_ref):
    x1 = x_ref[...] + jnp.dot(o_ref[...], wo_ref[0], preferred_element_type=F32)
    h = _rms(x1, g_ref[...]).astype(BF16)
    acc = x1
    for lo, hi in FFN_CHUNKS:
        g = jnp.dot(h, wg_ref[0, :, lo:hi], preferred_element_type=F32)
        u = jnp.dot(h, wu_ref[0, :, lo:hi], preferred_element_type=F32)
        a = (g * (1.0 / (1.0 + jnp.exp(-g))) * u).astype(BF16)
        acc = acc + jnp.dot(a, wd_ref[0, lo:hi, :], preferred_element_type=F32)
    out_ref[...] = acc


def _mix_ffn(x2, o, gain, wo, wg, wu, wd, layer):
    n, d = x2.shape
    tm = TOKEN_TILE
    assert FFN_CHUNKS[-1][1] == wg.shape[2]
    row_spec = pl.BlockSpec((tm, d), lambda i: (i, 0))
    return pl.pallas_call(
        _mix_ffn_kernel,
        grid=(n // tm,),
        in_specs=[row_spec, row_spec, _layer_spec(wo, layer), _const_spec((1, d)),
                  _layer_spec(wg, layer), _layer_spec(wu, layer), _layer_spec(wd, layer)],
        out_specs=row_spec,
        out_shape=jax.ShapeDtypeStruct((n, d), F32),
        compiler_params=pltpu.CompilerParams(dimension_semantics=("arbitrary",),
                                             vmem_limit_bytes=VMEM_LIMIT_BYTES),
        name="mix_ffn",
    )(x2, o, wo, gain, wg, wu, wd)


def _rope_tables(positions):
    inv_freq = ROPE_THETA ** (-jnp.arange(0, ROT_DIM, 2, dtype=F32) / ROT_DIM)
    ang = positions.astype(F32)[:, None, :] * inv_freq[None, :, None]
    return jnp.cos(ang), jnp.sin(ang)


def kernel(x, positions, attn_norm, w_in, w_out, q_norm, k_norm, lambda_q1, lambda_k1, lambda_q2,
           lambda_k2, sub_norm, ffn_norm, w_gate, w_up, w_down):
    batch, seq, d = x.shape
    depth = w_in.shape[0]
    assert seq % QKV_TILE == 0 and seq % TOKEN_TILE == 0
    assert seq % SB_Q_TILE == 0 and seq % DIFF_TILE == 0
    assert d % LANES == 0 and DIFF_V_DIM == LANES and DIFF_TILE % CHUNK == 0
    x2 = x.reshape(batch * seq, d)
    tables = _rope_tables(positions)
    w_in_t = jnp.swapaxes(w_in, 1, 2).astype(BF16)
    wo, wg, wu, wd = (w.astype(BF16) for w in (w_out, w_gate, w_up, w_down))
    for i in range(depth):
        gain = attn_norm[i].reshape(1, d)
        if i % N_MIXERS == 0:
            qt, k, vt = _qkv_proj(x2, gain, w_in_t, i, batch, seq, SB_HEAD_DIM ** -0.5 * LOG2E)
            o = _sb_attention(qt, k, vt, batch, seq)
        else:
            j = i // N_MIXERS
            qg = jnp.broadcast_to(q_norm[j].reshape(-1, 1), (DIFF_HEAD_DIM, LANES))
            kg = jnp.broadcast_to(k_norm[j].reshape(-1, 1), (DIFF_HEAD_DIM, LANES))
            qt, k, vt = _qkv_proj(x2, gain, w_in_t, i, batch, seq, DIFF_HEAD_DIM ** -0.5 * LOG2E,
                                  (qg, kg) + tables)
            lam_init = 0.8 - 0.6 * math.exp(-0.3 * i)
            o = _diff_attention(qt, k, vt, lambda_q1[j].reshape(1, -1), lambda_k1[j].reshape(1, -1),
                                lambda_q2[j].reshape(1, -1), lambda_k2[j].reshape(1, -1),
                                sub_norm[j].reshape(1, -1), batch, seq, lam_init)
        x2 = _mix_ffn(x2, o, ffn_norm[i].reshape(1, d), wo, wg, wu, wd, i)
    return x2.reshape(batch, seq, d)
```

```python
import functools
import math

import jax
import jax.numpy as jnp
from jax import lax
from jax.experimental import pallas as pl
from jax.experimental.pallas import tpu as pltpu

F32 = jnp.float32
BF16 = jnp.bfloat16

N_MIXERS = 2
SB_HEAD_DIM = 64
DIFF_HEAD_DIM = 64
DIFF_V_DIM = 2 * DIFF_HEAD_DIM
CHUNK = 64
ROPE_THETA = 500000.0
ROT_DIM = DIFF_HEAD_DIM // 4
EPS = 1e-6
NEG_INF = -1e30

LANES = 128
VMEM_LIMIT_BYTES = 56 * 1024 * 1024

QKV_TILE = 1024
TOKEN_TILE = 512
SB_Q_TILE = 256
SB_SUB = 128
SB_PRE_SUBS = 2
SB_NEXT_SUBS = 2
DIFF_TILE = 512
DIFF_STRIP = 128
DIFF_OUT_GROUP = 4
FFN_CHUNKS = ((0, 1024), (1024, 2048), (2048, 2816))

LOG2E = 1.4426950408889634
F32_EXP2_UNDERFLOW = -150.0


def _rms(x, gain):
    ms = jnp.mean(x * x, axis=-1, keepdims=True)
    return x * lax.rsqrt(ms + EPS) * gain


def _const_spec(shape):
    nd = len(shape)
    return pl.BlockSpec(shape, lambda *_: (0,) * nd, pipeline_mode=pl.Buffered(1))


def _layer_spec(stacked, layer):
    return pl.BlockSpec((1,) + stacked.shape[1:], lambda *_: (layer, 0, 0),
                        pipeline_mode=pl.Buffered(1))


def _project_t(wt_ref, h, part, d_model):
    w = wt_ref[0, part * d_model:(part + 1) * d_model, :]
    return lax.dot_general(w, h, (((1,), (1,)), ((), ())), preferred_element_type=F32)


def _qkv_kernel(x_ref, g_ref, wt_ref, qt_ref, k_ref, vt_ref, *, d_model, scale):
    h = _rms(x_ref[...], g_ref[...]).astype(BF16)
    kt = _project_t(wt_ref, h, 1, d_model)
    for c in range(d_model // LANES):
        sl = slice(c * LANES, (c + 1) * LANES)
        k_ref[:, sl] = kt[sl, :].T.astype(BF16)
    qt_ref[0] = (_project_t(wt_ref, h, 0, d_model) * scale).astype(BF16)
    vt_ref[0] = _project_t(wt_ref, h, 2, d_model).astype(BF16)


def _qkv_diff_kernel(x_ref, g_ref, wt_ref, qg_ref, kg_ref, cost_ref, sint_ref,
                     qt_ref, k_ref, vt_ref, *, d_model, scale):
    h = _rms(x_ref[...], g_ref[...]).astype(BF16)
    tm = h.shape[0]
    cos_t, sin_t = cost_ref[0], sint_ref[0]
    half = ROT_DIM // 2

    project = functools.partial(_project_t, wt_ref, h, d_model=d_model)

    def norm_rope(xg, gain):
        ms = jnp.mean(xg * xg, axis=0, keepdims=True)
        y = xg * lax.rsqrt(ms + EPS) * gain
        x1, x2 = y[:half], y[half:ROT_DIM]
        return jnp.concatenate([x1 * cos_t - x2 * sin_t, x2 * cos_t + x1 * sin_t, y[ROT_DIM:]],
                               axis=0)

    def slabs(xt, gain):
        per_slab = LANES // DIFF_HEAD_DIM
        for c in range(d_model // LANES):
            parts = [norm_rope(xt[g * DIFF_HEAD_DIM:(g + 1) * DIFF_HEAD_DIM, :], gain)
                     for g in range(c * per_slab, (c + 1) * per_slab)]
            yield slice(c * LANES, (c + 1) * LANES), jnp.concatenate(parts, axis=0)

    for sl, y in slabs(project(1), jnp.tile(kg_ref[...], (1, tm // LANES))):
        k_ref[:, sl] = y.T.astype(BF16)
    for sl, y in slabs(project(0), jnp.tile(qg_ref[...], (1, tm // LANES))):
        qt_ref[0, sl, :] = (y * scale).astype(BF16)
    vt_ref[0] = project(2).astype(BF16)


def _qkv_proj(x2, gain, wt, layer, batch, seq, scale, diff_args=None):
    n, d = x2.shape
    tm = QKV_TILE
    tiles_per_seq = seq // tm
    row_spec = pl.BlockSpec((tm, d), lambda i: (i, 0))
    t_spec = pl.BlockSpec((1, d, tm), lambda i: (i // tiles_per_seq, 0, i % tiles_per_seq))
    row_shape = jax.ShapeDtypeStruct((n, d), BF16)
    t_shape = jax.ShapeDtypeStruct((batch, d, seq), BF16)
    if diff_args is None:
        body = functools.partial(_qkv_kernel, d_model=d, scale=scale)
        name = "qkv_sb"
        in_specs = [row_spec, _const_spec((1, d)), _layer_spec(wt, layer)]
        args = [x2, gain, wt]
    else:
        qg, kg, cos_t, sin_t = diff_args
        tabt_spec = pl.BlockSpec((1, ROT_DIM // 2, tm),
                                 lambda i: (i // tiles_per_seq, 0, i % tiles_per_seq))
        body = functools.partial(_qkv_diff_kernel, d_model=d, scale=scale)
        name = "qkv_diff"
        in_specs = [row_spec, _const_spec((1, d)), _layer_spec(wt, layer),
                    _const_spec(qg.shape), _const_spec(kg.shape), tabt_spec, tabt_spec]
        args = [x2, gain, wt, qg, kg, cos_t, sin_t]
    return pl.pallas_call(
        body,
        grid=(n // tm,),
        in_specs=in_specs,
        out_specs=[t_spec, row_spec, t_spec],
        out_shape=[t_shape, row_shape, t_shape],
        compiler_params=pltpu.CompilerParams(dimension_semantics=("arbitrary",),
                                             vmem_limit_bytes=VMEM_LIMIT_BYTES),
        name=name,
    )(*args)


def _sb_attn_kernel(qt_ref, k_ref, vt_ref, o_ref, z_ref, w_ref, w0_ref, c_ref, accx_ref, call_ref):
    tq, sub = SB_Q_TILE, SB_SUB
    half = tq // 2
    gw = 2 * half
    n_tiles = k_ref.shape[0] // tq
    n_sub = SB_PRE_SUBS + tq // sub
    r_u = lax.broadcasted_iota(jnp.int32, (sub, 2 * sub), 0)
    c_u = lax.broadcasted_iota(jnp.int32, (sub, 2 * sub), 1) % sub
    later2 = jnp.where(c_u > r_u, 1.0, 0.0).astype(BF16)
    tri = (lax.broadcasted_iota(jnp.int32, (sub, gw), 0)
           < lax.broadcasted_iota(jnp.int32, (sub, gw), 1) % half)

    def q4x(t):
        qt = qt_ref[0, :, pl.ds(pl.multiple_of(t * tq, tq), tq)]
        row = lax.broadcasted_iota(jnp.int32, (LANES, half), 0)
        zero = jnp.zeros((LANES, half), qt.dtype)
        parts = []
        for g in range(2):
            qg = qt[:, g * half:(g + 1) * half]
            parts += [jnp.where(row < SB_HEAD_DIM, qg, zero), jnp.where(row >= SB_HEAD_DIM, qg, zero)]
        return jnp.concatenate(parts, axis=1)

    def first_block(t):
        return jnp.maximum(t * (tq // sub) - SB_PRE_SUBS, 0)

    def first_key(t):
        return pl.multiple_of(first_block(t) * sub, sub)

    def scores(t):
        kb = k_ref[pl.ds(first_key(t), n_sub * sub), :]
        return jnp.dot(kb, q4x(t), preferred_element_type=F32)

    def sub_block(z, c, mask):
        if mask is not None:
            z = jnp.where(mask, z, NEG_INF)
        nabs = pltpu.bitcast(pltpu.bitcast(z, jnp.uint32) | jnp.uint32(0x80000000), F32)
        sp = jnp.log(1.0 + jnp.exp2(nabs)) * LOG2E
        ls = jnp.minimum(z, 0.0) - sp
        lk = ls - z
        hi = lk.astype(BF16)
        lo = (lk - hi.astype(F32)).astype(BF16)
        after = jnp.dot(later2, jnp.concatenate([hi, lo], axis=0), preferred_element_type=F32)
        w = jnp.exp2(ls + after + c)
        return w.astype(BF16), c + after[0:1, :] + lk[0:1, :]

    def first_step(first_sb, dst_ref, skip_first):
        zero_c = jnp.zeros((1, gw), F32)
        lo_rows = slice(first_sb * sub, (first_sb + 1) * sub)
        hi_rows = slice((first_sb + 1) * sub, (first_sb + 2) * sub)
        wb, cb = sub_block(z_ref[hi_rows, gw:], zero_c, tri)
        dst_ref[hi_rows, gw:] = wb
        wa, ca = sub_block(z_ref[lo_rows, :gw], zero_c, tri)
        wb, cb = sub_block(z_ref[lo_rows, gw:], cb, None)
        dst_ref[lo_rows, :gw] = wa
        dst_ref[lo_rows, gw:] = wb
        c = jnp.concatenate([ca, cb], axis=1)
        for sb in reversed(range(1 if skip_first else 0, first_sb)):
            w, c = sub_block(z_ref[sb * sub:(sb + 1) * sub, :], c, None)
            dst_ref[sb * sub:(sb + 1) * sub, :] = w
        if skip_first:
            wa, ca = sub_block(z_ref[0:sub, :gw], c[:, :gw], None)
            dst_ref[0:sub, :gw] = wa
            c = jnp.concatenate([ca, c[:, gw:]], axis=1)
        c_ref[...] = c
        return jnp.max(c)

    def extra_steps(t, c_max):
        def cond(carry):
            j, c_max = carry
            return (j > 0) & (c_max > F32_EXP2_UNDERFLOW)

        def body(carry):
            j, _ = carry
            j = j - SB_NEXT_SUBS
            start = pl.multiple_of(j * sub, sub)
            kb = k_ref[pl.ds(start, SB_NEXT_SUBS * sub), :]
            z_all = jnp.dot(kb, q4x(t), preferred_element_type=F32)
            c = c_ref[...]
            ws = [None] * SB_NEXT_SUBS
            for sb in reversed(range(SB_NEXT_SUBS)):
                ws[sb], c = sub_block(z_all[sb * sub:(sb + 1) * sub], c, None)
            vtb = vt_ref[0, :, pl.ds(start, SB_NEXT_SUBS * sub)]
            accx_ref[...] += jnp.dot(vtb, jnp.concatenate(ws, axis=0), preferred_element_type=F32)
            c_ref[...] = c
            return j, jnp.max(c)

        lax.while_loop(cond, body, (first_block(t), c_max))

    def write_out(t, acc):
        row_o = lax.broadcasted_iota(jnp.int32, (LANES, half), 0)
        out_t = jnp.concatenate(
            [jnp.where(row_o < SB_HEAD_DIM, acc[:, g * gw:g * gw + half],
                       acc[:, g * gw + half:(g + 1) * gw]) for g in range(2)], axis=1)
        o_ref[pl.ds(pl.multiple_of(t * tq, tq), tq), :] = out_t.T.astype(o_ref.dtype)

    def finish(t, with_extra):
        vtb = vt_ref[0, :, pl.ds(first_key(t), n_sub * sub)]
        acc = jnp.dot(vtb, w_ref[...], preferred_element_type=F32)
        if with_extra:
            acc = acc + accx_ref[...]
        write_out(t, acc)

    w_ref[...] = jnp.zeros_like(w_ref)
    w0_ref[...] = jnp.zeros_like(w0_ref)
    z_ref[...] = scores(0)
    first_step(0, w0_ref, False)
    z_ref[...] = scores(1)
    write_out(0, jnp.dot(vt_ref[0, :, 0:tq], w0_ref[...], preferred_element_type=F32))

    def body(t, carry):
        finish(jnp.maximum(t - 1, 1), False)
        z_next = scores(jnp.minimum(t + 1, n_tiles - 1))
        first_step(SB_PRE_SUBS, w_ref, True)
        call_ref[t] = c_ref[...]
        z_ref[...] = z_next
        return carry

    lax.fori_loop(1, n_tiles, body, 0)
    finish(n_tiles - 1, False)

    @pl.when(jnp.max(call_ref[1:]) > F32_EXP2_UNDERFLOW)
    def _():
        def redo(t, carry):
            @pl.when(jnp.max(call_ref[t]) > F32_EXP2_UNDERFLOW)
            def _():
                z_ref[...] = scores(t)
                accx_ref[...] = jnp.zeros_like(accx_ref)
                extra_steps(t, first_step(SB_PRE_SUBS, w_ref, False))
                finish(t, True)
            return carry

        lax.fori_loop(1, n_tiles, redo, 0)


def _sb_attention(qt, k, vt, batch, seq):
    n, d = k.shape
    tq = SB_Q_TILE
    rows = SB_PRE_SUBS * SB_SUB + tq
    assert seq >= rows and seq // tq >= 2
    return pl.pallas_call(
        _sb_attn_kernel,
        grid=(batch, d // LANES),
        in_specs=[pl.BlockSpec((1, LANES, seq), lambda b, p: (b, p, 0)),
                  pl.BlockSpec((seq, LANES), lambda b, p: (b, p)),
                  pl.BlockSpec((1, LANES, seq), lambda b, p: (b, p, 0))],
        out_specs=pl.BlockSpec((seq, LANES), lambda b, p: (b, p)),
        out_shape=jax.ShapeDtypeStruct((n, d), BF16),
        scratch_shapes=[pltpu.VMEM((rows, 2 * tq), F32), pltpu.VMEM((rows, 2 * tq), BF16),
                        pltpu.VMEM((tq, 2 * tq), BF16),
                        pltpu.VMEM((1, 2 * tq), F32), pltpu.VMEM((LANES, 2 * tq), F32),
                        pltpu.VMEM((seq // tq, 1, 2 * tq), F32)],
        compiler_params=pltpu.CompilerParams(dimension_semantics=("arbitrary",) * 2,
                                             vmem_limit_bytes=VMEM_LIMIT_BYTES),
        name="sb_attention",
    )(qt, k, vt)


def _diff_attn_kernel(qt_ref, k_ref, vt_ref, lq1_ref, lk1_ref, lq2_ref, lk2_ref, sub_ref, o_ref,
                      m_ref, l_ref, acc_ref, alpha_ref, s_ref, p_ref, smax_ref, *, lam_init):
    t = DIFF_TILE
    n_tiles = k_ref.shape[0] // t

    def q2x(i):
        qt = qt_ref[0, :, pl.ds(pl.multiple_of(i * t, t), t)]
        row = lax.broadcasted_iota(jnp.int32, qt.shape, 0)
        zero = jnp.zeros_like(qt)
        return jnp.concatenate([jnp.where(row < DIFF_HEAD_DIM, qt, zero),
                                jnp.where(row >= DIFF_HEAD_DIM, qt, zero)], axis=1)

    def scores(i, j):
        kb = k_ref[pl.ds(pl.multiple_of(j * t, t), t), :]
        return jnp.dot(kb, q2x(i), preferred_element_type=F32)

    def stage_scores(s):
        s_ref[...] = s
        smax_ref[...] = jnp.max(s.reshape(t // 8, 8, 2 * t), axis=0)

    def softmax_next(i):
        s = s_ref[...]
        m_old = m_ref[i]
        m_new = jnp.maximum(m_old, jnp.max(smax_ref[...], axis=0, keepdims=True))
        p = jnp.exp2(s - m_new)
        alpha = jnp.exp2(m_old - m_new)
        l_ref[i] = alpha * l_ref[i] + jnp.sum(p, axis=0, keepdims=True)
        p_ref[...] = p.astype(BF16)
        alpha_ref[...] = alpha
        m_ref[i] = m_new

    def pv(j):
        vtb = vt_ref[0, :, pl.ds(pl.multiple_of(j * t, t), t)]
        return jnp.dot(vtb, p_ref[...], preferred_element_type=F32)

    strip = DIFF_STRIP
    n_strips = t // strip

    def late(x, r):
        return x if r == 0 else jnp.concatenate([x[:, r * strip:t], x[:, t + r * strip:]], axis=1)

    def merge(full, part, r):
        if r == 0:
            return part
        w = t - r * strip
        return jnp.concatenate([full[:, 0:r * strip], part[:, 0:w], full[:, t:t + r * strip],
                                part[:, w:]], axis=1)

    def rows(r):
        return slice(r * strip, (r + 1) * strip)

    def scores_first(i):
        start = pl.multiple_of(i * t, t)
        q = q2x(i)
        return [jnp.dot(k_ref[pl.ds(start + r * strip, strip), :], late(q, r),
                        preferred_element_type=F32) for r in range(n_strips)]

    def stage_first(pieces):
        for r, piece in enumerate(pieces):
            s_ref[rows(r), 0:piece.shape[1]] = piece

    def softmax_first(i):
        pieces = []
        for r in range(n_strips):
            w = t - r * strip
            sr = s_ref[rows(r), 0:2 * w]
            key_chunk = lax.broadcasted_iota(jnp.int32, sr.shape, 0) // CHUNK
            q_local = lax.broadcasted_iota(jnp.int32, sr.shape, 1) % w
            pieces.append(jnp.where((q_local >= strip) | (key_chunk <= q_local // CHUNK), sr, NEG_INF))
        m_new = jnp.max(pieces[0], axis=0, keepdims=True)
        for r in range(1, n_strips):
            m_r = jnp.maximum(late(m_new, r), jnp.max(pieces[r], axis=0, keepdims=True))
            m_new = merge(m_new, m_r, r)
        l_new = None
        for r in range(n_strips):
            p = jnp.exp2(pieces[r] - late(m_new, r))
            p_ref[rows(r), 0:p.shape[1]] = p.astype(BF16)
            l_r = jnp.sum(p, axis=0, keepdims=True)
            l_new = l_r if r == 0 else merge(l_new, late(l_new, r) + l_r, r)
        l_ref[i] = l_new
        m_ref[i] = m_new

    def pv_first(j):
        start = pl.multiple_of(j * t, t)
        acc = None
        for r in range(n_strips):
            w = t - r * strip
            d = jnp.dot(vt_ref[0, :, pl.ds(start + r * strip, strip)], p_ref[rows(r), 0:2 * w],
                        preferred_element_type=F32)
            acc = d if r == 0 else merge(acc, late(acc, r) + d, r)
        return acc

    stage_first(scores_first(0))
    softmax_first(0)
    stage_first(scores_first(1))

    def diag_body(i, carry):
        acc_ref[i - 1] = pv_first(i - 1)
        s_next = scores_first(jnp.minimum(i + 1, n_tiles - 1))
        softmax_first(i)
        stage_first(s_next)
        return carry

    lax.fori_loop(1, n_tiles, diag_body, 0)
    acc_ref[n_tiles - 1] = pv_first(n_tiles - 1)

    def next_pair(i, j):
        wrap = j + 1 >= i
        return jnp.where(wrap, i + 1, i), jnp.where(wrap, 0, j + 1)

    stage_scores(scores(1, 0))
    softmax_next(1)
    i1, j1 = next_pair(1, 0)
    i1c = jnp.minimum(i1, n_tiles - 1)
    stage_scores(scores(i1c, jnp.minimum(j1, i1c - 1)))
    n_pairs = n_tiles * (n_tiles - 1) // 2

    def pair_body(n, carry):
        ip, jp, i, j = carry
        acc_ref[ip] = alpha_ref[...] * acc_ref[ip] + pv(jp)
        i2, j2 = next_pair(i, j)
        i2c = jnp.minimum(i2, n_tiles - 1)
        s_next = scores(i2c, jnp.minimum(j2, i2c - 1))
        softmax_next(i)
        stage_scores(s_next)
        return i, j, i2, j2

    ip, jp, _, _ = lax.fori_loop(1, n_pairs, pair_body, (1, 0, i1, j1))
    acc_ref[ip] = alpha_ref[...] * acc_ref[ip] + pv(jp)

    lam = (jnp.exp(jnp.sum(lq1_ref[...] * lk1_ref[...], axis=-1, keepdims=True))
           - jnp.exp(jnp.sum(lq2_ref[...] * lk2_ref[...], axis=-1, keepdims=True)) + lam_init)

    group = DIFF_OUT_GROUP if n_tiles % DIFF_OUT_GROUP == 0 else 1

    def out_body(ig, carry):
        for u in range(group):
            i = ig * group + u
            on = acc_ref[i] / l_ref[i]
            o = on[:, :t] - lam * on[:, t:]
            ms = jnp.mean(o * o, axis=0, keepdims=True)
            o = o * lax.rsqrt(ms + EPS)
            o = o.T * sub_ref[...] * (1.0 - lam_init)
            o_ref[pl.ds(pl.multiple_of(i * t, t), t), :] = o.astype(o_ref.dtype)
        return carry

    lax.fori_loop(0, n_tiles // group, out_body, 0)


def _diff_attention(qt, k, vt, lq1, lk1, lq2, lk2, sub_g, batch, seq, lam_init):
    n, d = k.shape
    t = DIFF_TILE
    n_tiles = seq // t
    assert n_tiles >= 2
    small = _const_spec((1, DIFF_HEAD_DIM))
    return pl.pallas_call(
        functools.partial(_diff_attn_kernel, lam_init=lam_init),
        grid=(batch, d // LANES),
        in_specs=[pl.BlockSpec((1, LANES, seq), lambda b, h: (b, h, 0)),
                  pl.BlockSpec((seq, LANES), lambda b, h: (b, h)),
                  pl.BlockSpec((1, LANES, seq), lambda b, h: (b, h, 0)),
                  small, small, small, small, _const_spec((1, LANES))],
        out_specs=pl.BlockSpec((seq, LANES), lambda b, h: (b, h)),
        out_shape=jax.ShapeDtypeStruct((n, d), BF16),
        scratch_shapes=[pltpu.VMEM((n_tiles, 1, 2 * t), F32), pltpu.VMEM((n_tiles, 1, 2 * t), F32),
                        pltpu.VMEM((n_tiles, LANES, 2 * t), F32), pltpu.VMEM((1, 2 * t), F32),
                        pltpu.VMEM((t, 2 * t), F32), pltpu.VMEM((t, 2 * t), BF16),
                        pltpu.VMEM((8, 2 * t), F32)],
        compiler_params=pltpu.CompilerParams(dimension_semantics=("arbitrary",) * 2,
                                             vmem_limit_bytes=VMEM_LIMIT_BYTES),
        name="diff_attention",
    )(qt, k, vt, lq1, lk1, lq2, lk2, sub_g)


def _mix_ffn_kernel(x_ref, o_ref, wo_ref, g_ref, wg_ref, wu_ref, wd_ref, out_ref):
    x1 = x_ref[...] + jnp.dot(o_ref[...], wo_ref[0], preferred_element_type=F32)
    h = _rms(x1, g_ref[...]).astype(BF16)
    acc = x1
    for lo, hi in FFN_CHUNKS:
        g = jnp.dot(h, wg_ref[0, :, lo:hi], preferred_element_type=F32)
        u = jnp.dot(h, wu_ref[0, :, lo:hi], preferred_element_type=F32)
        a = (g * (1.0 / (1.0 + jnp.exp(-g))) * u).astype(BF16)
        acc = acc + jnp.dot(a, wd_ref[0, lo:hi, :], preferred_element_type=F32)
    out_ref[...] = acc


def _mix_ffn(x2, o, gain, wo, wg, wu, wd, layer):
    n, d = x2.shape
    tm = TOKEN_TILE
    assert FFN_CHUNKS[-1][1] == wg.shape[2]
    row_spec = pl.BlockSpec((tm, d), lambda i: (i, 0))
    return pl.pallas_call(
        _mix_ffn_kernel,
        grid=(n // tm,),
        in_specs=[row_spec, row_spec, _layer_spec(wo, layer), _const_spec((1, d)),
                  _layer_spec(wg, layer), _layer_spec(wu, layer), _layer_spec(wd, layer)],
        out_specs=row_spec,
        out_shape=jax.ShapeDtypeStruct((n, d), F32),
        compiler_params=pltpu.CompilerParams(dimension_semantics=("arbitrary",),
                                             vmem_limit_bytes=VMEM_LIMIT_BYTES),
        name="mix_ffn",
    )(x2, o, wo, gain, wg, wu, wd)


def _rope_tables(positions):
    inv_freq = ROPE_THETA ** (-jnp.arange(0, ROT_DIM, 2, dtype=F32) / ROT_DIM)
    ang = positions.astype(F32)[:, None, :] * inv_freq[None, :, None]
    return jnp.cos(ang), jnp.sin(ang)


def kernel(x, positions, attn_norm, w_in, w_out, q_norm, k_norm, lambda_q1, lambda_k1, lambda_q2,
           lambda_k2, sub_norm, ffn_norm, w_gate, w_up, w_down):
    batch, seq, d = x.shape
    depth = w_in.shape[0]
    assert seq % QKV_TILE == 0 and seq % TOKEN_TILE == 0
    assert seq % SB_Q_TILE == 0 and seq % DIFF_TILE == 0
    assert d % LANES == 0 and DIFF_V_DIM == LANES and DIFF_TILE % CHUNK == 0
    x2 = x.reshape(batch * seq, d)
    tables = _rope_tables(positions)
    w_in_t = jnp.swapaxes(w_in, 1, 2).astype(BF16)
    wo, wg, wu, wd = (w.astype(BF16) for w in (w_out, w_gate, w_up, w_down))
    for i in range(depth):
        gain = attn_norm[i].reshape(1, d)
        if i % N_MIXERS == 0:
            qt, k, vt = _qkv_proj(x2, gain, w_in_t, i, batch, seq, SB_HEAD_DIM ** -0.5 * LOG2E)
            o = _sb_attention(qt, k, vt, batch, seq)
        else:
            j = i // N_MIXERS
            qg = jnp.broadcast_to(q_norm[j].reshape(-1, 1), (DIFF_HEAD_DIM, LANES))
            kg = jnp.broadcast_to(k_norm[j].reshape(-1, 1), (DIFF_HEAD_DIM, LANES))
            qt, k, vt = _qkv_proj(x2, gain, w_in_t, i, batch, seq, DIFF_HEAD_DIM ** -0.5 * LOG2E,
                                  (qg, kg) + tables)
            lam_init = 0.8 - 0.6 * math.exp(-0.3 * i)
            o = _diff_attention(qt, k, vt, lambda_q1[j].reshape(1, -1), lambda_k1[j].reshape(1, -1),
                                lambda_q2[j].reshape(1, -1), lambda_k2[j].reshape(1, -1),
                                sub_norm[j].reshape(1, -1), batch, seq, lam_init)
        x2 = _mix_ffn(x2, o, ffn_norm[i].reshape(1, d), wo, wg, wu, wd, i)
    return x2.reshape(batch, seq, d)
```

```python
import functools
import math

import jax
import jax.numpy as jnp
from jax import lax
from jax.experimental import pallas as pl
from jax.experimental.pallas import tpu as pltpu

F32 = jnp.float32
BF16 = jnp.bfloat16

N_MIXERS = 2
SB_HEAD_DIM = 64
DIFF_HEAD_DIM = 64
DIFF_V_DIM = 2 * DIFF_HEAD_DIM
CHUNK = 64
ROPE_THETA = 500000.0
ROT_DIM = DIFF_HEAD_DIM // 4
EPS = 1e-6
NEG_INF = -1e30

LANES = 128
VMEM_LIMIT_BYTES = 56 * 1024 * 1024

QKV_TILE = 1024
TOKEN_TILE = 512
SB_Q_TILE = 256
SB_SUB = 128
SB_PRE_SUBS = 2
SB_NEXT_SUBS = 2
DIFF_TILE = 512
DIFF_STRIP = 128
DIFF_OUT_GROUP = 4
FFN_CHUNKS = ((0, 1024), (1024, 2048), (2048, 2816))

LOG2E = 1.4426950408889634
F32_EXP2_UNDERFLOW = -150.0


def _rms(x, gain):
    ms = jnp.mean(x * x, axis=-1, keepdims=True)
    return x * lax.rsqrt(ms + EPS) * gain


def _const_spec(shape):
    nd = len(shape)
    return pl.BlockSpec(shape, lambda *_: (0,) * nd, pipeline_mode=pl.Buffered(1))


def _layer_spec(stacked, layer):
    return pl.BlockSpec((1,) + stacked.shape[1:], lambda *_: (layer, 0, 0),
                        pipeline_mode=pl.Buffered(1))


def _project_t(wt_ref, h, part, d_model):
    w = wt_ref[0, part * d_model:(part + 1) * d_model, :]
    return lax.dot_general(w, h, (((1,), (1,)), ((), ())), preferred_element_type=F32)


def _qkv_kernel(x_ref, g_ref, wt_ref, qt_ref, k_ref, vt_ref, *, d_model, scale):
    h = _rms(x_ref[...], g_ref[...]).astype(BF16)
    kt = _project_t(wt_ref, h, 1, d_model)
    for c in range(d_model // LANES):
        sl = slice(c * LANES, (c + 1) * LANES)
        k_ref[:, sl] = kt[sl, :].T.astype(BF16)
    qt_ref[0] = (_project_t(wt_ref, h, 0, d_model) * scale).astype(BF16)
    vt_ref[0] = _project_t(wt_ref, h, 2, d_model).astype(BF16)


def _qkv_diff_kernel(x_ref, g_ref, wt_ref, qg_ref, kg_ref, cost_ref, sint_ref,
                     qt_ref, k_ref, vt_ref, *, d_model, scale):
    h = _rms(x_ref[...], g_ref[...]).astype(BF16)
    tm = h.shape[0]
    cos_t, sin_t = cost_ref[0], sint_ref[0]
    half = ROT_DIM // 2

    project = functools.partial(_project_t, wt_ref, h, d_model=d_model)

    def norm_rope(xg, gain):
        ms = jnp.mean(xg * xg, axis=0, keepdims=True)
        y = xg * lax.rsqrt(ms + EPS) * gain
        x1, x2 = y[:half], y[half:ROT_DIM]
        return jnp.concatenate([x1 * cos_t - x2 * sin_t, x2 * cos_t + x1 * sin_t, y[ROT_DIM:]],
                               axis=0)

    def slabs(xt, gain):
        per_slab = LANES // DIFF_HEAD_DIM
        for c in range(d_model // LANES):
            parts = [norm_rope(xt[g * DIFF_HEAD_DIM:(g + 1) * DIFF_HEAD_DIM, :], gain)
                     for g in range(c * per_slab, (c + 1) * per_slab)]
            yield slice(c * LANES, (c + 1) * LANES), jnp.concatenate(parts, axis=0)

    for sl, y in slabs(project(1), jnp.tile(kg_ref[...], (1, tm // LANES))):
        k_ref[:, sl] = y.T.astype(BF16)
    for sl, y in slabs(project(0), jnp.tile(qg_ref[...], (1, tm // LANES))):
        qt_ref[0, sl, :] = (y * scale).astype(BF16)
    vt_ref[0] = project(2).astype(BF16)


def _qkv_proj(x2, gain, wt, layer, batch, seq, scale, diff_args=None):
    n, d = x2.shape
    tm = QKV_TILE
    tiles_per_seq = seq // tm
    row_spec = pl.BlockSpec((tm, d), lambda i: (i, 0))
    t_spec = pl.BlockSpec((1, d, tm), lambda i: (i // tiles_per_seq, 0, i % tiles_per_seq))
    row_shape = jax.ShapeDtypeStruct((n, d), BF16)
    t_shape = jax.ShapeDtypeStruct((batch, d, seq), BF16)
    if diff_args is None:
        body = functools.partial(_qkv_kernel, d_model=d, scale=scale)
        name = "qkv_sb"
        in_specs = [row_spec, _const_spec((1, d)), _layer_spec(wt, layer)]
        args = [x2, gain, wt]
    else:
        qg, kg, cos_t, sin_t = diff_args
        tabt_spec = pl.BlockSpec((1, ROT_DIM // 2, tm),
                                 lambda i: (i // tiles_per_seq, 0, i % tiles_per_seq))
        body = functools.partial(_qkv_diff_kernel, d_model=d, scale=scale)
        name = "qkv_diff"
        in_specs = [row_spec, _const_spec((1, d)), _layer_spec(wt, layer),
                    _const_spec(qg.shape), _const_spec(kg.shape), tabt_spec, tabt_spec]
        args = [x2, gain, wt, qg, kg, cos_t, sin_t]
    return pl.pallas_call(
        body,
        grid=(n // tm,),
        in_specs=in_specs,
        out_specs=[t_spec, row_spec, t_spec],
        out_shape=[t_shape, row_shape, t_shape],
        compiler_params=pltpu.CompilerParams(dimension_semantics=("arbitrary",),
                                             vmem_limit_bytes=VMEM_LIMIT_BYTES),
        name=name,
    )(*args)


def _sb_attn_kernel(qt_ref, k_ref, vt_ref, o_ref, z_ref, w_ref, w0_ref, c_ref, accx_ref, call_ref):
    tq, sub = SB_Q_TILE, SB_SUB
    half = tq // 2
    gw = 2 * half
    n_tiles = k_ref.shape[0] // tq
    n_sub = SB_PRE_SUBS + tq // sub
    r_u = lax.broadcasted_iota(jnp.int32, (sub, 2 * sub), 0)
    c_u = lax.broadcasted_iota(jnp.int32, (sub, 2 * sub), 1) % sub
    later2 = jnp.where(c_u > r_u, 1.0, 0.0).astype(BF16)
    tri = (lax.broadcasted_iota(jnp.int32, (sub, gw), 0)
           < lax.broadcasted_iota(jnp.int32, (sub, gw), 1) % half)

    def q4x(t):
        qt = qt_ref[0, :, pl.ds(pl.multiple_of(t * tq, tq), tq)]
        row = lax.broadcasted_iota(jnp.int32, (LANES, half), 0)
        zero = jnp.zeros((LANES, half), qt.dtype)
        parts = []
        for g in range(2):
            qg = qt[:, g * half:(g + 1) * half]
            parts += [jnp.where(row < SB_HEAD_DIM, qg, zero), jnp.where(row >= SB_HEAD_DIM, qg, zero)]
        return jnp.concatenate(parts, axis=1)

    def first_block(t):
        return jnp.maximum(t * (tq // sub) - SB_PRE_SUBS, 0)

    def first_key(t):
        return pl.multiple_of(first_block(t) * sub, sub)

    def scores(t):
        kb = k_ref[pl.ds(first_key(t), n_sub * sub), :]
        return jnp.dot(kb, q4x(t), preferred_element_type=F32)

    def sub_block(z, c, mask):
        if mask is not None:
            z = jnp.where(mask, z, NEG_INF)
        nabs = pltpu.bitcast(pltpu.bitcast(z, jnp.uint32) | jnp.uint32(0x80000000), F32)
        sp = jnp.log(1.0 + jnp.exp2(nabs)) * LOG2E
        ls = jnp.minimum(z, 0.0) - sp
        lk = ls - z
        hi = lk.astype(BF16)
        lo = (lk - hi.astype(F32)).astype(BF16)
        after = jnp.dot(later2, jnp.concatenate([hi, lo], axis=0), preferred_element_type=F32)
        w = jnp.exp2(ls + after + c)
        return w.astype(BF16), c + after[0:1, :] + lk[0:1, :]

    def first_step(first_sb, dst_ref, skip_first):
        zero_c = jnp.zeros((1, gw), F32)
        lo_rows = slice(first_sb * sub, (first_sb + 1) * sub)
        hi_rows = slice((first_sb + 1) * sub, (first_sb + 2) * sub)
        wb, cb = sub_block(z_ref[hi_rows, gw:], zero_c, tri)
        dst_ref[hi_rows, gw:] = wb
        wa, ca = sub_block(z_ref[lo_rows, :gw], zero_c, tri)
        wb, cb = sub_block(z_ref[lo_rows, gw:], cb, None)
        dst_ref[lo_rows, :gw] = wa
        dst_ref[lo_rows, gw:] = wb
        c = jnp.concatenate([ca, cb], axis=1)
        for sb in reversed(range(1 if skip_first else 0, first_sb)):
            w, c = sub_block(z_ref[sb * sub:(sb + 1) * sub, :], c, None)
            dst_ref[sb * sub:(sb + 1) * sub, :] = w
        if skip_first:
            wa, ca = sub_block(z_ref[0:sub, :gw], c[:, :gw], None)
            dst_ref[0:sub, :gw] = wa
            c = jnp.concatenate([ca, c[:, gw:]], axis=1)
        c_ref[...] = c
        return jnp.max(c)

    def extra_steps(t, c_max):
        def cond(carry):
            j, c_max = carry
            return (j > 0) & (c_max > F32_EXP2_UNDERFLOW)

        def body(carry):
            j, _ = carry
            j = j - SB_NEXT_SUBS
            start = pl.multiple_of(j * sub, sub)
            kb = k_ref[pl.ds(start, SB_NEXT_SUBS * sub), :]
            z_all = jnp.dot(kb, q4x(t), preferred_element_type=F32)
            c = c_ref[...]
            ws = [None] * SB_NEXT_SUBS
            for sb in reversed(range(SB_NEXT_SUBS)):
                ws[sb], c = sub_block(z_all[sb * sub:(sb + 1) * sub], c, None)
            vtb = vt_ref[0, :, pl.ds(start, SB_NEXT_SUBS * sub)]
            accx_ref[...] += jnp.dot(vtb, jnp.concatenate(ws, axis=0), preferred_element_type=F32)
            c_ref[...] = c
            return j, jnp.max(c)

        lax.while_loop(cond, body, (first_block(t), c_max))

    def write_out(t, acc):
        row_o = lax.broadcasted_iota(jnp.int32, (LANES, half), 0)
        out_t = jnp.concatenate(
            [jnp.where(row_o < SB_HEAD_DIM, acc[:, g * gw:g * gw + half],
                       acc[:, g * gw + half:(g + 1) * gw]) for g in range(2)], axis=1)
        o_ref[pl.ds(pl.multiple_of(t * tq, tq), tq), :] = out_t.T.astype(o_ref.dtype)

    def finish(t, with_extra):
        vtb = vt_ref[0, :, pl.ds(first_key(t), n_sub * sub)]
        acc = jnp.dot(vtb, w_ref[...], preferred_element_type=F32)
        if with_extra:
            acc = acc + accx_ref[...]
        write_out(t, acc)

    w_ref[...] = jnp.zeros_like(w_ref)
    w0_ref[...] = jnp.zeros_like(w0_ref)
    z_ref[...] = scores(0)
    first_step(0, w0_ref, False)
    z_ref[...] = scores(1)
    write_out(0, jnp.dot(vt_ref[0, :, 0:tq], w0_ref[...], preferred_element_type=F32))

    def body(t, carry):
        finish(jnp.maximum(t - 1, 1), False)
        z_next = scores(jnp.minimum(t + 1, n_tiles - 1))
        first_step(SB_PRE_SUBS, w_ref, True)
        call_ref[t] = c_ref[...]
        z_ref[...] = z_next
        return carry

    lax.fori_loop(1, n_tiles, body, 0)
    finish(n_tiles - 1, False)

    @pl.when(jnp.max(call_ref[1:]) > F32_EXP2_UNDERFLOW)
    def _():
        def redo(t, carry):
            @pl.when(jnp.max(call_ref[t]) > F32_EXP2_UNDERFLOW)
            def _():
                z_ref[...] = scores(t)
                accx_ref[...] = jnp.zeros_like(accx_ref)
                extra_steps(t, first_step(SB_PRE_SUBS, w_ref, False))
                finish(t, True)
            return carry

        lax.fori_loop(1, n_tiles, redo, 0)


def _sb_attention(qt, k, vt, batch, seq):
    n, d = k.shape
    tq = SB_Q_TILE
    rows = SB_PRE_SUBS * SB_SUB + tq
    assert seq >= rows and seq // tq >= 2
    return pl.pallas_call(
        _sb_attn_kernel,
        grid=(batch, d // LANES),
        in_specs=[pl.BlockSpec((1, LANES, seq), lambda b, p: (b, p, 0)),
                  pl.BlockSpec((seq, LANES), lambda b, p: (b, p)),
                  pl.BlockSpec((1, LANES, seq), lambda b, p: (b, p, 0))],
        out_specs=pl.BlockSpec((seq, LANES), lambda b, p: (b, p)),
        out_shape=jax.ShapeDtypeStruct((n, d), BF16),
        scratch_shapes=[pltpu.VMEM((rows, 2 * tq), F32), pltpu.VMEM((rows, 2 * tq), BF16),
                        pltpu.VMEM((tq, 2 * tq), BF16),
                        pltpu.VMEM((1, 2 * tq), F32), pltpu.VMEM((LANES, 2 * tq), F32),
                        pltpu.VMEM((seq // tq, 1, 2 * tq), F32)],
        compiler_params=pltpu.CompilerParams(dimension_semantics=("arbitrary",) * 2,
                                             vmem_limit_bytes=VMEM_LIMIT_BYTES),
        name="sb_attention",
    )(qt, k, vt)


def _diff_attn_kernel(qt_ref, k_ref, vt_ref, lq1_ref, lk1_ref, lq2_ref, lk2_ref, sub_ref, o_ref,
                      m_ref, l_ref, acc_ref, alpha_ref, s_ref, p_ref, smax_ref, *, lam_init):
    t = DIFF_TILE
    n_tiles = k_ref.shape[0] // t

    def q2x(i):
        qt = qt_ref[0, :, pl.ds(pl.multiple_of(i * t, t), t)]
        row = lax.broadcasted_iota(jnp.int32, qt.shape, 0)
        zero = jnp.zeros_like(qt)
        return jnp.concatenate([jnp.where(row < DIFF_HEAD_DIM, qt, zero),
                                jnp.where(row >= DIFF_HEAD_DIM, qt, zero)], axis=1)

    def scores(i, j):
        kb = k_ref[pl.ds(pl.multiple_of(j * t, t), t), :]
        return jnp.dot(kb, q2x(i), preferred_element_type=F32)

    def stage_scores(s):
        s_ref[...] = s
        smax_ref[...] = jnp.max(s.reshape(t // 8, 8, 2 * t), axis=0)

    def softmax_next(i):
        s = s_ref[...]
        m_old = m_ref[i]
        m_new = jnp.maximum(m_old, jnp.max(smax_ref[...], axis=0, keepdims=True))
        p = jnp.exp2(s - m_new)
        alpha = jnp.exp2(m_old - m_new)
        l_ref[i] = alpha * l_ref[i] + jnp.sum(p, axis=0, keepdims=True)
        p_ref[...] = p.astype(BF16)
        alpha_ref[...] = alpha
        m_ref[i] = m_new

    def pv(j):
        vtb = vt_ref[0, :, pl.ds(pl.multiple_of(j * t, t), t)]
        return jnp.dot(vtb, p_ref[...], preferred_element_type=F32)

    strip = DIFF_STRIP
    n_strips = t // strip

    def late(x, r):
        return x if r == 0 else jnp.concatenate([x[:, r * strip:t], x[:, t + r * strip:]], axis=1)

    def merge(full, part, r):
        if r == 0:
            return part
        w = t - r * strip
        return jnp.concatenate([full[:, 0:r * strip], part[:, 0:w], full[:, t:t + r * strip],
                                part[:, w:]], axis=1)

    def rows(r):
        return slice(r * strip, (r + 1) * strip)

    def scores_first(i):
        start = pl.multiple_of(i * t, t)
        q = q2x(i)
        return [jnp.dot(k_ref[pl.ds(start + r * strip, strip), :], late(q, r),
                        preferred_element_type=F32) for r in range(n_strips)]

    def stage_first(pieces):
        for r, piece in enumerate(pieces):
            s_ref[rows(r), 0:piece.shape[1]] = piece

    def softmax_first(i):
        pieces = []
        for r in range(n_strips):
            w = t - r * strip
            sr = s_ref[rows(r), 0:2 * w]
            key_chunk = lax.broadcasted_iota(jnp.int32, sr.shape, 0) // CHUNK
            q_local = lax.broadcasted_iota(jnp.int32, sr.shape, 1) % w
            pieces.append(jnp.where((q_local >= strip) | (key_chunk <= q_local // CHUNK), sr, NEG_INF))
        m_new = jnp.max(pieces[0], axis=0, keepdims=True)
        for r in range(1, n_strips):
            m_r = jnp.maximum(late(m_new, r), jnp.max(pieces[r], axis=0, keepdims=True))
            m_new = merge(m_new, m_r, r)
        l_new = None
        for r in range(n_strips):
            p = jnp.exp2(pieces[r] - late(m_new, r))
            p_ref[rows(r), 0:p.shape[1]] = p.astype(BF16)
            l_r = jnp.sum(p, axis=0, keepdims=True)
            l_new = l_r if r == 0 else merge(l_new, late(l_new, r) + l_r, r)
        l_ref[i] = l_new
        m_ref[i] = m_new

    def pv_first(j):
        start = pl.multiple_of(j * t, t)
        acc = None
        for r in range(n_strips):
            w = t - r * strip
            d = jnp.dot(vt_ref[0, :, pl.ds(start + r * strip, strip)], p_ref[rows(r), 0:2 * w],
                        preferred_element_type=F32)
            acc = d if r == 0 else merge(acc, late(acc, r) + d, r)
        return acc

    stage_first(scores_first(0))
    softmax_first(0)
    stage_first(scores_first(1))

    def diag_body(i, carry):
        acc_ref[i - 1] = pv_first(i - 1)
        s_next = scores_first(jnp.minimum(i + 1, n_tiles - 1))
        softmax_first(i)
        stage_first(s_next)
        return carry

    lax.fori_loop(1, n_tiles, diag_body, 0)
    acc_ref[n_tiles - 1] = pv_first(n_tiles - 1)

    def next_pair(i, j):
        wrap = j + 1 >= i
        return jnp.where(wrap, i + 1, i), jnp.where(wrap, 0, j + 1)

    stage_scores(scores(1, 0))
    softmax_next(1)
    i1, j1 = next_pair(1, 0)
    i1c = jnp.minimum(i1, n_tiles - 1)
    stage_scores(scores(i1c, jnp.minimum(j1, i1c - 1)))
    n_pairs = n_tiles * (n_tiles - 1) // 2

    def pair_body(n, carry):
        ip, jp, i, j = carry
        acc_ref[ip] = alpha_ref[...] * acc_ref[ip] + pv(jp)
        i2, j2 = next_pair(i, j)
        i2c = jnp.minimum(i2, n_tiles - 1)
        s_next = scores(i2c, jnp.minimum(j2, i2c - 1))
        softmax_next(i)
        stage_scores(s_next)
        return i, j, i2, j2

    ip, jp, _, _ = lax.fori_loop(1, n_pairs, pair_body, (1, 0, i1, j1))
    acc_ref[ip] = alpha_ref[...] * acc_ref[ip] + pv(jp)

    lam = (jnp.exp(jnp.sum(lq1_ref[...] * lk1_ref[...], axis=-1, keepdims=True))
           - jnp.exp(jnp.sum(lq2_ref[...] * lk2_ref[...], axis=-1, keepdims=True)) + lam_init)

    group = DIFF_OUT_GROUP if n_tiles % DIFF_OUT_GROUP == 0 else 1

    def out_body(ig, carry):
        for u in range(group):
            i = ig * group + u
            on = acc_ref[i] / l_ref[i]
            o = on[:, :t] - lam * on[:, t:]
            ms = jnp.mean(o * o, axis=0, keepdims=True)
            o = o * lax.rsqrt(ms + EPS)
            o = o.T * sub_ref[...] * (1.0 - lam_init)
            o_ref[pl.ds(pl.multiple_of(i * t, t), t), :] = o.astype(o_ref.dtype)
        return carry

    lax.fori_loop(0, n_tiles // group, out_body, 0)


def _diff_attention(qt, k, vt, lq1, lk1, lq2, lk2, sub_g, batch, seq, lam_init):
    n, d = k.shape
    t = DIFF_TILE
    n_tiles = seq // t
    assert n_tiles >= 2
    small = _const_spec((1, DIFF_HEAD_DIM))
    return pl.pallas_call(
        functools.partial(_diff_attn_kernel, lam_init=lam_init),
        grid=(batch, d // LANES),
        in_specs=[pl.BlockSpec((1, LANES, seq), lambda b, h: (b, h, 0)),
                  pl.BlockSpec((seq, LANES), lambda b, h: (b, h)),
                  pl.BlockSpec((1, LANES, seq), lambda b, h: (b, h, 0)),
                  small, small, small, small, _const_spec((1, LANES))],
        out_specs=pl.BlockSpec((seq, LANES), lambda b, h: (b, h)),
        out_shape=jax.ShapeDtypeStruct((n, d), BF16),
        scratch_shapes=[pltpu.VMEM((n_tiles, 1, 2 * t), F32), pltpu.VMEM((n_tiles, 1, 2 * t), F32),
                        pltpu.VMEM((n_tiles, LANES, 2 * t), F32), pltpu.VMEM((1, 2 * t), F32),
                        pltpu.VMEM((t, 2 * t), F32), pltpu.VMEM((t, 2 * t), BF16),
                        pltpu.VMEM((8, 2 * t), F32)],
        compiler_params=pltpu.CompilerParams(dimension_semantics=("arbitrary",) * 2,
                                             vmem_limit_bytes=VMEM_LIMIT_BYTES),
        name="diff_attention",
    )(qt, k, vt, lq1, lk1, lq2, lk2, sub_g)


def _mix_ffn_kernel(x_ref, o_ref, wo_ref, g_ref, wg_ref, wu_ref, wd_ref, out_ref):
    tm = x_ref.shape[0]
    halves = (slice(0, tm // 2), slice(tm // 2, tm))
    x1 = [x_ref[r, :] + jnp.dot(o_ref[r, :], wo_ref[0], preferred_element_type=F32) for r in halves]
    hs = [_rms(v, g_ref[...]).astype(BF16) for v in x1]
    for r, acc, h in zip(halves, x1, hs):
        for lo, hi in FFN_CHUNKS:
            g = jnp.dot(h, wg_ref[0, :, lo:hi], preferred_element_type=F32)
            u = jnp.dot(h, wu_ref[0, :, lo:hi], preferred_element_type=F32)
            a = (g * (1.0 / (1.0 + jnp.exp(-g))) * u).astype(BF16)
            acc = acc + jnp.dot(a, wd_ref[0, lo:hi, :], preferred_element_type=F32)
        out_ref[r, :] = acc


def _mix_ffn(x2, o, gain, wo, wg, wu, wd, layer):
    n, d = x2.shape
    tm = TOKEN_TILE
    assert FFN_CHUNKS[-1][1] == wg.shape[2]
    row_spec = pl.BlockSpec((tm, d), lambda i: (i, 0))
    return pl.pallas_call(
        _mix_ffn_kernel,
        grid=(n // tm,),
        in_specs=[row_spec, row_spec, _layer_spec(wo, layer), _const_spec((1, d)),
                  _layer_spec(wg, layer), _layer_spec(wu, layer), _layer_spec(wd, layer)],
        out_specs=row_spec,
        out_shape=jax.ShapeDtypeStruct((n, d), F32),
        compiler_params=pltpu.CompilerParams(dimension_semantics=("arbitrary",),
                                             vmem_limit_bytes=VMEM_LIMIT_BYTES),
        name="mix_ffn",
    )(x2, o, wo, gain, wg, wu, wd)


def _rope_tables(positions):
    inv_freq = ROPE_THETA ** (-jnp.arange(0, ROT_DIM, 2, dtype=F32) / ROT_DIM)
    ang = positions.astype(F32)[:, None, :] * inv_freq[None, :, None]
    return jnp.cos(ang), jnp.sin(ang)


def kernel(x, positions, attn_norm, w_in, w_out, q_norm, k_norm, lambda_q1, lambda_k1, lambda_q2,
           lambda_k2, sub_norm, ffn_norm, w_gate, w_up, w_down):
    batch, seq, d = x.shape
    depth = w_in.shape[0]
    assert seq % QKV_TILE == 0 and seq % TOKEN_TILE == 0
    assert seq % SB_Q_TILE == 0 and seq % DIFF_TILE == 0
    assert d % LANES == 0 and DIFF_V_DIM == LANES and DIFF_TILE % CHUNK == 0
    x2 = x.reshape(batch * seq, d)
    tables = _rope_tables(positions)
    w_in_t = jnp.swapaxes(w_in, 1, 2).astype(BF16)
    wo, wg, wu, wd = (w.astype(BF16) for w in (w_out, w_gate, w_up, w_down))
    for i in range(depth):
        gain = attn_norm[i].reshape(1, d)
        if i % N_MIXERS == 0:
            qt, k, vt = _qkv_proj(x2, gain, w_in_t, i, batch, seq, SB_HEAD_DIM ** -0.5 * LOG2E)
            o = _sb_attention(qt, k, vt, batch, seq)
        else:
            j = i // N_MIXERS
            qg = jnp.broadcast_to(q_norm[j].reshape(-1, 1), (DIFF_HEAD_DIM, LANES))
            kg = jnp.broadcast_to(k_norm[j].reshape(-1, 1), (DIFF_HEAD_DIM, LANES))
            qt, k, vt = _qkv_proj(x2, gain, w_in_t, i, batch, seq, DIFF_HEAD_DIM ** -0.5 * LOG2E,
                                  (qg, kg) + tables)
            lam_init = 0.8 - 0.6 * math.exp(-0.3 * i)
            o = _diff_attention(qt, k, vt, lambda_q1[j].reshape(1, -1), lambda_k1[j].reshape(1, -1),
                                lambda_q2[j].reshape(1, -1), lambda_k2[j].reshape(1, -1),
                                sub_norm[j].reshape(1, -1), batch, seq, lam_init)
        x2 = _mix_ffn(x2, o, ffn_norm[i].reshape(1, d), wo, wg, wu, wd, i)
    return x2.reshape(batch, seq, d)
```

```python
import functools
import math

import jax
import jax.numpy as jnp
from jax import lax
from jax.experimental import pallas as pl
from jax.experimental.pallas import tpu as pltpu

F32 = jnp.float32
BF16 = jnp.bfloat16

N_MIXERS = 2
SB_HEAD_DIM = 64
DIFF_HEAD_DIM = 64
DIFF_V_DIM = 2 * DIFF_HEAD_DIM
CHUNK = 64
ROPE_THETA = 500000.0
ROT_DIM = DIFF_HEAD_DIM // 4
EPS = 1e-6
NEG_INF = -1e30

LANES = 128
VMEM_LIMIT_BYTES = 56 * 1024 * 1024

QKV_TILE = 1024
TOKEN_TILE = 512
SB_Q_TILE = 256
SB_SUB = 128
SB_PRE_SUBS = 2
SB_NEXT_SUBS = 2
DIFF_TILE = 512
DIFF_STRIP = 128
DIFF_OUT_GROUP = 4
FFN_CHUNKS = ((0, 1024), (1024, 2048), (2048, 2816))

LOG2E = 1.4426950408889634
F32_EXP2_UNDERFLOW = -150.0


def _rms(x, gain):
    ms = jnp.mean(x * x, axis=-1, keepdims=True)
    return x * lax.rsqrt(ms + EPS) * gain


def _const_spec(shape):
    nd = len(shape)
    return pl.BlockSpec(shape, lambda *_: (0,) * nd, pipeline_mode=pl.Buffered(1))


def _layer_spec(stacked, layer):
    return pl.BlockSpec((1,) + stacked.shape[1:], lambda *_: (layer, 0, 0),
                        pipeline_mode=pl.Buffered(1))


def _project_t(wt_ref, h, part, d_model):
    w = wt_ref[0, part * d_model:(part + 1) * d_model, :]
    return lax.dot_general(w, h, (((1,), (1,)), ((), ())), preferred_element_type=F32)


def _qkv_kernel(x_ref, g_ref, wt_ref, qt_ref, k_ref, vt_ref, *, d_model, scale):
    h = _rms(x_ref[...], g_ref[...]).astype(BF16)
    kt = _project_t(wt_ref, h, 1, d_model)
    for c in range(d_model // LANES):
        sl = slice(c * LANES, (c + 1) * LANES)
        k_ref[:, sl] = kt[sl, :].T.astype(BF16)
    qt_ref[0] = (_project_t(wt_ref, h, 0, d_model) * scale).astype(BF16)
    vt_ref[0] = _project_t(wt_ref, h, 2, d_model).astype(BF16)


def _qkv_diff_kernel(x_ref, g_ref, wt_ref, qg_ref, kg_ref, cost_ref, sint_ref,
                     qt_ref, k_ref, vt_ref, *, d_model, scale):
    h = _rms(x_ref[...], g_ref[...]).astype(BF16)
    tm = h.shape[0]
    cos_t, sin_t = cost_ref[0], sint_ref[0]
    half = ROT_DIM // 2

    project = functools.partial(_project_t, wt_ref, h, d_model=d_model)

    def norm_rope(xg, gain):
        ms = jnp.mean(xg * xg, axis=0, keepdims=True)
        y = xg * lax.rsqrt(ms + EPS) * gain
        x1, x2 = y[:half], y[half:ROT_DIM]
        return jnp.concatenate([x1 * cos_t - x2 * sin_t, x2 * cos_t + x1 * sin_t, y[ROT_DIM:]],
                               axis=0)

    def slabs(xt, gain):
        per_slab = LANES // DIFF_HEAD_DIM
        for c in range(d_model // LANES):
            parts = [norm_rope(xt[g * DIFF_HEAD_DIM:(g + 1) * DIFF_HEAD_DIM, :], gain)
                     for g in range(c * per_slab, (c + 1) * per_slab)]
            yield slice(c * LANES, (c + 1) * LANES), jnp.concatenate(parts, axis=0)

    for sl, y in slabs(project(1), jnp.tile(kg_ref[...], (1, tm // LANES))):
        k_ref[:, sl] = y.T.astype(BF16)
    for sl, y in slabs(project(0), jnp.tile(qg_ref[...], (1, tm // LANES))):
        qt_ref[0, sl, :] = (y * scale).astype(BF16)
    vt_ref[0] = project(2).astype(BF16)


def _qkv_proj(x2, gain, wt, layer, batch, seq, scale, diff_args=None):
    n, d = x2.shape
    tm = QKV_TILE
    tiles_per_seq = seq // tm
    row_spec = pl.BlockSpec((tm, d), lambda i: (i, 0))
    t_spec = pl.BlockSpec((1, d, tm), lambda i: (i // tiles_per_seq, 0, i % tiles_per_seq))
    row_shape = jax.ShapeDtypeStruct((n, d), BF16)
    t_shape = jax.ShapeDtypeStruct((batch, d, seq), BF16)
    if diff_args is None:
        body = functools.partial(_qkv_kernel, d_model=d, scale=scale)
        name = "qkv_sb"
        in_specs = [row_spec, _const_spec((1, d)), _layer_spec(wt, layer)]
        args = [x2, gain, wt]
    else:
        qg, kg, cos_t, sin_t = diff_args
        tabt_spec = pl.BlockSpec((1, ROT_DIM // 2, tm),
                                 lambda i: (i // tiles_per_seq, 0, i % tiles_per_seq))
        body = functools.partial(_qkv_diff_kernel, d_model=d, scale=scale)
        name = "qkv_diff"
        in_specs = [row_spec, _const_spec((1, d)), _layer_spec(wt, layer),
                    _const_spec(qg.shape), _const_spec(kg.shape), tabt_spec, tabt_spec]
        args = [x2, gain, wt, qg, kg, cos_t, sin_t]
    return pl.pallas_call(
        body,
        grid=(n // tm,),
        in_specs=in_specs,
        out_specs=[t_spec, row_spec, t_spec],
        out_shape=[t_shape, row_shape, t_shape],
        compiler_params=pltpu.CompilerParams(dimension_semantics=("arbitrary",),
                                             vmem_limit_bytes=VMEM_LIMIT_BYTES),
        name=name,
    )(*args)


def _sb_attn_kernel(qt_ref, k_ref, vt_ref, o_ref, z_ref, w_ref, w0_ref, c_ref, accx_ref, call_ref):
    tq, sub = SB_Q_TILE, SB_SUB
    half = tq // 2
    gw = 2 * half
    n_tiles = k_ref.shape[0] // tq
    n_sub = SB_PRE_SUBS + tq // sub
    r_u = lax.broadcasted_iota(jnp.int32, (sub, 2 * sub), 0)
    c_u = lax.broadcasted_iota(jnp.int32, (sub, 2 * sub), 1) % sub
    later2 = jnp.where(c_u > r_u, 1.0, 0.0).astype(BF16)
    tri = (lax.broadcasted_iota(jnp.int32, (sub, gw), 0)
           < lax.broadcasted_iota(jnp.int32, (sub, gw), 1) % half)

    def q4x(t):
        qt = qt_ref[0, :, pl.ds(pl.multiple_of(t * tq, tq), tq)]
        row = lax.broadcasted_iota(jnp.int32, (LANES, half), 0)
        zero = jnp.zeros((LANES, half), qt.dtype)
        parts = []
        for g in range(2):
            qg = qt[:, g * half:(g + 1) * half]
            parts += [jnp.where(row < SB_HEAD_DIM, qg, zero), jnp.where(row >= SB_HEAD_DIM, qg, zero)]
        return jnp.concatenate(parts, axis=1)

    def first_block(t):
        return jnp.maximum(t * (tq // sub) - SB_PRE_SUBS, 0)

    def first_key(t):
        return pl.multiple_of(first_block(t) * sub, sub)

    def scores(t):
        kb = k_ref[pl.ds(first_key(t), n_sub * sub), :]
        return jnp.dot(kb, q4x(t), preferred_element_type=F32)

    def sub_block(z, c, mask):
        if mask is not None:
            z = jnp.where(mask, z, NEG_INF)
        nabs = pltpu.bitcast(pltpu.bitcast(z, jnp.uint32) | jnp.uint32(0x80000000), F32)
        sp = jnp.log(1.0 + jnp.exp2(nabs)) * LOG2E
        ls = jnp.minimum(z, 0.0) - sp
        lk = ls - z
        hi = lk.astype(BF16)
        lo = (lk - hi.astype(F32)).astype(BF16)
        after = jnp.dot(later2, jnp.concatenate([hi, lo], axis=0), preferred_element_type=F32)
        w = jnp.exp2(ls + after + c)
        return w.astype(BF16), c + after[0:1, :] + lk[0:1, :]

    def first_step(first_sb, dst_ref, skip_first):
        zero_c = jnp.zeros((1, gw), F32)
        lo_rows = slice(first_sb * sub, (first_sb + 1) * sub)
        hi_rows = slice((first_sb + 1) * sub, (first_sb + 2) * sub)
        wb, cb = sub_block(z_ref[hi_rows, gw:], zero_c, tri)
        dst_ref[hi_rows, gw:] = wb
        wa, ca = sub_block(z_ref[lo_rows, :gw], zero_c, tri)
        wb, cb = sub_block(z_ref[lo_rows, gw:], cb, None)
        dst_ref[lo_rows, :gw] = wa
        dst_ref[lo_rows, gw:] = wb
        c = jnp.concatenate([ca, cb], axis=1)
        for sb in reversed(range(1 if skip_first else 0, first_sb)):
            w, c = sub_block(z_ref[sb * sub:(sb + 1) * sub, :], c, None)
            dst_ref[sb * sub:(sb + 1) * sub, :] = w
        if skip_first:
            wa, ca = sub_block(z_ref[0:sub, :gw], c[:, :gw], None)
            dst_ref[0:sub, :gw] = wa
            c = jnp.concatenate([ca, c[:, gw:]], axis=1)
        c_ref[...] = c
        return jnp.max(c)

    def extra_steps(t, c_max):
        def cond(carry):
            j, c_max = carry
            return (j > 0) & (c_max > F32_EXP2_UNDERFLOW)

        def body(carry):
            j, _ = carry
            j = j - SB_NEXT_SUBS
            start = pl.multiple_of(j * sub, sub)
            kb = k_ref[pl.ds(start, SB_NEXT_SUBS * sub), :]
            z_all = jnp.dot(kb, q4x(t), preferred_element_type=F32)
            c = c_ref[...]
            ws = [None] * SB_NEXT_SUBS
            for sb in reversed(range(SB_NEXT_SUBS)):
                ws[sb], c = sub_block(z_all[sb * sub:(sb + 1) * sub], c, None)
            vtb = vt_ref[0, :, pl.ds(start, SB_NEXT_SUBS * sub)]
            accx_ref[...] += jnp.dot(vtb, jnp.concatenate(ws, axis=0), preferred_element_type=F32)
            c_ref[...] = c
            return j, jnp.max(c)

        lax.while_loop(cond, body, (first_block(t), c_max))

    def write_out(t, acc):
        row_o = lax.broadcasted_iota(jnp.int32, (LANES, half), 0)
        out_t = jnp.concatenate(
            [jnp.where(row_o < SB_HEAD_DIM, acc[:, g * gw:g * gw + half],
                       acc[:, g * gw + half:(g + 1) * gw]) for g in range(2)], axis=1)
        o_ref[pl.ds(pl.multiple_of(t * tq, tq), tq), :] = out_t.T.astype(o_ref.dtype)

    def finish(t, with_extra):
        vtb = vt_ref[0, :, pl.ds(first_key(t), n_sub * sub)]
        acc = jnp.dot(vtb, w_ref[...], preferred_element_type=F32)
        if with_extra:
            acc = acc + accx_ref[...]
        write_out(t, acc)

    w_ref[...] = jnp.zeros_like(w_ref)
    w0_ref[...] = jnp.zeros_like(w0_ref)
    z_ref[...] = scores(0)
    first_step(0, w0_ref, False)
    z_ref[...] = scores(1)
    write_out(0, jnp.dot(vt_ref[0, :, 0:tq], w0_ref[...], preferred_element_type=F32))

    def body(t, carry):
        finish(jnp.maximum(t - 1, 1), False)
        z_next = scores(jnp.minimum(t + 1, n_tiles - 1))
        first_step(SB_PRE_SUBS, w_ref, True)
        call_ref[t] = c_ref[...]
        z_ref[...] = z_next
        return carry

    lax.fori_loop(1, n_tiles, body, 0)
    finish(n_tiles - 1, False)

    @pl.when(jnp.max(call_ref[1:]) > F32_EXP2_UNDERFLOW)
    def _():
        def redo(t, carry):
            @pl.when(jnp.max(call_ref[t]) > F32_EXP2_UNDERFLOW)
            def _():
                z_ref[...] = scores(t)
                accx_ref[...] = jnp.zeros_like(accx_ref)
                extra_steps(t, first_step(SB_PRE_SUBS, w_ref, False))
                finish(t, True)
            return carry

        lax.fori_loop(1, n_tiles, redo, 0)


def _sb_attention(qt, k, vt, batch, seq):
    n, d = k.shape
    tq = SB_Q_TILE
    rows = SB_PRE_SUBS * SB_SUB + tq
    assert seq >= rows and seq // tq >= 2
    return pl.pallas_call(
        _sb_attn_kernel,
        grid=(batch, d // LANES),
        in_specs=[pl.BlockSpec((1, LANES, seq), lambda b, p: (b, p, 0)),
                  pl.BlockSpec((seq, LANES), lambda b, p: (b, p)),
                  pl.BlockSpec((1, LANES, seq), lambda b, p: (b, p, 0))],
        out_specs=pl.BlockSpec((seq, LANES), lambda b, p: (b, p)),
        out_shape=jax.ShapeDtypeStruct((n, d), BF16),
        scratch_shapes=[pltpu.VMEM((rows, 2 * tq), F32), pltpu.VMEM((rows, 2 * tq), BF16),
                        pltpu.VMEM((tq, 2 * tq), BF16),
                        pltpu.VMEM((1, 2 * tq), F32), pltpu.VMEM((LANES, 2 * tq), F32),
                        pltpu.VMEM((seq // tq, 1, 2 * tq), F32)],
        compiler_params=pltpu.CompilerParams(dimension_semantics=("arbitrary",) * 2,
                                             vmem_limit_bytes=VMEM_LIMIT_BYTES),
        name="sb_attention",
    )(qt, k, vt)


def _diff_attn_kernel(qt_ref, k_ref, vt_ref, lq1_ref, lk1_ref, lq2_ref, lk2_ref, sub_ref, o_ref,
                      m_ref, l_ref, acc_ref, alpha_ref, s_ref, p_ref, smax_ref, *, lam_init):
    t = DIFF_TILE
    n_tiles = k_ref.shape[0] // t

    def q2x(i):
        qt = qt_ref[0, :, pl.ds(pl.multiple_of(i * t, t), t)]
        row = lax.broadcasted_iota(jnp.int32, qt.shape, 0)
        zero = jnp.zeros_like(qt)
        return jnp.concatenate([jnp.where(row < DIFF_HEAD_DIM, qt, zero),
                                jnp.where(row >= DIFF_HEAD_DIM, qt, zero)], axis=1)

    def scores(i, j):
        kb = k_ref[pl.ds(pl.multiple_of(j * t, t), t), :]
        return jnp.dot(kb, q2x(i), preferred_element_type=F32)

    def stage_scores(s):
        s_ref[...] = s
        smax_ref[...] = jnp.max(s.reshape(t // 8, 8, 2 * t), axis=0)

    def softmax_next(i):
        s = s_ref[...]
        m_old = m_ref[i]
        m_new = jnp.maximum(m_old, jnp.max(smax_ref[...], axis=0, keepdims=True))
        p = jnp.exp2(s - m_new)
        alpha = jnp.exp2(m_old - m_new)
        l_ref[i] = alpha * l_ref[i] + jnp.sum(p, axis=0, keepdims=True)
        p_ref[...] = p.astype(BF16)
        alpha_ref[...] = alpha
        m_ref[i] = m_new

    def pv(j):
        vtb = vt_ref[0, :, pl.ds(pl.multiple_of(j * t, t), t)]
        return jnp.dot(vtb, p_ref[...], preferred_element_type=F32)

    strip = DIFF_STRIP
    n_strips = t // strip

    def late(x, r):
        return x if r == 0 else jnp.concatenate([x[:, r * strip:t], x[:, t + r * strip:]], axis=1)

    def merge(full, part, r):
        if r == 0:
            return part
        w = t - r * strip
        return jnp.concatenate([full[:, 0:r * strip], part[:, 0:w], full[:, t:t + r * strip],
                                part[:, w:]], axis=1)

    def rows(r):
        return slice(r * strip, (r + 1) * strip)

    def scores_first(i):
        start = pl.multiple_of(i * t, t)
        q = q2x(i)
        return [jnp.dot(k_ref[pl.ds(start + r * strip, strip), :], late(q, r),
                        preferred_element_type=F32) for r in range(n_strips)]

    def stage_first(pieces):
        for r, piece in enumerate(pieces):
            s_ref[rows(r), 0:piece.shape[1]] = piece

    def softmax_first(i):
        pieces = []
        for r in range(n_strips):
            w = t - r * strip
            sr = s_ref[rows(r), 0:2 * w]
            key_chunk = lax.broadcasted_iota(jnp.int32, sr.shape, 0) // CHUNK
            q_local = lax.broadcasted_iota(jnp.int32, sr.shape, 1) % w
            pieces.append(jnp.where((q_local >= strip) | (key_chunk <= q_local // CHUNK), sr, NEG_INF))
        m_new = jnp.max(pieces[0], axis=0, keepdims=True)
        for r in range(1, n_strips):
            m_r = jnp.maximum(late(m_new, r), jnp.max(pieces[r], axis=0, keepdims=True))
            m_new = merge(m_new, m_r, r)
        l_new = None
        for r in range(n_strips):
            p = jnp.exp2(pieces[r] - late(m_new, r))
            p_ref[rows(r), 0:p.shape[1]] = p.astype(BF16)
            l_r = jnp.sum(p, axis=0, keepdims=True)
            l_new = l_r if r == 0 else merge(l_new, late(l_new, r) + l_r, r)
        l_ref[i] = l_new
        m_ref[i] = m_new

    def pv_first(j):
        start = pl.multiple_of(j * t, t)
        acc = None
        for r in range(n_strips):
            w = t - r * strip
            d = jnp.dot(vt_ref[0, :, pl.ds(start + r * strip, strip)], p_ref[rows(r), 0:2 * w],
                        preferred_element_type=F32)
            acc = d if r == 0 else merge(acc, late(acc, r) + d, r)
        return acc

    stage_first(scores_first(0))
    softmax_first(0)
    stage_first(scores_first(1))

    def diag_body(i, carry):
        acc_ref[i - 1] = pv_first(i - 1)
        s_next = scores_first(jnp.minimum(i + 1, n_tiles - 1))
        softmax_first(i)
        stage_first(s_next)
        return carry

    lax.fori_loop(1, n_tiles, diag_body, 0)
    acc_ref[n_tiles - 1] = pv_first(n_tiles - 1)

    def next_pair(i, j):
        wrap = j + 1 >= i
        return jnp.where(wrap, i + 1, i), jnp.where(wrap, 0, j + 1)

    stage_scores(scores(1, 0))
    softmax_next(1)
    i1, j1 = next_pair(1, 0)
    i1c = jnp.minimum(i1, n_tiles - 1)
    stage_scores(scores(i1c, jnp.minimum(j1, i1c - 1)))
    n_pairs = n_tiles * (n_tiles - 1) // 2

    def pair_body(n, carry):
        ip, jp, i, j = carry
        i2, j2 = next_pair(i, j)
        i2c = jnp.minimum(i2, n_tiles - 1)
        kb = k_ref[pl.ds(pl.multiple_of(jnp.minimum(j2, i2c - 1) * t, t), t), :]
        q_next = q2x(i2c)
        s_map1 = jnp.dot(kb, q_next[:, :t], preferred_element_type=F32)
        acc_ref[ip] = alpha_ref[...] * acc_ref[ip] + pv(jp)
        s_map2 = jnp.dot(kb, q_next[:, t:], preferred_element_type=F32)
        s_next = jnp.concatenate([s_map1, s_map2], axis=1)
        softmax_next(i)
        stage_scores(s_next)
        return i, j, i2, j2

    ip, jp, _, _ = lax.fori_loop(1, n_pairs, pair_body, (1, 0, i1, j1))
    acc_ref[ip] = alpha_ref[...] * acc_ref[ip] + pv(jp)

    lam = (jnp.exp(jnp.sum(lq1_ref[...] * lk1_ref[...], axis=-1, keepdims=True))
           - jnp.exp(jnp.sum(lq2_ref[...] * lk2_ref[...], axis=-1, keepdims=True)) + lam_init)

    group = DIFF_OUT_GROUP if n_tiles % DIFF_OUT_GROUP == 0 else 1

    def out_body(ig, carry):
        for u in range(group):
            i = ig * group + u
            on = acc_ref[i] / l_ref[i]
            o = on[:, :t] - lam * on[:, t:]
            ms = jnp.mean(o * o, axis=0, keepdims=True)
            o = o * lax.rsqrt(ms + EPS)
            o = o.T * sub_ref[...] * (1.0 - lam_init)
            o_ref[pl.ds(pl.multiple_of(i * t, t), t), :] = o.astype(o_ref.dtype)
        return carry

    lax.fori_loop(0, n_tiles // group, out_body, 0)


def _diff_attention(qt, k, vt, lq1, lk1, lq2, lk2, sub_g, batch, seq, lam_init):
    n, d = k.shape
    t = DIFF_TILE
    n_tiles = seq // t
    assert n_tiles >= 2
    small = _const_spec((1, DIFF_HEAD_DIM))
    return pl.pallas_call(
        functools.partial(_diff_attn_kernel, lam_init=lam_init),
        grid=(batch, d // LANES),
        in_specs=[pl.BlockSpec((1, LANES, seq), lambda b, h: (b, h, 0)),
                  pl.BlockSpec((seq, LANES), lambda b, h: (b, h)),
                  pl.BlockSpec((1, LANES, seq), lambda b, h: (b, h, 0)),
                  small, small, small, small, _const_spec((1, LANES))],
        out_specs=pl.BlockSpec((seq, LANES), lambda b, h: (b, h)),
        out_shape=jax.ShapeDtypeStruct((n, d), BF16),
        scratch_shapes=[pltpu.VMEM((n_tiles, 1, 2 * t), F32), pltpu.VMEM((n_tiles, 1, 2 * t), F32),
                        pltpu.VMEM((n_tiles, LANES, 2 * t), F32), pltpu.VMEM((1, 2 * t), F32),
                        pltpu.VMEM((t, 2 * t), F32), pltpu.VMEM((t, 2 * t), BF16),
                        pltpu.VMEM((8, 2 * t), F32)],
        compiler_params=pltpu.CompilerParams(dimension_semantics=("arbitrary",) * 2,
                                             vmem_limit_bytes=VMEM_LIMIT_BYTES),
        name="diff_attention",
    )(qt, k, vt, lq1, lk1, lq2, lk2, sub_g)


def _mix_ffn_kernel(x_ref, o_ref, wo_ref, g_ref, wg_ref, wu_ref, wd_ref, out_ref):
    tm = x_ref.shape[0]
    halves = (slice(0, tm // 2), slice(tm // 2, tm))
    x1 = [x_ref[r, :] + jnp.dot(o_ref[r, :], wo_ref[0], preferred_element_type=F32) for r in halves]
    hs = [_rms(v, g_ref[...]).astype(BF16) for v in x1]
    for r, acc, h in zip(halves, x1, hs):
        for lo, hi in FFN_CHUNKS:
            g = jnp.dot(h, wg_ref[0, :, lo:hi], preferred_element_type=F32)
            u = jnp.dot(h, wu_ref[0, :, lo:hi], preferred_element_type=F32)
            a = (g * (1.0 / (1.0 + jnp.exp(-g))) * u).astype(BF16)
            acc = acc + jnp.dot(a, wd_ref[0, lo:hi, :], preferred_element_type=F32)
        out_ref[r, :] = acc


def _mix_ffn(x2, o, gain, wo, wg, wu, wd, layer):
    n, d = x2.shape
    tm = TOKEN_TILE
    assert FFN_CHUNKS[-1][1] == wg.shape[2]
    row_spec = pl.BlockSpec((tm, d), lambda i: (i, 0))
    return pl.pallas_call(
        _mix_ffn_kernel,
        grid=(n // tm,),
        in_specs=[row_spec, row_spec, _layer_spec(wo, layer), _const_spec((1, d)),
                  _layer_spec(wg, layer), _layer_spec(wu, layer), _layer_spec(wd, layer)],
        out_specs=row_spec,
        out_shape=jax.ShapeDtypeStruct((n, d), F32),
        compiler_params=pltpu.CompilerParams(dimension_semantics=("arbitrary",),
                                             vmem_limit_bytes=VMEM_LIMIT_BYTES),
        name="mix_ffn",
    )(x2, o, wo, gain, wg, wu, wd)


def _rope_tables(positions):
    inv_freq = ROPE_THETA ** (-jnp.arange(0, ROT_DIM, 2, dtype=F32) / ROT_DIM)
    ang = positions.astype(F32)[:, None, :] * inv_freq[None, :, None]
    return jnp.cos(ang), jnp.sin(ang)


def kernel(x, positions, attn_norm, w_in, w_out, q_norm, k_norm, lambda_q1, lambda_k1, lambda_q2,
           lambda_k2, sub_norm, ffn_norm, w_gate, w_up, w_down):
    batch, seq, d = x.shape
    depth = w_in.shape[0]
    assert seq % QKV_TILE == 0 and seq % TOKEN_TILE == 0
    assert seq % SB_Q_TILE == 0 and seq % DIFF_TILE == 0
    assert d % LANES == 0 and DIFF_V_DIM == LANES and DIFF_TILE % CHUNK == 0
    x2 = x.reshape(batch * seq, d)
    tables = _rope_tables(positions)
    w_in_t = jnp.swapaxes(w_in, 1, 2).astype(BF16)
    wo, wg, wu, wd = (w.astype(BF16) for w in (w_out, w_gate, w_up, w_down))
    for i in range(depth):
        gain = attn_norm[i].reshape(1, d)
        if i % N_MIXERS == 0:
            qt, k, vt = _qkv_proj(x2, gain, w_in_t, i, batch, seq, SB_HEAD_DIM ** -0.5 * LOG2E)
            o = _sb_attention(qt, k, vt, batch, seq)
        else:
            j = i // N_MIXERS
            qg = jnp.broadcast_to(q_norm[j].reshape(-1, 1), (DIFF_HEAD_DIM, LANES))
            kg = jnp.broadcast_to(k_norm[j].reshape(-1, 1), (DIFF_HEAD_DIM, LANES))
            qt, k, vt = _qkv_proj(x2, gain, w_in_t, i, batch, seq, DIFF_HEAD_DIM ** -0.5 * LOG2E,
                                  (qg, kg) + tables)
            lam_init = 0.8 - 0.6 * math.exp(-0.3 * i)
            o = _diff_attention(qt, k, vt, lambda_q1[j].reshape(1, -1), lambda_k1[j].reshape(1, -1),
                                lambda_q2[j].reshape(1, -1), lambda_k2[j].reshape(1, -1),
                                sub_norm[j].reshape(1, -1), batch, seq, lam_init)
        x2 = _mix_ffn(x2, o, ffn_norm[i].reshape(1, d), wo, wg, wu, wd, i)
    return x2.reshape(batch, seq, d)
```

```python
import functools
import math

import jax
import jax.numpy as jnp
from jax import lax
from jax.experimental import pallas as pl
from jax.experimental.pallas import tpu as pltpu

F32 = jnp.float32
BF16 = jnp.bfloat16

N_MIXERS = 2
SB_HEAD_DIM = 64
DIFF_HEAD_DIM = 64
DIFF_V_DIM = 2 * DIFF_HEAD_DIM
CHUNK = 64
ROPE_THETA = 500000.0
ROT_DIM = DIFF_HEAD_DIM // 4
EPS = 1e-6
NEG_INF = -1e30

LANES = 128
VMEM_LIMIT_BYTES = 56 * 1024 * 1024

QKV_TILE = 1024
TOKEN_TILE = 512
SB_Q_TILE = 256
SB_SUB = 128
SB_PRE_SUBS = 2
SB_NEXT_SUBS = 2
DIFF_TILE = 512
DIFF_STRIP = 128
DIFF_OUT_GROUP = 4
FFN_CHUNKS = ((0, 1024), (1024, 2048), (2048, 2816))

LOG2E = 1.4426950408889634
F32_EXP2_UNDERFLOW = -150.0


def _rms(x, gain):
    ms = jnp.mean(x * x, axis=-1, keepdims=True)
    return x * lax.rsqrt(ms + EPS) * gain


def _const_spec(shape):
    nd = len(shape)
    return pl.BlockSpec(shape, lambda *_: (0,) * nd, pipeline_mode=pl.Buffered(1))


def _layer_spec(stacked, layer):
    return pl.BlockSpec((1,) + stacked.shape[1:], lambda *_: (layer, 0, 0),
                        pipeline_mode=pl.Buffered(1))


def _project_t(wt_ref, h, part, d_model):
    w = wt_ref[0, part * d_model:(part + 1) * d_model, :]
    return lax.dot_general(w, h, (((1,), (1,)), ((), ())), preferred_element_type=F32)


def _qkv_kernel(x_ref, g_ref, wt_ref, qt_ref, k_ref, vt_ref, *, d_model, scale):
    h = _rms(x_ref[...], g_ref[...]).astype(BF16)
    kt = _project_t(wt_ref, h, 1, d_model)
    for c in range(d_model // LANES):
        sl = slice(c * LANES, (c + 1) * LANES)
        k_ref[:, sl] = kt[sl, :].T.astype(BF16)
    qt_ref[0] = (_project_t(wt_ref, h, 0, d_model) * scale).astype(BF16)
    vt_ref[0] = _project_t(wt_ref, h, 2, d_model).astype(BF16)


def _qkv_diff_kernel(x_ref, g_ref, wt_ref, qg_ref, kg_ref, cost_ref, sint_ref,
                     qt_ref, k_ref, vt_ref, *, d_model, scale):
    h = _rms(x_ref[...], g_ref[...]).astype(BF16)
    tm = h.shape[0]
    cos_t, sin_t = cost_ref[0], sint_ref[0]
    half = ROT_DIM // 2

    project = functools.partial(_project_t, wt_ref, h, d_model=d_model)

    def norm_rope(xg, gain):
        ms = jnp.mean(xg * xg, axis=0, keepdims=True)
        y = xg * lax.rsqrt(ms + EPS) * gain
        x1, x2 = y[:half], y[half:ROT_DIM]
        return jnp.concatenate([x1 * cos_t - x2 * sin_t, x2 * cos_t + x1 * sin_t, y[ROT_DIM:]],
                               axis=0)

    def slabs(xt, gain):
        per_slab = LANES // DIFF_HEAD_DIM
        for c in range(d_model // LANES):
            parts = [norm_rope(xt[g * DIFF_HEAD_DIM:(g + 1) * DIFF_HEAD_DIM, :], gain)
                     for g in range(c * per_slab, (c + 1) * per_slab)]
            yield slice(c * LANES, (c + 1) * LANES), jnp.concatenate(parts, axis=0)

    for sl, y in slabs(project(1), jnp.tile(kg_ref[...], (1, tm // LANES))):
        k_ref[:, sl] = y.T.astype(BF16)
    for sl, y in slabs(project(0), jnp.tile(qg_ref[...], (1, tm // LANES))):
        qt_ref[0, sl, :] = (y * scale).astype(BF16)
    vt_ref[0] = project(2).astype(BF16)


def _qkv_proj(x2, gain, wt, layer, batch, seq, scale, diff_args=None):
    n, d = x2.shape
    tm = QKV_TILE
    tiles_per_seq = seq // tm
    row_spec = pl.BlockSpec((tm, d), lambda i: (i, 0))
    t_spec = pl.BlockSpec((1, d, tm), lambda i: (i // tiles_per_seq, 0, i % tiles_per_seq))
    row_shape = jax.ShapeDtypeStruct((n, d), BF16)
    t_shape = jax.ShapeDtypeStruct((batch, d, seq), BF16)
    if diff_args is None:
        body = functools.partial(_qkv_kernel, d_model=d, scale=scale)
        name = "qkv_sb"
        in_specs = [row_spec, _const_spec((1, d)), _layer_spec(wt, layer)]
        args = [x2, gain, wt]
    else:
        qg, kg, cos_t, sin_t = diff_args
        tabt_spec = pl.BlockSpec((1, ROT_DIM // 2, tm),
                                 lambda i: (i // tiles_per_seq, 0, i % tiles_per_seq))
        body = functools.partial(_qkv_diff_kernel, d_model=d, scale=scale)
        name = "qkv_diff"
        in_specs = [row_spec, _const_spec((1, d)), _layer_spec(wt, layer),
                    _const_spec(qg.shape), _const_spec(kg.shape), tabt_spec, tabt_spec]
        args = [x2, gain, wt, qg, kg, cos_t, sin_t]
    return pl.pallas_call(
        body,
        grid=(n // tm,),
        in_specs=in_specs,
        out_specs=[t_spec, row_spec, t_spec],
        out_shape=[t_shape, row_shape, t_shape],
        compiler_params=pltpu.CompilerParams(dimension_semantics=("arbitrary",),
                                             vmem_limit_bytes=VMEM_LIMIT_BYTES),
        name=name,
    )(*args)


def _sb_attn_kernel(qt_ref, k_ref, vt_ref, o_ref, z_ref, w_ref, w0_ref, c_ref, accx_ref, call_ref):
    tq, sub = SB_Q_TILE, SB_SUB
    half = tq // 2
    gw = 2 * half
    n_tiles = k_ref.shape[0] // tq
    n_sub = SB_PRE_SUBS + tq // sub
    r_u = lax.broadcasted_iota(jnp.int32, (sub, 2 * sub), 0)
    c_u = lax.broadcasted_iota(jnp.int32, (sub, 2 * sub), 1) % sub
    later2 = jnp.where(c_u > r_u, 1.0, 0.0).astype(BF16)
    tri = (lax.broadcasted_iota(jnp.int32, (sub, gw), 0)
           < lax.broadcasted_iota(jnp.int32, (sub, gw), 1) % half)

    def q4x(t):
        qt = qt_ref[0, :, pl.ds(pl.multiple_of(t * tq, tq), tq)]
        row = lax.broadcasted_iota(jnp.int32, (LANES, half), 0)
        zero = jnp.zeros((LANES, half), qt.dtype)
        parts = []
        for g in range(2):
            qg = qt[:, g * half:(g + 1) * half]
            parts += [jnp.where(row < SB_HEAD_DIM, qg, zero), jnp.where(row >= SB_HEAD_DIM, qg, zero)]
        return jnp.concatenate(parts, axis=1)

    def first_block(t):
        return jnp.maximum(t * (tq // sub) - SB_PRE_SUBS, 0)

    def first_key(t):
        return pl.multiple_of(first_block(t) * sub, sub)

    def scores(t):
        kb = k_ref[pl.ds(first_key(t), n_sub * sub), :]
        return jnp.dot(kb, q4x(t), preferred_element_type=F32)

    def sub_block(z, c, mask):
        if mask is not None:
            z = jnp.where(mask, z, NEG_INF)
        nabs = pltpu.bitcast(pltpu.bitcast(z, jnp.uint32) | jnp.uint32(0x80000000), F32)
        sp = jnp.log(1.0 + jnp.exp2(nabs)) * LOG2E
        ls = jnp.minimum(z, 0.0) - sp
        lk = ls - z
        hi = lk.astype(BF16)
        lo = (lk - hi.astype(F32)).astype(BF16)
        after = jnp.dot(later2, jnp.concatenate([hi, lo], axis=0), preferred_element_type=F32)
        w = jnp.exp2(ls + after + c)
        return w.astype(BF16), c + after[0:1, :] + lk[0:1, :]

    def first_step(first_sb, dst_ref, skip_first):
        zero_c = jnp.zeros((1, gw), F32)
        lo_rows = slice(first_sb * sub, (first_sb + 1) * sub)
        hi_rows = slice((first_sb + 1) * sub, (first_sb + 2) * sub)
        wb, cb = sub_block(z_ref[hi_rows, gw:], zero_c, tri)
        dst_ref[hi_rows, gw:] = wb
        wa, ca = sub_block(z_ref[lo_rows, :gw], zero_c, tri)
        wb, cb = sub_block(z_ref[lo_rows, gw:], cb, None)
        dst_ref[lo_rows, :gw] = wa
        dst_ref[lo_rows, gw:] = wb
        c = jnp.concatenate([ca, cb], axis=1)
        for sb in reversed(range(1 if skip_first else 0, first_sb)):
            w, c = sub_block(z_ref[sb * sub:(sb + 1) * sub, :], c, None)
            dst_ref[sb * sub:(sb + 1) * sub, :] = w
        if skip_first:
            wa, ca = sub_block(z_ref[0:sub, :gw], c[:, :gw], None)
            dst_ref[0:sub, :gw] = wa
            c = jnp.concatenate([ca, c[:, gw:]], axis=1)
        c_ref[...] = c
        return jnp.max(c)

    def extra_steps(t, c_max):
        def cond(carry):
            j, c_max = carry
            return (j > 0) & (c_max > F32_EXP2_UNDERFLOW)

        def body(carry):
            j, _ = carry
            j = j - SB_NEXT_SUBS
            start = pl.multiple_of(j * sub, sub)
            kb = k_ref[pl.ds(start, SB_NEXT_SUBS * sub), :]
            z_all = jnp.dot(kb, q4x(t), preferred_element_type=F32)
            c = c_ref[...]
            ws = [None] * SB_NEXT_SUBS
            for sb in reversed(range(SB_NEXT_SUBS)):
                ws[sb], c = sub_block(z_all[sb * sub:(sb + 1) * sub], c, None)
            vtb = vt_ref[0, :, pl.ds(start, SB_NEXT_SUBS * sub)]
            accx_ref[...] += jnp.dot(vtb, jnp.concatenate(ws, axis=0), preferred_element_type=F32)
            c_ref[...] = c
            return j, jnp.max(c)

        lax.while_loop(cond, body, (first_block(t), c_max))

    def write_out(t, acc):
        row_o = lax.broadcasted_iota(jnp.int32, (LANES, half), 0)
        out_t = jnp.concatenate(
            [jnp.where(row_o < SB_HEAD_DIM, acc[:, g * gw:g * gw + half],
                       acc[:, g * gw + half:(g + 1) * gw]) for g in range(2)], axis=1)
        o_ref[pl.ds(pl.multiple_of(t * tq, tq), tq), :] = out_t.T.astype(o_ref.dtype)

    def finish(t, with_extra):
        vtb = vt_ref[0, :, pl.ds(first_key(t), n_sub * sub)]
        acc = jnp.dot(vtb, w_ref[...], preferred_element_type=F32)
        if with_extra:
            acc = acc + accx_ref[...]
        write_out(t, acc)

    w_ref[...] = jnp.zeros_like(w_ref)
    w0_ref[...] = jnp.zeros_like(w0_ref)
    z_ref[...] = scores(0)
    first_step(0, w0_ref, False)
    z_ref[...] = scores(1)
    write_out(0, jnp.dot(vt_ref[0, :, 0:tq], w0_ref[...], preferred_element_type=F32))

    def body(t, carry):
        finish(jnp.maximum(t - 1, 1), False)
        z_next = scores(jnp.minimum(t + 1, n_tiles - 1))
        first_step(SB_PRE_SUBS, w_ref, True)
        call_ref[t] = c_ref[...]
        z_ref[...] = z_next
        return carry

    lax.fori_loop(1, n_tiles, body, 0)
    finish(n_tiles - 1, False)

    @pl.when(jnp.max(call_ref[1:]) > F32_EXP2_UNDERFLOW)
    def _():
        def redo(t, carry):
            @pl.when(jnp.max(call_ref[t]) > F32_EXP2_UNDERFLOW)
            def _():
                z_ref[...] = scores(t)
                accx_ref[...] = jnp.zeros_like(accx_ref)
                extra_steps(t, first_step(SB_PRE_SUBS, w_ref, False))
                finish(t, True)
            return carry

        lax.fori_loop(1, n_tiles, redo, 0)


def _sb_attention(qt, k, vt, batch, seq):
    n, d = k.shape
    tq = SB_Q_TILE
    rows = SB_PRE_SUBS * SB_SUB + tq
    assert seq >= rows and seq // tq >= 2
    return pl.pallas_call(
        _sb_attn_kernel,
        grid=(batch, d // LANES),
        in_specs=[pl.BlockSpec((1, LANES, seq), lambda b, p: (b, p, 0)),
                  pl.BlockSpec((seq, LANES), lambda b, p: (b, p)),
                  pl.BlockSpec((1, LANES, seq), lambda b, p: (b, p, 0))],
        out_specs=pl.BlockSpec((seq, LANES), lambda b, p: (b, p)),
        out_shape=jax.ShapeDtypeStruct((n, d), BF16),
        scratch_shapes=[pltpu.VMEM((rows, 2 * tq), F32), pltpu.VMEM((rows, 2 * tq), BF16),
                        pltpu.VMEM((tq, 2 * tq), BF16),
                        pltpu.VMEM((1, 2 * tq), F32), pltpu.VMEM((LANES, 2 * tq), F32),
                        pltpu.VMEM((seq // tq, 1, 2 * tq), F32)],
        compiler_params=pltpu.CompilerParams(dimension_semantics=("arbitrary",) * 2,
                                             vmem_limit_bytes=VMEM_LIMIT_BYTES),
        name="sb_attention",
    )(qt, k, vt)


def _diff_attn_kernel(qt_ref, k_ref, vt_ref, lq1_ref, lk1_ref, lq2_ref, lk2_ref, sub_ref, o_ref,
                      m_ref, l_ref, acc_ref, alpha_ref, s_ref, p_ref, smax_ref, *, lam_init):
    t = DIFF_TILE
    n_tiles = k_ref.shape[0] // t

    def q2x(i):
        qt = qt_ref[0, :, pl.ds(pl.multiple_of(i * t, t), t)]
        row = lax.broadcasted_iota(jnp.int32, qt.shape, 0)
        zero = jnp.zeros_like(qt)
        return jnp.concatenate([jnp.where(row < DIFF_HEAD_DIM, qt, zero),
                                jnp.where(row >= DIFF_HEAD_DIM, qt, zero)], axis=1)

    def scores(i, j):
        kb = k_ref[pl.ds(pl.multiple_of(j * t, t), t), :]
        return jnp.dot(kb, q2x(i), preferred_element_type=F32)

    def stage_scores(s):
        s_ref[...] = s
        smax_ref[...] = jnp.max(s.reshape(t // 8, 8, 2 * t), axis=0)

    def softmax_next(i):
        s = s_ref[...]
        m_old = m_ref[i]
        m_new = jnp.maximum(m_old, jnp.max(smax_ref[...], axis=0, keepdims=True))
        p = jnp.exp2(s - m_new)
        alpha = jnp.exp2(m_old - m_new)
        l_ref[i] = alpha * l_ref[i] + jnp.sum(p, axis=0, keepdims=True)
        p_ref[...] = p.astype(BF16)
        alpha_ref[...] = alpha
        m_ref[i] = m_new

    def pv(j):
        vtb = vt_ref[0, :, pl.ds(pl.multiple_of(j * t, t), t)]
        return jnp.dot(vtb, p_ref[...], preferred_element_type=F32)

    strip = DIFF_STRIP
    n_strips = t // strip

    def late(x, r):
        return x if r == 0 else jnp.concatenate([x[:, r * strip:t], x[:, t + r * strip:]], axis=1)

    def merge(full, part, r):
        if r == 0:
            return part
        w = t - r * strip
        return jnp.concatenate([full[:, 0:r * strip], part[:, 0:w], full[:, t:t + r * strip],
                                part[:, w:]], axis=1)

    def rows(r):
        return slice(r * strip, (r + 1) * strip)

    def scores_first(i):
        start = pl.multiple_of(i * t, t)
        q = q2x(i)
        return [jnp.dot(k_ref[pl.ds(start + r * strip, strip), :], late(q, r),
                        preferred_element_type=F32) for r in range(n_strips)]

    def stage_first(pieces):
        for r, piece in enumerate(pieces):
            s_ref[rows(r), 0:piece.shape[1]] = piece

    def softmax_first(i):
        pieces = []
        for r in range(n_strips):
            w = t - r * strip
            sr = s_ref[rows(r), 0:2 * w]
            key_chunk = lax.broadcasted_iota(jnp.int32, sr.shape, 0) // CHUNK
            q_local = lax.broadcasted_iota(jnp.int32, sr.shape, 1) % w
            pieces.append(jnp.where((q_local >= strip) | (key_chunk <= q_local // CHUNK), sr, NEG_INF))
        m_new = jnp.max(pieces[0], axis=0, keepdims=True)
        for r in range(1, n_strips):
            m_r = jnp.maximum(late(m_new, r), jnp.max(pieces[r], axis=0, keepdims=True))
            m_new = merge(m_new, m_r, r)
        l_new = None
        for r in range(n_strips):
            p = jnp.exp2(pieces[r] - late(m_new, r))
            p_ref[rows(r), 0:p.shape[1]] = p.astype(BF16)
            l_r = jnp.sum(p, axis=0, keepdims=True)
            l_new = l_r if r == 0 else merge(l_new, late(l_new, r) + l_r, r)
        l_ref[i] = l_new
        m_ref[i] = m_new

    def pv_first(j):
        start = pl.multiple_of(j * t, t)
        acc = None
        for r in range(n_strips):
            w = t - r * strip
            d = jnp.dot(vt_ref[0, :, pl.ds(start + r * strip, strip)], p_ref[rows(r), 0:2 * w],
                        preferred_element_type=F32)
            acc = d if r == 0 else merge(acc, late(acc, r) + d, r)
        return acc

    stage_first(scores_first(0))
    softmax_first(0)
    stage_first(scores_first(1))

    def diag_body(i, carry):
        acc_ref[i - 1] = pv_first(i - 1)
        s_next = scores_first(jnp.minimum(i + 1, n_tiles - 1))
        softmax_first(i)
        stage_first(s_next)
        return carry

    lax.fori_loop(1, n_tiles, diag_body, 0)
    acc_ref[n_tiles - 1] = pv_first(n_tiles - 1)

    def next_pair(i, j):
        wrap = j + 1 >= i
        return jnp.where(wrap, i + 1, i), jnp.where(wrap, 0, j + 1)

    stage_scores(scores(1, 0))
    softmax_next(1)
    i1, j1 = next_pair(1, 0)
    i1c = jnp.minimum(i1, n_tiles - 1)
    stage_scores(scores(i1c, jnp.minimum(j1, i1c - 1)))
    n_pairs = n_tiles * (n_tiles - 1) // 2

    def pair_body(n, carry):
        ip, jp, i, j = carry
        i2, j2 = next_pair(i, j)
        i2c = jnp.minimum(i2, n_tiles - 1)
        kb = k_ref[pl.ds(pl.multiple_of(jnp.minimum(j2, i2c - 1) * t, t), t), :]
        q_next = q2x(i2c)
        s_map1 = jnp.dot(kb, q_next[:, :t], preferred_element_type=F32)
        acc_ref[ip] = alpha_ref[...] * acc_ref[ip] + pv(jp)
        s_map2 = jnp.dot(kb, q_next[:, t:], preferred_element_type=F32)
        s_next = jnp.concatenate([s_map1, s_map2], axis=1)
        softmax_next(i)
        stage_scores(s_next)
        return i, j, i2, j2

    ip, jp = 1, 0
    if n_pairs >= 2:
        ip, jp, i, j = lax.fori_loop(1, n_pairs - 1, pair_body, (1, 0, i1, j1))
        acc_ref[ip] = alpha_ref[...] * acc_ref[ip] + pv(jp)
        softmax_next(i)
        ip, jp = i, j
    acc_ref[ip] = alpha_ref[...] * acc_ref[ip] + pv(jp)

    lam = (jnp.exp(jnp.sum(lq1_ref[...] * lk1_ref[...], axis=-1, keepdims=True))
           - jnp.exp(jnp.sum(lq2_ref[...] * lk2_ref[...], axis=-1, keepdims=True)) + lam_init)

    group = DIFF_OUT_GROUP if n_tiles % DIFF_OUT_GROUP == 0 else 1

    def out_body(ig, carry):
        for u in range(group):
            i = ig * group + u
            on = acc_ref[i] / l_ref[i]
            o = on[:, :t] - lam * on[:, t:]
            ms = jnp.mean(o * o, axis=0, keepdims=True)
            o = o * lax.rsqrt(ms + EPS)
            o = o.T * sub_ref[...] * (1.0 - lam_init)
            o_ref[pl.ds(pl.multiple_of(i * t, t), t), :] = o.astype(o_ref.dtype)
        return carry

    lax.fori_loop(0, n_tiles // group, out_body, 0)


def _diff_attention(qt, k, vt, lq1, lk1, lq2, lk2, sub_g, batch, seq, lam_init):
    n, d = k.shape
    t = DIFF_TILE
    n_tiles = seq // t
    assert n_tiles >= 2
    small = _const_spec((1, DIFF_HEAD_DIM))
    return pl.pallas_call(
        functools.partial(_diff_attn_kernel, lam_init=lam_init),
        grid=(batch, d // LANES),
        in_specs=[pl.BlockSpec((1, LANES, seq), lambda b, h: (b, h, 0)),
                  pl.BlockSpec((seq, LANES), lambda b, h: (b, h)),
                  pl.BlockSpec((1, LANES, seq), lambda b, h: (b, h, 0)),
                  small, small, small, small, _const_spec((1, LANES))],
        out_specs=pl.BlockSpec((seq, LANES), lambda b, h: (b, h)),
        out_shape=jax.ShapeDtypeStruct((n, d), BF16),
        scratch_shapes=[pltpu.VMEM((n_tiles, 1, 2 * t), F32), pltpu.VMEM((n_tiles, 1, 2 * t), F32),
                        pltpu.VMEM((n_tiles, LANES, 2 * t), F32), pltpu.VMEM((1, 2 * t), F32),
                        pltpu.VMEM((t, 2 * t), F32), pltpu.VMEM((t, 2 * t), BF16),
                        pltpu.VMEM((8, 2 * t), F32)],
        compiler_params=pltpu.CompilerParams(dimension_semantics=("arbitrary",) * 2,
                                             vmem_limit_bytes=VMEM_LIMIT_BYTES),
        name="diff_attention",
    )(qt, k, vt, lq1, lk1, lq2, lk2, sub_g)


def _mix_ffn_kernel(x_ref, o_ref, wo_ref, g_ref, wg_ref, wu_ref, wd_ref, out_ref):
    tm = x_ref.shape[0]
    halves = (slice(0, tm // 2), slice(tm // 2, tm))
    x1 = [x_ref[r, :] + jnp.dot(o_ref[r, :], wo_ref[0], preferred_element_type=F32) for r in halves]
    hs = [_rms(v, g_ref[...]).astype(BF16) for v in x1]
    for r, acc, h in zip(halves, x1, hs):
        for lo, hi in FFN_CHUNKS:
            g = jnp.dot(h, wg_ref[0, :, lo:hi], preferred_element_type=F32)
            u = jnp.dot(h, wu_ref[0, :, lo:hi], preferred_element_type=F32)
            a = (g * (1.0 / (1.0 + jnp.exp(-g))) * u).astype(BF16)
            acc = acc + jnp.dot(a, wd_ref[0, lo:hi, :], preferred_element_type=F32)
        out_ref[r, :] = acc


def _mix_ffn(x2, o, gain, wo, wg, wu, wd, layer):
    n, d = x2.shape
    tm = TOKEN_TILE
    assert FFN_CHUNKS[-1][1] == wg.shape[2]
    row_spec = pl.BlockSpec((tm, d), lambda i: (i, 0))
    return pl.pallas_call(
        _mix_ffn_kernel,
        grid=(n // tm,),
        in_specs=[row_spec, row_spec, _layer_spec(wo, layer), _const_spec((1, d)),
                  _layer_spec(wg, layer), _layer_spec(wu, layer), _layer_spec(wd, layer)],
        out_specs=row_spec,
        out_shape=jax.ShapeDtypeStruct((n, d), F32),
        compiler_params=pltpu.CompilerParams(dimension_semantics=("arbitrary",),
                                             vmem_limit_bytes=VMEM_LIMIT_BYTES),
        name="mix_ffn",
    )(x2, o, wo, gain, wg, wu, wd)


def _rope_tables(positions):
    inv_freq = ROPE_THETA ** (-jnp.arange(0, ROT_DIM, 2, dtype=F32) / ROT_DIM)
    ang = positions.astype(F32)[:, None, :] * inv_freq[None, :, None]
    return jnp.cos(ang), jnp.sin(ang)


def kernel(x, positions, attn_norm, w_in, w_out, q_norm, k_norm, lambda_q1, lambda_k1, lambda_q2,
           lambda_k2, sub_norm, ffn_norm, w_gate, w_up, w_down):
    batch, seq, d = x.shape
    depth = w_in.shape[0]
    assert seq % QKV_TILE == 0 and seq % TOKEN_TILE == 0
    assert seq % SB_Q_TILE == 0 and seq % DIFF_TILE == 0
    assert d % LANES == 0 and DIFF_V_DIM == LANES and DIFF_TILE % CHUNK == 0
    x2 = x.reshape(batch * seq, d)
    tables = _rope_tables(positions)
    w_in_t = jnp.swapaxes(w_in, 1, 2).astype(BF16)
    wo, wg, wu, wd = (w.astype(BF16) for w in (w_out, w_gate, w_up, w_down))
    for i in range(depth):
        gain = attn_norm[i].reshape(1, d)
        if i % N_MIXERS == 0:
            qt, k, vt = _qkv_proj(x2, gain, w_in_t, i, batch, seq, SB_HEAD_DIM ** -0.5 * LOG2E)
            o = _sb_attention(qt, k, vt, batch, seq)
        else:
            j = i // N_MIXERS
            qg = jnp.broadcast_to(q_norm[j].reshape(-1, 1), (DIFF_HEAD_DIM, LANES))
            kg = jnp.broadcast_to(k_norm[j].reshape(-1, 1), (DIFF_HEAD_DIM, LANES))
            qt, k, vt = _qkv_proj(x2, gain, w_in_t, i, batch, seq, DIFF_HEAD_DIM ** -0.5 * LOG2E,
                                  (qg, kg) + tables)
            lam_init = 0.8 - 0.6 * math.exp(-0.3 * i)
            o = _diff_attention(qt, k, vt, lambda_q1[j].reshape(1, -1), lambda_k1[j].reshape(1, -1),
                                lambda_q2[j].reshape(1, -1), lambda_k2[j].reshape(1, -1),
                                sub_norm[j].reshape(1, -1), batch, seq, lam_init)
        x2 = _mix_ffn(x2, o, ffn_norm[i].reshape(1, d), wo, wg, wu, wd, i)
    return x2.reshape(batch, seq, d)
```

```python
import functools
import math

import jax
import jax.numpy as jnp
from jax import lax
from jax.experimental import pallas as pl
from jax.experimental.pallas import tpu as pltpu

F32 = jnp.float32
BF16 = jnp.bfloat16

N_MIXERS = 2
SB_HEAD_DIM = 64
DIFF_HEAD_DIM = 64
DIFF_V_DIM = 2 * DIFF_HEAD_DIM
CHUNK = 64
ROPE_THETA = 500000.0
ROT_DIM = DIFF_HEAD_DIM // 4
EPS = 1e-6
NEG_INF = -1e30

LANES = 128
VMEM_LIMIT_BYTES = 56 * 1024 * 1024

QKV_TILE = 1024
TOKEN_TILE = 512
SB_Q_TILE = 256
SB_SUB = 128
SB_PRE_SUBS = 2
SB_NEXT_SUBS = 2
DIFF_TILE = 512
DIFF_STRIP = 128
DIFF_OUT_GROUP = 4
FFN_CHUNKS = ((0, 1024), (1024, 2048), (2048, 2816))

LOG2E = 1.4426950408889634
F32_EXP2_UNDERFLOW = -150.0


def _rms(x, gain):
    ms = jnp.mean(x * x, axis=-1, keepdims=True)
    return x * lax.rsqrt(ms + EPS) * gain


def _const_spec(shape):
    nd = len(shape)
    return pl.BlockSpec(shape, lambda *_: (0,) * nd, pipeline_mode=pl.Buffered(1))


def _layer_spec(stacked, layer):
    return pl.BlockSpec((1,) + stacked.shape[1:], lambda *_: (layer, 0, 0),
                        pipeline_mode=pl.Buffered(1))


def _project_t(wt_ref, h, part, d_model):
    w = wt_ref[0, part * d_model:(part + 1) * d_model, :]
    return lax.dot_general(w, h, (((1,), (1,)), ((), ())), preferred_element_type=F32)


def _qkv_kernel(x_ref, g_ref, wt_ref, qt_ref, k_ref, vt_ref, *, d_model, scale):
    h = _rms(x_ref[...], g_ref[...]).astype(BF16)
    kt = _project_t(wt_ref, h, 1, d_model)
    for c in range(d_model // LANES):
        sl = slice(c * LANES, (c + 1) * LANES)
        k_ref[:, sl] = kt[sl, :].T.astype(BF16)
    qt_ref[0] = (_project_t(wt_ref, h, 0, d_model) * scale).astype(BF16)
    vt_ref[0] = _project_t(wt_ref, h, 2, d_model).astype(BF16)


def _qkv_diff_kernel(x_ref, g_ref, wt_ref, qg_ref, kg_ref, cost_ref, sint_ref,
                     qt_ref, k_ref, vt_ref, *, d_model, scale):
    h = _rms(x_ref[...], g_ref[...]).astype(BF16)
    tm = h.shape[0]
    cos_t, sin_t = cost_ref[0], sint_ref[0]
    half = ROT_DIM // 2

    project = functools.partial(_project_t, wt_ref, h, d_model=d_model)

    def norm_rope(xg, gain):
        ms = jnp.mean(xg * xg, axis=0, keepdims=True)
        y = xg * lax.rsqrt(ms + EPS) * gain
        x1, x2 = y[:half], y[half:ROT_DIM]
        return jnp.concatenate([x1 * cos_t - x2 * sin_t, x2 * cos_t + x1 * sin_t, y[ROT_DIM:]],
                               axis=0)

    def slabs(xt, gain):
        per_slab = LANES // DIFF_HEAD_DIM
        for c in range(d_model // LANES):
            parts = [norm_rope(xt[g * DIFF_HEAD_DIM:(g + 1) * DIFF_HEAD_DIM, :], gain)
                     for g in range(c * per_slab, (c + 1) * per_slab)]
            yield slice(c * LANES, (c + 1) * LANES), jnp.concatenate(parts, axis=0)

    for sl, y in slabs(project(1), jnp.tile(kg_ref[...], (1, tm // LANES))):
        k_ref[:, sl] = y.T.astype(BF16)
    for sl, y in slabs(project(0), jnp.tile(qg_ref[...], (1, tm // LANES))):
        qt_ref[0, sl, :] = (y * scale).astype(BF16)
    vt_ref[0] = project(2).astype(BF16)


def _qkv_proj(x2, gain, wt, layer, batch, seq, scale, diff_args=None):
    n, d = x2.shape
    tm = QKV_TILE
    tiles_per_seq = seq // tm
    row_spec = pl.BlockSpec((tm, d), lambda i: (i, 0))
    t_spec = pl.BlockSpec((1, d, tm), lambda i: (i // tiles_per_seq, 0, i % tiles_per_seq))
    row_shape = jax.ShapeDtypeStruct((n, d), BF16)
    t_shape = jax.ShapeDtypeStruct((batch, d, seq), BF16)
    if diff_args is None:
        body = functools.partial(_qkv_kernel, d_model=d, scale=scale)
        name = "qkv_sb"
        in_specs = [row_spec, _const_spec((1, d)), _layer_spec(wt, layer)]
        args = [x2, gain, wt]
    else:
        qg, kg, cos_t, sin_t = diff_args
        tabt_spec = pl.BlockSpec((1, ROT_DIM // 2, tm),
                                 lambda i: (i // tiles_per_seq, 0, i % tiles_per_seq))
        body = functools.partial(_qkv_diff_kernel, d_model=d, scale=scale)
        name = "qkv_diff"
        in_specs = [row_spec, _const_spec((1, d)), _layer_spec(wt, layer),
                    _const_spec(qg.shape), _const_spec(kg.shape), tabt_spec, tabt_spec]
        args = [x2, gain, wt, qg, kg, cos_t, sin_t]
    return pl.pallas_call(
        body,
        grid=(n // tm,),
        in_specs=in_specs,
        out_specs=[t_spec, row_spec, t_spec],
        out_shape=[t_shape, row_shape, t_shape],
        compiler_params=pltpu.CompilerParams(dimension_semantics=("arbitrary",),
                                             vmem_limit_bytes=VMEM_LIMIT_BYTES),
        name=name,
    )(*args)


def _sb_attn_kernel(qt_ref, k_ref, vt_ref, o_ref, z_ref, w_ref, w0_ref, c_ref, accx_ref, call_ref):
    tq, sub = SB_Q_TILE, SB_SUB
    half = tq // 2
    gw = 2 * half
    n_tiles = k_ref.shape[0] // tq
    n_sub = SB_PRE_SUBS + tq // sub
    r_u = lax.broadcasted_iota(jnp.int32, (sub, 2 * sub), 0)
    c_u = lax.broadcasted_iota(jnp.int32, (sub, 2 * sub), 1) % sub
    later2 = jnp.where(c_u > r_u, 1.0, 0.0).astype(BF16)
    tri = (lax.broadcasted_iota(jnp.int32, (sub, gw), 0)
           < lax.broadcasted_iota(jnp.int32, (sub, gw), 1) % half)

    def q4x(t):
        qt = qt_ref[0, :, pl.ds(pl.multiple_of(t * tq, tq), tq)]
        row = lax.broadcasted_iota(jnp.int32, (LANES, half), 0)
        zero = jnp.zeros((LANES, half), qt.dtype)
        parts = []
        for g in range(2):
            qg = qt[:, g * half:(g + 1) * half]
            parts += [jnp.where(row < SB_HEAD_DIM, qg, zero), jnp.where(row >= SB_HEAD_DIM, qg, zero)]
        return jnp.concatenate(parts, axis=1)

    def first_block(t):
        return jnp.maximum(t * (tq // sub) - SB_PRE_SUBS, 0)

    def first_key(t):
        return pl.multiple_of(first_block(t) * sub, sub)

    def scores(t):
        kb = k_ref[pl.ds(first_key(t), n_sub * sub), :]
        return jnp.dot(kb, q4x(t), preferred_element_type=F32)

    def sub_block(z, c, mask):
        if mask is not None:
            z = jnp.where(mask, z, NEG_INF)
        nabs = pltpu.bitcast(pltpu.bitcast(z, jnp.uint32) | jnp.uint32(0x80000000), F32)
        sp = jnp.log(1.0 + jnp.exp2(nabs)) * LOG2E
        ls = jnp.minimum(z, 0.0) - sp
        lk = ls - z
        hi = lk.astype(BF16)
        lo = (lk - hi.astype(F32)).astype(BF16)
        after = jnp.dot(later2, jnp.concatenate([hi, lo], axis=0), preferred_element_type=F32)
        w = jnp.exp2(ls + after + c)
        return w.astype(BF16), c + after[0:1, :] + lk[0:1, :]

    def first_step(first_sb, dst_ref, skip_first):
        zero_c = jnp.zeros((1, gw), F32)
        lo_rows = slice(first_sb * sub, (first_sb + 1) * sub)
        hi_rows = slice((first_sb + 1) * sub, (first_sb + 2) * sub)
        wb, cb = sub_block(z_ref[hi_rows, gw:], zero_c, tri)
        dst_ref[hi_rows, gw:] = wb
        wa, ca = sub_block(z_ref[lo_rows, :gw], zero_c, tri)
        wb, cb = sub_block(z_ref[lo_rows, gw:], cb, None)
        dst_ref[lo_rows, :gw] = wa
        dst_ref[lo_rows, gw:] = wb
        c = jnp.concatenate([ca, cb], axis=1)
        for sb in reversed(range(1 if skip_first else 0, first_sb)):
            w, c = sub_block(z_ref[sb * sub:(sb + 1) * sub, :], c, None)
            dst_ref[sb * sub:(sb + 1) * sub, :] = w
        if skip_first:
            wa, ca = sub_block(z_ref[0:sub, :gw], c[:, :gw], None)
            dst_ref[0:sub, :gw] = wa
            c = jnp.concatenate([ca, c[:, gw:]], axis=1)
        c_ref[...] = c
        return jnp.max(c)

    def extra_steps(t, c_max):
        def cond(carry):
            j, c_max = carry
            return (j > 0) & (c_max > F32_EXP2_UNDERFLOW)

        def body(carry):
            j, _ = carry
            j = j - SB_NEXT_SUBS
            start = pl.multiple_of(j * sub, sub)
            kb = k_ref[pl.ds(start, SB_NEXT_SUBS * sub), :]
            z_all = jnp.dot(kb, q4x(t), preferred_element_type=F32)
            c = c_ref[...]
            ws = [None] * SB_NEXT_SUBS
            for sb in reversed(range(SB_NEXT_SUBS)):
                ws[sb], c = sub_block(z_all[sb * sub:(sb + 1) * sub], c, None)
            vtb = vt_ref[0, :, pl.ds(start, SB_NEXT_SUBS * sub)]
            accx_ref[...] += jnp.dot(vtb, jnp.concatenate(ws, axis=0), preferred_element_type=F32)
            c_ref[...] = c
            return j, jnp.max(c)

        lax.while_loop(cond, body, (first_block(t), c_max))

    def write_out(t, acc):
        row_o = lax.broadcasted_iota(jnp.int32, (LANES, half), 0)
        out_t = jnp.concatenate(
            [jnp.where(row_o < SB_HEAD_DIM, acc[:, g * gw:g * gw + half],
                       acc[:, g * gw + half:(g + 1) * gw]) for g in range(2)], axis=1)
        o_ref[pl.ds(pl.multiple_of(t * tq, tq), tq), :] = out_t.T.astype(o_ref.dtype)

    def finish(t, with_extra):
        vtb = vt_ref[0, :, pl.ds(first_key(t), n_sub * sub)]
        acc = jnp.dot(vtb, w_ref[...], preferred_element_type=F32)
        if with_extra:
            acc = acc + accx_ref[...]
        write_out(t, acc)

    w_ref[...] = jnp.zeros_like(w_ref)
    w0_ref[...] = jnp.zeros_like(w0_ref)
    z_ref[...] = scores(0)
    first_step(0, w0_ref, False)
    z_ref[...] = scores(1)
    write_out(0, jnp.dot(vt_ref[0, :, 0:tq], w0_ref[...], preferred_element_type=F32))

    def body(t, carry):
        finish(jnp.maximum(t - 1, 1), False)
        z_next = scores(jnp.minimum(t + 1, n_tiles - 1))
        first_step(SB_PRE_SUBS, w_ref, True)
        call_ref[t] = c_ref[...]
        z_ref[...] = z_next
        return carry

    lax.fori_loop(1, n_tiles, body, 0)
    finish(n_tiles - 1, False)

    @pl.when(jnp.max(call_ref[1:]) > F32_EXP2_UNDERFLOW)
    def _():
        def redo(t, carry):
            @pl.when(jnp.max(call_ref[t]) > F32_EXP2_UNDERFLOW)
            def _():
                z_ref[...] = scores(t)
                accx_ref[...] = jnp.zeros_like(accx_ref)
                extra_steps(t, first_step(SB_PRE_SUBS, w_ref, False))
                finish(t, True)
            return carry

        lax.fori_loop(1, n_tiles, redo, 0)


def _sb_attention(qt, k, vt, batch, seq):
    n, d = k.shape
    tq = SB_Q_TILE
    rows = SB_PRE_SUBS * SB_SUB + tq
    assert seq >= rows and seq // tq >= 2
    return pl.pallas_call(
        _sb_attn_kernel,
        grid=(batch, d // LANES),
        in_specs=[pl.BlockSpec((1, LANES, seq), lambda b, p: (b, p, 0)),
                  pl.BlockSpec((seq, LANES), lambda b, p: (b, p)),
                  pl.BlockSpec((1, LANES, seq), lambda b, p: (b, p, 0))],
        out_specs=pl.BlockSpec((seq, LANES), lambda b, p: (b, p)),
        out_shape=jax.ShapeDtypeStruct((n, d), BF16),
        scratch_shapes=[pltpu.VMEM((rows, 2 * tq), F32), pltpu.VMEM((rows, 2 * tq), BF16),
                        pltpu.VMEM((tq, 2 * tq), BF16),
                        pltpu.VMEM((1, 2 * tq), F32), pltpu.VMEM((LANES, 2 * tq), F32),
                        pltpu.VMEM((seq // tq, 1, 2 * tq), F32)],
        compiler_params=pltpu.CompilerParams(dimension_semantics=("arbitrary",) * 2,
                                             vmem_limit_bytes=VMEM_LIMIT_BYTES),
        name="sb_attention",
    )(qt, k, vt)


def _diff_attn_kernel(qt_ref, k_ref, vt_ref, lq1_ref, lk1_ref, lq2_ref, lk2_ref, sub_ref, o_ref,
                      m_ref, l_ref, acc_ref, alpha_ref, s_ref, p_ref, smax_ref, *, lam_init):
    t = DIFF_TILE
    n_tiles = k_ref.shape[0] // t

    def q2x(i):
        qt = qt_ref[0, :, pl.ds(pl.multiple_of(i * t, t), t)]
        row = lax.broadcasted_iota(jnp.int32, qt.shape, 0)
        zero = jnp.zeros_like(qt)
        return jnp.concatenate([jnp.where(row < DIFF_HEAD_DIM, qt, zero),
                                jnp.where(row >= DIFF_HEAD_DIM, qt, zero)], axis=1)

    def scores(i, j):
        kb = k_ref[pl.ds(pl.multiple_of(j * t, t), t), :]
        return jnp.dot(kb, q2x(i), preferred_element_type=F32)

    def stage_scores(s):
        s_ref[...] = s
        smax_ref[...] = jnp.max(s.reshape(t // 8, 8, 2 * t), axis=0)

    def softmax_next(i):
        s = s_ref[...]
        m_old = m_ref[i]
        m_new = jnp.maximum(m_old, jnp.max(smax_ref[...], axis=0, keepdims=True))
        p = jnp.exp2(s - m_new)
        alpha = jnp.exp2(m_old - m_new)
        l_ref[i] = alpha * l_ref[i] + jnp.sum(p, axis=0, keepdims=True)
        p_ref[...] = p.astype(BF16)
        alpha_ref[...] = alpha
        m_ref[i] = m_new

    def pv(j):
        vtb = vt_ref[0, :, pl.ds(pl.multiple_of(j * t, t), t)]
        return jnp.dot(vtb, p_ref[...], preferred_element_type=F32)

    strip = DIFF_STRIP
    n_strips = t // strip

    def late(x, r):
        return x if r == 0 else jnp.concatenate([x[:, r * strip:t], x[:, t + r * strip:]], axis=1)

    def merge(full, part, r):
        if r == 0:
            return part
        w = t - r * strip
        return jnp.concatenate([full[:, 0:r * strip], part[:, 0:w], full[:, t:t + r * strip],
                                part[:, w:]], axis=1)

    def rows(r):
        return slice(r * strip, (r + 1) * strip)

    def scores_first(i):
        start = pl.multiple_of(i * t, t)
        q = q2x(i)
        return [jnp.dot(k_ref[pl.ds(start + r * strip, strip), :], late(q, r),
                        preferred_element_type=F32) for r in range(n_strips)]

    def stage_first(pieces):
        for r, piece in enumerate(pieces):
            s_ref[rows(r), 0:piece.shape[1]] = piece

    def softmax_first(i):
        pieces = []
        for r in range(n_strips):
            w = t - r * strip
            sr = s_ref[rows(r), 0:2 * w]
            key_chunk = lax.broadcasted_iota(jnp.int32, sr.shape, 0) // CHUNK
            q_local = lax.broadcasted_iota(jnp.int32, sr.shape, 1) % w
            pieces.append(jnp.where((q_local >= strip) | (key_chunk <= q_local // CHUNK), sr, NEG_INF))
        m_new = jnp.max(pieces[0], axis=0, keepdims=True)
        for r in range(1, n_strips):
            m_r = jnp.maximum(late(m_new, r), jnp.max(pieces[r], axis=0, keepdims=True))
            m_new = merge(m_new, m_r, r)
        l_new = None
        for r in range(n_strips):
            p = jnp.exp2(pieces[r] - late(m_new, r))
            p_ref[rows(r), 0:p.shape[1]] = p.astype(BF16)
            l_r = jnp.sum(p, axis=0, keepdims=True)
            l_new = l_r if r == 0 else merge(l_new, late(l_new, r) + l_r, r)
        l_ref[i] = l_new
        m_ref[i] = m_new

    def pv_first(j):
        start = pl.multiple_of(j * t, t)
        acc = None
        for r in range(n_strips):
            w = t - r * strip
            d = jnp.dot(vt_ref[0, :, pl.ds(start + r * strip, strip)], p_ref[rows(r), 0:2 * w],
                        preferred_element_type=F32)
            acc = d if r == 0 else merge(acc, late(acc, r) + d, r)
        return acc

    stage_first(scores_first(0))
    softmax_first(0)
    stage_first(scores_first(1))

    def diag_body(i, carry):
        acc_ref[i - 1] = pv_first(i - 1)
        s_next = scores_first(jnp.minimum(i + 1, n_tiles - 1))
        softmax_first(i)
        stage_first(s_next)
        return carry

    lax.fori_loop(1, n_tiles - 1, diag_body, 0)
    acc_ref[n_tiles - 2] = pv_first(n_tiles - 2)
    softmax_first(n_tiles - 1)
    acc_ref[n_tiles - 1] = pv_first(n_tiles - 1)

    def next_pair(i, j):
        wrap = j + 1 >= i
        return jnp.where(wrap, i + 1, i), jnp.where(wrap, 0, j + 1)

    stage_scores(scores(1, 0))
    softmax_next(1)
    i1, j1 = next_pair(1, 0)
    i1c = jnp.minimum(i1, n_tiles - 1)
    stage_scores(scores(i1c, jnp.minimum(j1, i1c - 1)))
    n_pairs = n_tiles * (n_tiles - 1) // 2

    def pair_body(n, carry):
        ip, jp, i, j = carry
        i2, j2 = next_pair(i, j)
        i2c = jnp.minimum(i2, n_tiles - 1)
        kb = k_ref[pl.ds(pl.multiple_of(jnp.minimum(j2, i2c - 1) * t, t), t), :]
        q_next = q2x(i2c)
        vtb = vt_ref[0, :, pl.ds(pl.multiple_of(jp * t, t), t)]
        s_map1 = jnp.dot(kb, q_next[:, :t], preferred_element_type=F32)
        acc_ref[ip, :, 0:t] = (alpha_ref[:, 0:t] * acc_ref[ip, :, 0:t]
                               + jnp.dot(vtb, p_ref[:, 0:t], preferred_element_type=F32))
        s_map2 = jnp.dot(kb, q_next[:, t:], preferred_element_type=F32)
        acc_ref[ip, :, t:] = (alpha_ref[:, t:] * acc_ref[ip, :, t:]
                              + jnp.dot(vtb, p_ref[:, t:], preferred_element_type=F32))
        s_next = jnp.concatenate([s_map1, s_map2], axis=1)
        softmax_next(i)
        stage_scores(s_next)
        return i, j, i2, j2

    ip, jp = 1, 0
    if n_pairs >= 2:
        ip, jp, i, j = lax.fori_loop(1, n_pairs - 1, pair_body, (1, 0, i1, j1))
        acc_ref[ip] = alpha_ref[...] * acc_ref[ip] + pv(jp)
        softmax_next(i)
        ip, jp = i, j
    acc_ref[ip] = alpha_ref[...] * acc_ref[ip] + pv(jp)

    lam = (jnp.exp(jnp.sum(lq1_ref[...] * lk1_ref[...], axis=-1, keepdims=True))
           - jnp.exp(jnp.sum(lq2_ref[...] * lk2_ref[...], axis=-1, keepdims=True)) + lam_init)

    group = DIFF_OUT_GROUP if n_tiles % DIFF_OUT_GROUP == 0 else 1

    def out_body(ig, carry):
        for u in range(group):
            i = ig * group + u
            on = acc_ref[i] / l_ref[i]
            o = on[:, :t] - lam * on[:, t:]
            ms = jnp.mean(o * o, axis=0, keepdims=True)
            o = o * lax.rsqrt(ms + EPS)
            o = o.T * sub_ref[...] * (1.0 - lam_init)
            o_ref[pl.ds(pl.multiple_of(i * t, t), t), :] = o.astype(o_ref.dtype)
        return carry

    lax.fori_loop(0, n_tiles // group, out_body, 0)


def _diff_attention(qt, k, vt, lq1, lk1, lq2, lk2, sub_g, batch, seq, lam_init):
    n, d = k.shape
    t = DIFF_TILE
    n_tiles = seq // t
    assert n_tiles >= 2
    small = _const_spec((1, DIFF_HEAD_DIM))
    return pl.pallas_call(
        functools.partial(_diff_attn_kernel, lam_init=lam_init),
        grid=(batch, d // LANES),
        in_specs=[pl.BlockSpec((1, LANES, seq), lambda b, h: (b, h, 0)),
                  pl.BlockSpec((seq, LANES), lambda b, h: (b, h)),
                  pl.BlockSpec((1, LANES, seq), lambda b, h: (b, h, 0)),
                  small, small, small, small, _const_spec((1, LANES))],
        out_specs=pl.BlockSpec((seq, LANES), lambda b, h: (b, h)),
        out_shape=jax.ShapeDtypeStruct((n, d), BF16),
        scratch_shapes=[pltpu.VMEM((n_tiles, 1, 2 * t), F32), pltpu.VMEM((n_tiles, 1, 2 * t), F32),
                        pltpu.VMEM((n_tiles, LANES, 2 * t), F32), pltpu.VMEM((1, 2 * t), F32),
                        pltpu.VMEM((t, 2 * t), F32), pltpu.VMEM((t, 2 * t), BF16),
                        pltpu.VMEM((8, 2 * t), F32)],
        compiler_params=pltpu.CompilerParams(dimension_semantics=("arbitrary",) * 2,
                                             vmem_limit_bytes=VMEM_LIMIT_BYTES),
        name="diff_attention",
    )(qt, k, vt, lq1, lk1, lq2, lk2, sub_g)


def _mix_ffn_kernel(x_ref, o_ref, wo_ref, g_ref, wg_ref, wu_ref, wd_ref, out_ref):
    tm = x_ref.shape[0]
    halves = (slice(0, tm // 2), slice(tm // 2, tm))
    x1 = [x_ref[r, :] + jnp.dot(o_ref[r, :], wo_ref[0], preferred_element_type=F32) for r in halves]
    hs = [_rms(v, g_ref[...]).astype(BF16) for v in x1]
    for r, acc, h in zip(halves, x1, hs):
        for lo, hi in FFN_CHUNKS:
            g = jnp.dot(h, wg_ref[0, :, lo:hi], preferred_element_type=F32)
            u = jnp.dot(h, wu_ref[0, :, lo:hi], preferred_element_type=F32)
            a = (g * (1.0 / (1.0 + jnp.exp(-g))) * u).astype(BF16)
            acc = acc + jnp.dot(a, wd_ref[0, lo:hi, :], preferred_element_type=F32)
        out_ref[r, :] = acc


def _mix_ffn(x2, o, gain, wo, wg, wu, wd, layer):
    n, d = x2.shape
    tm = TOKEN_TILE
    assert FFN_CHUNKS[-1][1] == wg.shape[2]
    row_spec = pl.BlockSpec((tm, d), lambda i: (i, 0))
    return pl.pallas_call(
        _mix_ffn_kernel,
        grid=(n // tm,),
        in_specs=[row_spec, row_spec, _layer_spec(wo, layer), _const_spec((1, d)),
                  _layer_spec(wg, layer), _layer_spec(wu, layer), _layer_spec(wd, layer)],
        out_specs=row_spec,
        out_shape=jax.ShapeDtypeStruct((n, d), F32),
        compiler_params=pltpu.CompilerParams(dimension_semantics=("arbitrary",),
                                             vmem_limit_bytes=VMEM_LIMIT_BYTES),
        name="mix_ffn",
    )(x2, o, wo, gain, wg, wu, wd)


def _rope_tables(positions):
    inv_freq = ROPE_THETA ** (-jnp.arange(0, ROT_DIM, 2, dtype=F32) / ROT_DIM)
    ang = positions.astype(F32)[:, None, :] * inv_freq[None, :, None]
    return jnp.cos(ang), jnp.sin(ang)


def kernel(x, positions, attn_norm, w_in, w_out, q_norm, k_norm, lambda_q1, lambda_k1, lambda_q2,
           lambda_k2, sub_norm, ffn_norm, w_gate, w_up, w_down):
    batch, seq, d = x.shape
    depth = w_in.shape[0]
    assert seq % QKV_TILE == 0 and seq % TOKEN_TILE == 0
    assert seq % SB_Q_TILE == 0 and seq % DIFF_TILE == 0
    assert d % LANES == 0 and DIFF_V_DIM == LANES and DIFF_TILE % CHUNK == 0
    x2 = x.reshape(batch * seq, d)
    tables = _rope_tables(positions)
    w_in_t = jnp.swapaxes(w_in, 1, 2).astype(BF16)
    wo, wg, wu, wd = (w.astype(BF16) for w in (w_out, w_gate, w_up, w_down))
    for i in range(depth):
        gain = attn_norm[i].reshape(1, d)
        if i % N_MIXERS == 0:
            qt, k, vt = _qkv_proj(x2, gain, w_in_t, i, batch, seq, SB_HEAD_DIM ** -0.5 * LOG2E)
            o = _sb_attention(qt, k, vt, batch, seq)
        else:
            j = i // N_MIXERS
            qg = jnp.broadcast_to(q_norm[j].reshape(-1, 1), (DIFF_HEAD_DIM, LANES))
            kg = jnp.broadcast_to(k_norm[j].reshape(-1, 1), (DIFF_HEAD_DIM, LANES))
            qt, k, vt = _qkv_proj(x2, gain, w_in_t, i, batch, seq, DIFF_HEAD_DIM ** -0.5 * LOG2E,
                                  (qg, kg) + tables)
            lam_init = 0.8 - 0.6 * math.exp(-0.3 * i)
            o = _diff_attention(qt, k, vt, lambda_q1[j].reshape(1, -1), lambda_k1[j].reshape(1, -1),
                                lambda_q2[j].reshape(1, -1), lambda_k2[j].reshape(1, -1),
                                sub_norm[j].reshape(1, -1), batch, seq, lam_init)
        x2 = _mix_ffn(x2, o, ffn_norm[i].reshape(1, d), wo, wg, wu, wd, i)
    return x2.reshape(batch, seq, d)
```
